```python
import math
import jax
import jax.numpy as jnp
from jax import lax
import numpy as np

D_MODEL = 1024
BATCH = 2
SEQ = 8192
DEPTH = 4
DEC_BATCH = 128
DEC_SEQ = 8
PAST_LEN = 8192
PAGE_SIZE = 128

N_MIXERS = 3
N_GLA = (DEPTH + 2) // 3
N_SWA = (DEPTH + 1) // 3
N_LRU = DEPTH // 3

GLA_HEADS = 4
GLA_QK = D_MODEL // 2
GLA_V = D_MODEL
GLA_DK = GLA_QK // GLA_HEADS
GLA_DV = GLA_V // GLA_HEADS
GLA_GATE_RANK = 16
GLA_TAU = 16.0
GLA_CHUNK = 16
GLA_IN = 2 * GLA_QK + 2 * GLA_V + GLA_GATE_RANK

SWA_HEAD_DIM = 64
SWA_Q_HEADS = D_MODEL // SWA_HEAD_DIM
SWA_KV_HEADS = 4
SWA_GROUP = SWA_Q_HEADS // SWA_KV_HEADS
SWA_Q = SWA_Q_HEADS * SWA_HEAD_DIM
SWA_KV = SWA_KV_HEADS * SWA_HEAD_DIM
SWA_IN = SWA_Q + 2 * SWA_KV
WINDOW = 128

D_RNN = D_MODEL
LRU_BLOCKS = 4
LRU_BLOCK = D_RNN // LRU_BLOCKS
CONV_WIDTH = 4
LRU_C = 8.0

PEER_HEADS = 8
PEER_NKEYS = 128
PEER_EXPERTS = PEER_NKEYS * PEER_NKEYS
PEER_KEY_DIM = 256
PEER_HALF = PEER_KEY_DIM // 2
PEER_TOPK = 16
PEER_BLOCK = 256

RMS_EPS = 1e-6

kernel_name = 'hybrid_gla_swa_rglru_peer_adaln_step'


def rms_norm(x, g):
    xf = x.astype(jnp.float32)
    y = xf * lax.rsqrt(jnp.mean(xf * xf, axis=-1, keepdims=True) + RMS_EPS)
    return (y * g.astype(jnp.float32)).astype(x.dtype)


def gla_recurrence(q, k, v, log_a, s0):
    B, T, H, _ = q.shape
    DV = v.shape[-1]
    C = math.gcd(T, GLA_CHUNK)
    n = T // C

    def blk(t):
        return t.reshape(B, n, C, H, t.shape[-1]).transpose(1, 0, 3, 2, 4)

    qc, kc, vc, lac = blk(q), blk(k), blk(v), blk(log_a)
    b = jnp.cumsum(lac, axis=3)
    b_last = b[:, :, :, -1:, :]
    q_t = qc * jnp.exp(b)
    k_t = kc * jnp.exp(-b)
    k_end = kc * jnp.exp(b_last - b)
    decay_end = jnp.exp(b_last[:, :, :, 0, :])
    causal = jnp.tril(jnp.ones((C, C), dtype=bool))
    att = jnp.where(causal, jnp.einsum('nbhid,nbhjd->nbhij', q_t, k_t), 0.0)
    o_intra = jnp.einsum('nbhij,nbhje->nbhie', att, vc)

    def step(S, inp):
        q_i, k_i, v_i, d_i = inp
        o = jnp.einsum('bhcd,bhde->bhce', q_i, S)
        S = d_i[..., None] * S + jnp.einsum('bhcd,bhce->bhde', k_i, v_i)
        return S, o

    s_fin, o_inter = lax.scan(step, s0.astype(jnp.float32), (q_t, k_end, vc, decay_end))
    o = (o_intra + o_inter).transpose(1, 0, 3, 2, 4).reshape(B, T, H, DV)
    return o, s_fin


def mixer_gla(h, s0, w_in, w_a2, b_a, g_norm, w_out):
    B, T, _ = h.shape
    q, k, v, g, lr = jnp.split(h @ w_in, (GLA_QK, 2 * GLA_QK, 2 * GLA_QK + GLA_V, 2 * GLA_QK + 2 * GLA_V), axis=-1)
    q = q.reshape(B, T, GLA_HEADS, GLA_DK) * (GLA_DK ** -0.5)
    k = k.reshape(B, T, GLA_HEADS, GLA_DK)
    v = v.reshape(B, T, GLA_HEADS, GLA_DV)
    z = (lr @ w_a2 + b_a).astype(jnp.float32)
    log_a = (jax.nn.log_sigmoid(z) / GLA_TAU).reshape(B, T, GLA_HEADS, GLA_DK)
    if s0 is None:
        s0 = jnp.zeros((B, GLA_HEADS, GLA_DK, GLA_DV), jnp.float32)
    o, s_new = gla_recurrence(q, k, v, log_a, s0)
    o = rms_norm(o, g_norm).reshape(B, T, GLA_V) * jax.nn.silu(g)
    return o @ w_out, s_new


def swa_attend(q, k_prev, v_prev, k_cur, v_cur, prev_valid, sinks):
    Tq = q.shape[2]
    kk = jnp.concatenate([k_prev, k_cur], axis=2)
    vv = jnp.concatenate([v_prev, v_cur], axis=2)
    s = jnp.einsum('bnqkgd,bnskd->bnkgqs', q, kk).astype(jnp.float32)
    t = jnp.arange(Tq)[:, None]
    sidx = jnp.arange(WINDOW + Tq)[None, :]
    band = (sidx >= t) & (sidx <= t + WINDOW)
    mask = band[None] & (prev_valid[:, None, None] | (sidx >= WINDOW)[None])
    s = jnp.where(mask[None, :, None, None], s, -jnp.inf)
    sink = jnp.broadcast_to(
        sinks.astype(jnp.float32).reshape(SWA_KV_HEADS, SWA_GROUP)[None, None, :, :, None, None],
        s.shape[:-1] + (1,))
    p = jax.nn.softmax(jnp.concatenate([s, sink], axis=-1), axis=-1)[..., :-1]
    return jnp.einsum('bnkgqs,bnskd->bnqkgd', p.astype(vv.dtype), vv)


def mixer_swa(h, k_buf, v_buf, w_in, g_q, g_k, sinks, w_out):
    B, T, _ = h.shape
    q, k, v = jnp.split(h @ w_in, (SWA_Q, SWA_Q + SWA_KV), axis=-1)
    q = rms_norm(q.reshape(B, T, SWA_Q_HEADS, SWA_HEAD_DIM), g_q) * (SWA_HEAD_DIM ** -0.5)
    k = rms_norm(k.reshape(B, T, SWA_KV_HEADS, SWA_HEAD_DIM), g_k)
    v = v.reshape(B, T, SWA_KV_HEADS, SWA_HEAD_DIM)
    if k_buf is None:
        nb = T // WINDOW
        qb = q.reshape(B, nb, WINDOW, SWA_KV_HEADS, SWA_GROUP, SWA_HEAD_DIM)
        kb = k.reshape(B, nb, WINDOW, SWA_KV_HEADS, SWA_HEAD_DIM)
        vb = v.reshape(B, nb, WINDOW, SWA_KV_HEADS, SWA_HEAD_DIM)
        k_prev = jnp.concatenate([jnp.zeros_like(kb[:, :1]), kb[:, :-1]], axis=1)
        v_prev = jnp.concatenate([jnp.zeros_like(vb[:, :1]), vb[:, :-1]], axis=1)
        prev_valid = jnp.arange(nb) > 0
        k_new, v_new = k[:, -WINDOW:], v[:, -WINDOW:]
    else:
        qb = q.reshape(B, 1, T, SWA_KV_HEADS, SWA_GROUP, SWA_HEAD_DIM)
        kb, vb = k[:, None], v[:, None]
        k_prev = k_buf.astype(k.dtype)[:, None]
        v_prev = v_buf.astype(v.dtype)[:, None]
        prev_valid = jnp.ones((1,), dtype=bool)
        k_new = jnp.concatenate([k_buf.astype(k.dtype), k], axis=1)[:, -WINDOW:]
        v_new = jnp.concatenate([v_buf.astype(v.dtype), v], axis=1)[:, -WINDOW:]
    o = swa_attend(qb, k_prev, v_prev, kb, vb, prev_valid, sinks)
    return o.reshape(B, T, SWA_Q) @ w_out, k_new, v_new


def causal_conv(x, prev, w, bias):
    T = x.shape[1]
    xx = jnp.concatenate([prev.astype(x.dtype), x], axis=1)
    y = bias
    for tap in range(CONV_WIDTH):
        y = y + w[tap] * xx[:, tap:tap + T]
    return y, xx[:, -(CONV_WIDTH - 1):]


def rglru(xc, w_ga, b_ga, w_gx, b_gx, lam, h0):
    B, T, Dr = xc.shape
    xb = xc.reshape(B, T, LRU_BLOCKS, LRU_BLOCK)
    r = jax.nn.sigmoid((jnp.einsum('btnd,nde->btne', xb, w_ga).reshape(B, T, Dr) + b_ga).astype(jnp.float32))
    i = jax.nn.sigmoid((jnp.einsum('btnd,nde->btne', xb, w_gx).reshape(B, T, Dr) + b_gx).astype(jnp.float32))
    log_a = -LRU_C * r * jax.nn.softplus(-lam.astype(jnp.float32))
    a = jnp.exp(log_a)
    mult = jnp.sqrt(-jnp.expm1(2.0 * log_a))
    bterm = mult * i * xc.astype(jnp.float32)
    bterm = bterm.at[:, 0].add(a[:, 0] * h0.astype(jnp.float32))

    def combine(left, right):
        a1, b1 = left
        a2, b2 = right
        return a1 * a2, a2 * b1 + b2

    _, hs = lax.associative_scan(combine, (a, bterm), axis=1)
    return hs, hs[:, -1]


def mixer_lru(h, conv_buf, h0, w_in, conv_w, conv_b, w_ga, b_ga, w_gx, b_gx, lam, w_out):
    B, T, _ = h.shape
    y_br, x_br = jnp.split(h @ w_in, 2, axis=-1)
    if conv_buf is None:
        conv_buf = jnp.zeros((B, CONV_WIDTH - 1, D_RNN), x_br.dtype)
        h0 = jnp.zeros((B, D_RNN), jnp.float32)
    xc, conv_new = causal_conv(x_br, conv_buf, conv_w, conv_b)
    hs, h_last = rglru(xc, w_ga, b_ga, w_gx, b_gx, lam, h0)
    return (jax.nn.gelu(y_br) * hs) @ w_out, conv_new, h_last


def peer_block(h, w_q, sub_keys, u, v):
    n = h.shape[0]
    q = (h @ w_q).reshape(n, PEER_HEADS, 2, PEER_HALF)
    s = jnp.einsum('nhpd,hpkd->nhpk', q, sub_keys).astype(jnp.float32)
    s1, i1 = lax.top_k(s[:, :, 0], PEER_TOPK)
    s2, i2 = lax.top_k(s[:, :, 1], PEER_TOPK)
    cand = (s1[..., :, None] + s2[..., None, :]).reshape(n, PEER_HEADS, PEER_TOPK * PEER_TOPK)
    top, ci = lax.top_k(cand, PEER_TOPK)
    e1 = jnp.take_along_axis(i1, ci // PEER_TOPK, axis=-1)
    e2 = jnp.take_along_axis(i2, ci % PEER_TOPK, axis=-1)
    eidx = e1 * PEER_NKEYS + e2
    g = jax.nn.softmax(top, axis=-1)
    ue = u[eidx]
    act = jax.nn.gelu(jnp.einsum('nhkd,nd->nhk', ue, h).astype(jnp.float32))
    ve = v[eidx]
    return jnp.einsum('nhk,nhkd->nd', (g * act).astype(ve.dtype), ve)


def peer(h, w_q, sub_keys, u, v):
    B, T, D = h.shape
    N = B * T
    nb = -(-N // PEER_BLOCK)
    flat = jnp.pad(h.reshape(N, D), ((0, nb * PEER_BLOCK - N), (0, 0)))
    out = lax.map(lambda hb: peer_block(hb, w_q, sub_keys, u, v), flat.reshape(nb, PEER_BLOCK, D))
    return out.reshape(nb * PEER_BLOCK, D)[:N].reshape(B, T, D)


def trunk(x, c, states, p):
    silu_c = jax.nn.silu(c)
    new_gla, new_k, new_v, new_conv, new_h = [], [], [], [], []
    for i in range(DEPTH):
        kind, j = i % N_MIXERS, i // N_MIXERS
        mod = silu_c @ p['w_mod'][i] + p['b_mod'][i]
        sh_m, sc_m, gt_m, sh_f, sc_f, gt_f = jnp.split(mod[:, None, :], 6, axis=-1)
        h = rms_norm(x, p['g_ln_mix'][i]) * (1.0 + sc_m) + sh_m
        if kind == 0:
            s0 = None if states is None else states[0][j]
            mix, s_new = mixer_gla(h, s0, p['w_gla_in'][j], p['w_gla_a2'][j], p['b_gla_a'][j],
                                   p['g_gla_norm'][j], p['w_gla_out'][j])
            new_gla.append(s_new)
        elif kind == 1:
            kb = None if states is None else states[1][j]
            vbuf = None if states is None else states[2][j]
            mix, k_n, v_n = mixer_swa(h, kb, vbuf, p['w_swa_in'][j], p['g_swa_q'][j], p['g_swa_k'][j],
                                      p['swa_sinks'][j], p['w_swa_out'][j])
            new_k.append(k_n)
            new_v.append(v_n)
        else:
            cb = None if states is None else states[3][j]
            hb = None if states is None else states[4][j]
            mix, c_n, h_n = mixer_lru(h, cb, hb, p['w_lru_in'][j], p['lru_conv_w'][j], p['lru_conv_b'][j],
                                      p['w_lru_ga'][j], p['b_lru_ga'][j], p['w_lru_gx'][j], p['b_lru_gx'][j],
                                      p['lru_lam'][j], p['w_lru_out'][j])
            new_conv.append(c_n)
            new_h.append(h_n)
        x = x + gt_m * mix
        h = rms_norm(x, p['g_ln_ffn'][i]) * (1.0 + sc_f) + sh_f
        x = x + gt_f * peer(h, p['w_peer_q'][i], p['peer_sub_keys'][i], p['peer_u'][i], p['peer_v'][i])
    return x, (jnp.stack(new_gla), jnp.stack(new_k), jnp.stack(new_v), jnp.stack(new_conv), jnp.stack(new_h))


def setup_inputs(seed: int = 0) -> dict:
    key = jax.random.key(seed)
    ks = jax.random.split(key, 48)
    cnt = [0]

    def nxt():
        kk = ks[cnt[0]]
        cnt[0] += 1
        return kk

    def nrm(shape, scale):
        return jax.random.normal(nxt(), shape, jnp.float32) * scale

    D = D_MODEL
    x_prompt = nrm((BATCH, SEQ, D), 1.0)
    x_sample = nrm((DEC_BATCH, DEC_SEQ, D), 1.0)
    state_gla = nrm((N_GLA, DEC_BATCH, GLA_HEADS, GLA_DK, GLA_DV), 0.3)
    cache_swa_k = nrm((N_SWA, DEC_BATCH, WINDOW, SWA_KV_HEADS, SWA_HEAD_DIM), 1.0)
    cache_swa_v = nrm((N_SWA, DEC_BATCH, WINDOW, SWA_KV_HEADS, SWA_HEAD_DIM), 1.0)
    state_lru_conv = nrm((N_LRU, DEC_BATCH, CONV_WIDTH - 1, D_RNN), 1.0)
    state_lru_h = nrm((N_LRU, DEC_BATCH, D_RNN), 0.5)
    c_prompt = nrm((BATCH, D), 1.0)
    c_sample = nrm((DEC_BATCH, D), 1.0)

    g_ln_mix = 1.0 + nrm((DEPTH, D), 0.02)
    g_ln_ffn = 1.0 + nrm((DEPTH, D), 0.02)
    w_mod = nrm((DEPTH, D, 6 * D), 0.3 * D ** -0.5)
    b_mod = nrm((DEPTH, 6 * D), 0.02)

    w_gla_in = nrm((N_GLA, D, GLA_IN), D ** -0.5)
    w_gla_a2 = nrm((N_GLA, GLA_GATE_RANK, GLA_QK), GLA_GATE_RANK ** -0.5)
    b_gla_a = nrm((N_GLA, GLA_QK), 0.1)
    g_gla_norm = 1.0 + nrm((N_GLA, GLA_DV), 0.02)
    w_gla_out = nrm((N_GLA, GLA_V, D), GLA_V ** -0.5)

    w_swa_in = nrm((N_SWA, D, SWA_IN), D ** -0.5)
    g_swa_q = 1.0 + nrm((N_SWA, SWA_HEAD_DIM), 0.02)
    g_swa_k = 1.0 + nrm((N_SWA, SWA_HEAD_DIM), 0.02)
    swa_sinks = nrm((N_SWA, SWA_Q_HEADS), 0.5)
    w_swa_out = nrm((N_SWA, SWA_Q, D), SWA_Q ** -0.5)

    w_lru_in = nrm((N_LRU, D, 2 * D_RNN), D ** -0.5)
    lru_conv_w = nrm((N_LRU, CONV_WIDTH, D_RNN), CONV_WIDTH ** -0.5)
    lru_conv_b = nrm((N_LRU, D_RNN), 0.02)
    w_lru_ga = nrm((N_LRU, LRU_BLOCKS, LRU_BLOCK, LRU_BLOCK), LRU_BLOCK ** -0.5)
    b_lru_ga = nrm((N_LRU, D_RNN), 0.02)
    w_lru_gx = nrm((N_LRU, LRU_BLOCKS, LRU_BLOCK, LRU_BLOCK), LRU_BLOCK ** -0.5)
    b_lru_gx = nrm((N_LRU, D_RNN), 0.02)
    a_c = jax.random.uniform(nxt(), (N_LRU, D_RNN), jnp.float32, 0.9, 0.999)
    sig = a_c ** (1.0 / LRU_C)
    lru_lam = jnp.log(sig) - jnp.log1p(-sig)
    w_lru_out = nrm((N_LRU, D_RNN, D), D_RNN ** -0.5)

    w_peer_q = nrm((DEPTH, D, PEER_HEADS * PEER_KEY_DIM), D ** -0.5)
    peer_sub_keys = nrm((DEPTH, PEER_HEADS, 2, PEER_NKEYS, PEER_HALF), PEER_HALF ** -0.5)
    peer_u = nrm((DEPTH, PEER_EXPERTS, D), D ** -0.5)
    peer_v = nrm((DEPTH, PEER_EXPERTS, D), 1.0)

    return {'x_prompt': x_prompt, 'x_sample': x_sample, 'state_gla': state_gla,
            'cache_swa_k': cache_swa_k, 'cache_swa_v': cache_swa_v,
            'state_lru_conv': state_lru_conv, 'state_lru_h': state_lru_h,
            'c_prompt': c_prompt, 'c_sample': c_sample,
            'g_ln_mix': g_ln_mix, 'g_ln_ffn': g_ln_ffn, 'w_mod': w_mod, 'b_mod': b_mod,
            'w_gla_in': w_gla_in, 'w_gla_a2': w_gla_a2, 'b_gla_a': b_gla_a, 'g_gla_norm': g_gla_norm,
            'w_gla_out': w_gla_out,
            'w_swa_in': w_swa_in, 'g_swa_q': g_swa_q, 'g_swa_k': g_swa_k, 'swa_sinks': swa_sinks,
            'w_swa_out': w_swa_out,
            'w_lru_in': w_lru_in, 'lru_conv_w': lru_conv_w, 'lru_conv_b': lru_conv_b,
            'w_lru_ga': w_lru_ga, 'b_lru_ga': b_lru_ga, 'w_lru_gx': w_lru_gx, 'b_lru_gx': b_lru_gx,
            'lru_lam': lru_lam, 'w_lru_out': w_lru_out,
            'w_peer_q': w_peer_q, 'peer_sub_keys': peer_sub_keys, 'peer_u': peer_u, 'peer_v': peer_v}


def reference(x_prompt, x_sample, state_gla, cache_swa_k, cache_swa_v, state_lru_conv, state_lru_h,
              c_prompt, c_sample, g_ln_mix, g_ln_ffn, w_mod, b_mod,
              w_gla_in, w_gla_a2, b_gla_a, g_gla_norm, w_gla_out,
              w_swa_in, g_swa_q, g_swa_k, swa_sinks, w_swa_out,
              w_lru_in, lru_conv_w, lru_conv_b, w_lru_ga, b_lru_ga, w_lru_gx, b_lru_gx, lru_lam, w_lru_out,
              w_peer_q, peer_sub_keys, peer_u, peer_v):
    p = {'g_ln_mix': g_ln_mix, 'g_ln_ffn': g_ln_ffn, 'w_mod': w_mod, 'b_mod': b_mod,
         'w_gla_in': w_gla_in, 'w_gla_a2': w_gla_a2, 'b_gla_a': b_gla_a, 'g_gla_norm': g_gla_norm,
         'w_gla_out': w_gla_out,
         'w_swa_in': w_swa_in, 'g_swa_q': g_swa_q, 'g_swa_k': g_swa_k, 'swa_sinks': swa_sinks,
         'w_swa_out': w_swa_out,
         'w_lru_in': w_lru_in, 'lru_conv_w': lru_conv_w, 'lru_conv_b': lru_conv_b,
         'w_lru_ga': w_lru_ga, 'b_lru_ga': b_lru_ga, 'w_lru_gx': w_lru_gx, 'b_lru_gx': b_lru_gx,
         'lru_lam': lru_lam, 'w_lru_out': w_lru_out,
         'w_peer_q': w_peer_q, 'peer_sub_keys': peer_sub_keys, 'peer_u': peer_u, 'peer_v': peer_v}
    y_prompt, (gla_p, k_p, v_p, conv_p, h_p) = trunk(x_prompt, c_prompt, None, p)
    y_sample, (gla_s, k_s, v_s, conv_s, h_s) = trunk(
        x_sample, c_sample, (state_gla, cache_swa_k, cache_swa_v, state_lru_conv, state_lru_h), p)
    return (y_prompt, y_sample, gla_p, gla_s, k_p, k_s, v_p, v_s, conv_p, conv_s, h_p, h_s)
```

```python
import functools
import math

import jax
import jax.numpy as jnp
from jax import lax
from jax.experimental import pallas as pl
from jax.experimental.pallas import tpu as pltpu

F32 = jnp.float32
BF16 = jnp.bfloat16

D_MODEL = 1024
DEPTH = 4
N_MIXERS = 3
RMS_EPS = 1e-6

GLA_HEADS = 4
GLA_QK = D_MODEL // 2
GLA_V = D_MODEL
GLA_DK = GLA_QK // GLA_HEADS
GLA_DV = GLA_V // GLA_HEADS
GLA_GATE_RANK = 16
GLA_TAU = 16.0
GLA_SUB = 16
GLA_CHUNK = 128
GLA_IN_PAD = 2 * GLA_QK + 2 * GLA_V + 128

SWA_HEAD_DIM = 64
SWA_Q_HEADS = D_MODEL // SWA_HEAD_DIM
SWA_KV_HEADS = 4
SWA_GROUP = SWA_Q_HEADS // SWA_KV_HEADS
SWA_Q = SWA_Q_HEADS * SWA_HEAD_DIM
SWA_KV = SWA_KV_HEADS * SWA_HEAD_DIM
WINDOW = 128

D_RNN = D_MODEL
LRU_BLOCKS = 4
LRU_BLOCK = D_RNN // LRU_BLOCKS
CONV_WIDTH = 4
LRU_C = 8.0

PEER_HEADS = 8
PEER_NKEYS = 128
PEER_EXPERTS = PEER_NKEYS * PEER_NKEYS
PEER_KEY_DIM = 256
PEER_HALF = PEER_KEY_DIM // 2
PEER_TOPK = 16
PEER_PAIRS = PEER_HEADS * PEER_TOPK

V7X_LANES = 128
V7X_SUBLANES = 8
V7X_VMEM_BYTES = 64 * 1024 * 1024

TOKEN_TILE = 256
NEG_BIG = -1e30


def _vmem_limit(nbytes):
    return int(min(max(nbytes * 3 // 2, 16 * 1024 * 1024), V7X_VMEM_BYTES - 8 * 1024 * 1024))


def _params(semantics, nbytes):
    return pltpu.CompilerParams(dimension_semantics=semantics, vmem_limit_bytes=_vmem_limit(nbytes))


def _rms(x, g):
    return x * lax.rsqrt(jnp.mean(x * x, axis=-1, keepdims=True) + RMS_EPS) * g


def _gelu_tanh(x):
    return 0.5 * x * (1.0 + jnp.tanh(math.sqrt(2.0 / math.pi) * (x + 0.044715 * (x * x * x))))


def _sigmoid(x):
    return 1.0 / (1.0 + jnp.exp(-x))


def _softplus(x):
    return jnp.maximum(x, 0.0) + jnp.log1p(jnp.exp(-jnp.abs(x)))


def _row_operand(vec, seq_len, tile):
    b, d = vec.shape
    if seq_len % tile == 0:
        per_seq = seq_len // tile
        return vec.reshape(b, 1, d), pl.BlockSpec((1, 1, d), lambda i: (i // per_seq, 0, 0))
    assert tile % seq_len == 0
    rep = jnp.repeat(vec, seq_len, axis=0).reshape(b * seq_len // tile, tile, d)
    return rep, pl.BlockSpec((1, tile, d), lambda i: (i, 0, 0))


def _mod_kernel(c_ref, w_ref, b_ref, o_ref):
    c = c_ref[...]
    sc = c * _sigmoid(c)
    o_ref[0] = jnp.dot(sc, w_ref[0], preferred_element_type=F32) + b_ref[0]


def _modulation(c, w_mod, b_mod):
    bp, d = c.shape
    tn = 1024
    nt = 6 * d // tn
    return pl.pallas_call(
        _mod_kernel,
        grid=(DEPTH, nt),
        in_specs=[pl.BlockSpec((bp, d), lambda l, j: (0, 0)),
                  pl.BlockSpec((1, d, tn), lambda l, j: (l, 0, j)),
                  pl.BlockSpec((1, 1, tn), lambda l, j: (l, 0, j))],
        out_specs=pl.BlockSpec((1, bp, tn), lambda l, j: (l, 0, j)),
        out_shape=jax.ShapeDtypeStruct((DEPTH, bp, 6 * d), F32),
        compiler_params=_params(("parallel", "parallel"), 2 * (d * tn * 4 + 2 * bp * tn * 4)),
        name="modulation",
    )(c, w_mod, b_mod.reshape(DEPTH, 1, 6 * d))


def _norm_proj_kernel(x_ref, g_ref, sc_ref, sh_ref, w_ref, o_ref, *h_ref):
    h = _rms(x_ref[...], g_ref[...]) * (1.0 + sc_ref[0]) + sh_ref[0]
    hb = h.astype(BF16)
    o_ref[...] = jnp.dot(hb, w_ref[...], preferred_element_type=F32)
    if h_ref:
        h_ref[0][...] = hb


def _norm_proj(x, g, scale, shift, w_bf16, seq_len, with_h=False):
    n, d = x.shape
    nout = w_bf16.shape[1]
    tm = TOKEN_TILE
    sc_arr, sc_spec = _row_operand(scale, seq_len, tm)
    sh_arr, sh_spec = _row_operand(shift, seq_len, tm)
    out_shape = [jax.ShapeDtypeStruct((n, nout), F32)]
    out_specs = [pl.BlockSpec((tm, nout), lambda i: (i, 0))]
    if with_h:
        out_shape.append(jax.ShapeDtypeStruct((n, d), BF16))
        out_specs.append(pl.BlockSpec((tm, d), lambda i: (i, 0)))
    nbytes = 2 * (tm * d * 4 + d * nout * 2 + tm * nout * 4 + 3 * tm * d * 4)
    res = pl.pallas_call(
        _norm_proj_kernel,
        grid=(n // tm,),
        in_specs=[pl.BlockSpec((tm, d), lambda i: (i, 0)),
                  pl.BlockSpec((1, d), lambda i: (0, 0)),
                  sc_spec, sh_spec,
                  pl.BlockSpec((d, nout), lambda i: (0, 0))],
        out_specs=out_specs,
        out_shape=out_shape,
        compiler_params=_params(("parallel",), nbytes),
        name="norm_proj",
    )(x, g.reshape(1, d), sc_arr, sh_arr, w_bf16)
    return res if with_h else res[0]


def _proj_residual_kernel(a_ref, w_ref, x_ref, gt_ref, o_ref):
    y = jnp.dot(a_ref[...].astype(BF16), w_ref[...], preferred_element_type=F32)
    o_ref[...] = x_ref[...] + gt_ref[0] * y


def _proj_residual(a, w_bf16, x, gate, seq_len):
    n, k = a.shape
    d = x.shape[1]
    tm = TOKEN_TILE
    gt_arr, gt_spec = _row_operand(gate, seq_len, tm)
    nbytes = 2 * (tm * k * 4 + k * d * 2 + 3 * tm * d * 4)
    return pl.pallas_call(
        _proj_residual_kernel,
        grid=(n // tm,),
        in_specs=[pl.BlockSpec((tm, k), lambda i: (i, 0)),
                  pl.BlockSpec((k, d), lambda i: (0, 0)),
                  pl.BlockSpec((tm, d), lambda i: (i, 0)),
                  gt_spec],
        out_specs=pl.BlockSpec((tm, d), lambda i: (i, 0)),
        out_shape=jax.ShapeDtypeStruct((n, d), F32),
        compiler_params=_params(("parallel",), nbytes),
        name="proj_residual",
    )(a, w_bf16, x, gt_arr)


def _log_decay(lr, wa2, ba):
    z = jnp.dot(lr, wa2, preferred_element_type=F32) + ba
    return (jnp.minimum(z, 0.0) - jnp.log1p(jnp.exp(-jnp.abs(z)))) * (1.0 / GLA_TAU)


def _col_bcast(row):
    return jnp.transpose(jnp.broadcast_to(row, (V7X_LANES, V7X_LANES)))


def _head_out(o, gate, gn):
    return _rms(o, gn) * (gate * _sigmoid(gate))


def _gla_prompt_kernel(p_ref, wa2_ref, ba_ref, gn_ref, o_ref, s_ref):
    c = pl.program_id(1)
    C = GLA_CHUNK
    nsub = C // GLA_SUB

    @pl.when(c == 0)
    def _():
        s_ref[...] = jnp.zeros_like(s_ref)

    row = lax.broadcasted_iota(jnp.int32, (C, C), 0)
    col = lax.broadcasted_iota(jnp.int32, (C, C), 1)
    tri = (col <= row).astype(F32)
    later = ((col > row) & (col // GLA_SUB == row // GLA_SUB)).astype(F32)
    sums = jnp.concatenate([tri, later], axis=0)
    causal = col <= row
    rsub = lax.broadcasted_iota(jnp.int32, (C, GLA_DK), 0) // GLA_SUB

    lr = p_ref[0, :, 2 * GLA_QK + 2 * GLA_V:]
    for h in range(GLA_HEADS):
        q = p_ref[0, :, h * GLA_DK:(h + 1) * GLA_DK] * (GLA_DK ** -0.5)
        k = p_ref[0, :, GLA_QK + h * GLA_DK:GLA_QK + (h + 1) * GLA_DK]
        v = p_ref[0, :, 2 * GLA_QK + h * GLA_DV:2 * GLA_QK + (h + 1) * GLA_DV]
        gate = p_ref[0, :, 2 * GLA_QK + GLA_V + h * GLA_DV:2 * GLA_QK + GLA_V + (h + 1) * GLA_DV]
        la = _log_decay(lr, wa2_ref[:, h * GLA_DK:(h + 1) * GLA_DK], ba_ref[:, h * GLA_DK:(h + 1) * GLA_DK])
        cs = jnp.dot(sums, la, preferred_element_type=F32, precision=lax.Precision.HIGHEST)
        b = cs[:C]
        to_sub_end = cs[C:]
        b_last = b[C - 1:C]
        k_sub = k * jnp.exp(to_sub_end)
        q_parts, k_parts = [], []
        for m in range(nsub):
            ref_row = b[m * GLA_SUB + GLA_SUB - 1:m * GLA_SUB + GLA_SUB]
            e = jnp.where(rsub >= m, b - ref_row, NEG_BIG)
            q_parts.append((q * jnp.exp(e)).astype(BF16))
            k_parts.append(jnp.where(rsub == m, k_sub, 0.0).astype(BF16))
        qcat = jnp.concatenate(q_parts, axis=1)
        kcat = jnp.concatenate(k_parts, axis=1)
        att = lax.dot_general(qcat, kcat, (((1,), (1,)), ((), ())), preferred_element_type=F32)
        att = jnp.where(causal, att, 0.0)
        s_old = s_ref[0, h]
        o = jnp.dot(att.astype(BF16), v.astype(BF16), preferred_element_type=F32)
        o = o + jnp.dot((q * jnp.exp(b)).astype(BF16), s_old.astype(BF16), preferred_element_type=F32)
        k_end = (k * jnp.exp(b_last - b)).astype(BF16)
        upd = lax.dot_general(k_end, v.astype(BF16), (((0,), (0,)), ((), ())), preferred_element_type=F32)
        decay = _col_bcast(jnp.exp(b_last))
        s_ref[0, h] = jnp.concatenate([decay] * (GLA_DV // V7X_LANES), axis=1) * s_old + upd
        o_ref[0, :, h * GLA_DV:(h + 1) * GLA_DV] = _head_out(o, gate, gn_ref[...])


def _gla_prompt(proj, wa2_pad, ba, gn, batch, seq_len):
    C = GLA_CHUNK
    p3 = proj.reshape(batch, seq_len, GLA_IN_PAD)
    nbytes = 2 * (C * GLA_IN_PAD * 4 + C * GLA_V * 4 + GLA_HEADS * GLA_DK * GLA_DV * 4) + 64 * C * C * 4
    o, s = pl.pallas_call(
        _gla_prompt_kernel,
        grid=(batch, seq_len // C),
        in_specs=[pl.BlockSpec((1, C, GLA_IN_PAD), lambda b, c: (b, c, 0)),
                  pl.BlockSpec((V7X_LANES, GLA_QK), lambda b, c: (0, 0)),
                  pl.BlockSpec((1, GLA_QK), lambda b, c: (0, 0)),
                  pl.BlockSpec((1, GLA_DV), lambda b, c: (0, 0))],
        out_specs=[pl.BlockSpec((1, C, GLA_V), lambda b, c: (b, c, 0)),
                   pl.BlockSpec((1, GLA_HEADS, GLA_DK, GLA_DV), lambda b, c: (b, 0, 0, 0))],
        out_shape=[jax.ShapeDtypeStruct((batch, seq_len, GLA_V), F32),
                   jax.ShapeDtypeStruct((batch, GLA_HEADS, GLA_DK, GLA_DV), F32)],
        compiler_params=_params(("parallel", "arbitrary"), nbytes),
        name="gla_prompt",
    )(p3, wa2_pad, ba.reshape(1, GLA_QK), gn.reshape(1, GLA_DV))
    return o.reshape(batch * seq_len, GLA_V), s


GLA_SAMPLE_BATCH = 8


def _gla_sample_kernel(p_ref, s0_ref, wa2_ref, ba_ref, gn_ref, o_ref, s_ref):
    T = p_ref.shape[1]
    row = lax.broadcasted_iota(jnp.int32, (T, T), 0)
    col = lax.broadcasted_iota(jnp.int32, (T, T), 1)
    causal = col <= row
    tri = causal.astype(F32)
    for j in range(p_ref.shape[0]):
        lr = p_ref[j, :, 2 * GLA_QK + 2 * GLA_V:]
        for h in range(GLA_HEADS):
            q = p_ref[j, :, h * GLA_DK:(h + 1) * GLA_DK] * (GLA_DK ** -0.5)
            k = p_ref[j, :, GLA_QK + h * GLA_DK:GLA_QK + (h + 1) * GLA_DK]
            v = p_ref[j, :, 2 * GLA_QK + h * GLA_DV:2 * GLA_QK + (h + 1) * GLA_DV]
            gate = p_ref[j, :, 2 * GLA_QK + GLA_V + h * GLA_DV:2 * GLA_QK + GLA_V + (h + 1) * GLA_DV]
            la = _log_decay(lr, wa2_ref[:, h * GLA_DK:(h + 1) * GLA_DK], ba_ref[:, h * GLA_DK:(h + 1) * GLA_DK])
            b = jnp.dot(tri, la, preferred_element_type=F32, precision=lax.Precision.HIGHEST)
            b_last = b[T - 1:T]
            k_end = (k * jnp.exp(b_last - b)).astype(BF16)
            q_rel = (q * jnp.exp(b - b_last)).astype(BF16)
            att = lax.dot_general(q_rel, k_end, (((1,), (1,)), ((), ())), preferred_element_type=F32)
            att = jnp.where(causal, att, 0.0)
            s_old = s0_ref[j, h]
            vb = v.astype(BF16)
            o = jnp.dot(att.astype(BF16), vb, preferred_element_type=F32)
            o = o + jnp.dot((q * jnp.exp(b)).astype(BF16), s_old.astype(BF16), preferred_element_type=F32)
            upd = lax.dot_general(k_end, vb, (((0,), (0,)), ((), ())), preferred_element_type=F32)
            decay = _col_bcast(jnp.exp(b_last))
            s_ref[j, h] = jnp.concatenate([decay] * (GLA_DV // V7X_LANES), axis=1) * s_old + upd
            o_ref[j, :, h * GLA_DV:(h + 1) * GLA_DV] = _head_out(o, gate, gn_ref[...])


def _gla_sample(proj, s0, wa2_pad, ba, gn, batch, seq_len):
    assert seq_len <= GLA_SUB
    nb = GLA_SAMPLE_BATCH
    p3 = proj.reshape(batch, seq_len, GLA_IN_PAD)
    state_block = nb * GLA_HEADS * GLA_DK * GLA_DV * 4
    nbytes = 2 * (nb * seq_len * (GLA_IN_PAD + GLA_V) * 4 + 2 * state_block)
    o, s = pl.pallas_call(
        _gla_sample_kernel,
        grid=(batch // nb,),
        in_specs=[pl.BlockSpec((nb, seq_len, GLA_IN_PAD), lambda b: (b, 0, 0)),
                  pl.BlockSpec((nb, GLA_HEADS, GLA_DK, GLA_DV), lambda b: (b, 0, 0, 0)),
                  pl.BlockSpec((V7X_LANES, GLA_QK), lambda b: (0, 0)),
                  pl.BlockSpec((1, GLA_QK), lambda b: (0, 0)),
                  pl.BlockSpec((1, GLA_DV), lambda b: (0, 0))],
        out_specs=[pl.BlockSpec((nb, seq_len, GLA_V), lambda b: (b, 0, 0)),
                   pl.BlockSpec((nb, GLA_HEADS, GLA_DK, GLA_DV), lambda b: (b, 0, 0, 0))],
        out_shape=[jax.ShapeDtypeStruct((batch, seq_len, GLA_V), F32),
                   jax.ShapeDtypeStruct((batch, GLA_HEADS, GLA_DK, GLA_DV), F32)],
        compiler_params=_params(("parallel",), nbytes),
        name="gla_sample",
    )(p3, s0, wa2_pad, ba.reshape(1, GLA_QK), gn.reshape(1, GLA_DV))
    return o.reshape(batch * seq_len, GLA_V), s


def _head_group_norm(x, gain, gsum):
    sq = x * x
    hi = sq.astype(BF16)
    lo = (sq - hi.astype(F32)).astype(BF16)
    ms = (jnp.dot(hi, gsum, preferred_element_type=F32) + jnp.dot(lo, gsum, preferred_element_type=F32))
    return x * lax.rsqrt(ms * (1.0 / SWA_HEAD_DIM) + RMS_EPS) * gain


def _swa_attend(q_groups, k_all, v_all, mask, sink_ref):
    tq = q_groups[0].shape[0]
    lane_head = lax.broadcasted_iota(jnp.int32, (tq, SWA_KV), 1) // SWA_HEAD_DIM
    mask_rows = jnp.concatenate([mask] * SWA_GROUP, axis=0)
    out = [jnp.zeros((tq, SWA_KV), F32) for _ in range(SWA_GROUP)]
    for kv in range(SWA_KV_HEADS):
        in_head = lane_head == kv
        qs = jnp.concatenate([jnp.where(in_head, qg, 0.0) for qg in q_groups], axis=0).astype(BF16)
        s = lax.dot_general(qs, k_all, (((1,), (1,)), ((), ())), preferred_element_type=F32)
        s = jnp.where(mask_rows, s, -jnp.inf)
        sink = jnp.concatenate(
            [jnp.full((tq, 1), sink_ref[kv * SWA_GROUP + g], F32) for g in range(SWA_GROUP)], axis=0)
        m = jnp.maximum(jnp.max(s, axis=1, keepdims=True), sink)
        p = jnp.exp(s - m)
        denom = jnp.sum(p, axis=1, keepdims=True) + jnp.exp(sink - m)
        pv = jnp.dot(p.astype(BF16), v_all, preferred_element_type=F32) / denom
        for g in range(SWA_GROUP):
            out[g] = jnp.where(in_head, pv[g * tq:(g + 1) * tq], out[g])
    return out


def _swa_prompt_kernel(sink_ref, cur_ref, prev_ref, gq_ref, gk_ref, gsum_ref, o_ref, k_ref, v_ref):
    n = pl.program_id(1)
    gsum = gsum_ref[...]
    k_cur = _head_group_norm(cur_ref[0, :, SWA_Q:SWA_Q + SWA_KV], gk_ref[...], gsum)
    k_prev = _head_group_norm(prev_ref[0, :, SWA_Q:SWA_Q + SWA_KV], gk_ref[...], gsum)
    v_cur = cur_ref[0, :, SWA_Q + SWA_KV:]
    v_prev = prev_ref[0, :, SWA_Q + SWA_KV:]
    k_all = jnp.concatenate([k_prev, k_cur], axis=0).astype(BF16)
    v_all = jnp.concatenate([v_prev, v_cur], axis=0).astype(BF16)
    t = lax.broadcasted_iota(jnp.int32, (WINDOW, 2 * WINDOW), 0)
    s = lax.broadcasted_iota(jnp.int32, (WINDOW, 2 * WINDOW), 1)
    mask = (s >= t) & (s <= t + WINDOW) & ((s >= WINDOW) | (n > 0))
    q_groups = [_head_group_norm(cur_ref[0, :, g * SWA_KV:(g + 1) * SWA_KV], gq_ref[...], gsum)
                * (SWA_HEAD_DIM ** -0.5) for g in range(SWA_GROUP)]
    out = _swa_attend(q_groups, k_all, v_all, mask, sink_ref)
    o_ref[0] = jnp.concatenate(out, axis=1)
    k_ref[0] = k_cur
    v_ref[0] = v_cur


def _swa_gsum():
    head = jnp.arange(SWA_KV) // SWA_HEAD_DIM
    return (head[:, None] == head[None, :]).astype(BF16)


def _swa_prompt(proj, gq, gk, sinks, batch, seq_len):
    W = WINDOW
    width = SWA_Q + 2 * SWA_KV
    p3 = proj.reshape(batch, seq_len, width)
    nbytes = 2 * (2 * W * width * 4 + W * SWA_Q * 4 + 2 * W * SWA_KV * 4) + 48 * W * 2 * W * 4
    o, k, v = pl.pallas_call(
        _swa_prompt_kernel,
        grid=(batch, seq_len // W),
        in_specs=[pl.BlockSpec(memory_space=pltpu.SMEM),
                  pl.BlockSpec((1, W, width), lambda b, n: (b, n, 0)),
                  pl.BlockSpec((1, W, width), lambda b, n: (b, jnp.maximum(n - 1, 0), 0)),
                  pl.BlockSpec((1, SWA_KV), lambda b, n: (0, 0)),
                  pl.BlockSpec((1, SWA_KV), lambda b, n: (0, 0)),
                  pl.BlockSpec((SWA_KV, SWA_KV), lambda b, n: (0, 0))],
        out_specs=[pl.BlockSpec((1, W, SWA_Q), lambda b, n: (b, n, 0)),
                   pl.BlockSpec((1, W, SWA_KV), lambda b, n: (b, 0, 0)),
                   pl.BlockSpec((1, W, SWA_KV), lambda b, n: (b, 0, 0))],
        out_shape=[jax.ShapeDtypeStruct((batch, seq_len, SWA_Q), F32),
                   jax.ShapeDtypeStruct((batch, W, SWA_KV), F32),
                   jax.ShapeDtypeStruct((batch, W, SWA_KV), F32)],
        compiler_params=_params(("parallel", "arbitrary"), nbytes),
        name="swa_prompt",
    )(sinks, p3, p3, jnp.tile(gq, SWA_KV_HEADS).reshape(1, SWA_KV), jnp.tile(gk, SWA_KV_HEADS).reshape(1, SWA_KV),
      _swa_gsum())
    return o.reshape(batch * seq_len, SWA_Q), k, v


SWA_SAMPLE_BATCH = 8


def _swa_sample_kernel(sink_ref, p_ref, kc_ref, vc_ref, gq_ref, gk_ref, gsum_ref, o_ref, k_ref, v_ref):
    T = p_ref.shape[1]
    gsum = gsum_ref[...]
    t = lax.broadcasted_iota(jnp.int32, (T, WINDOW + T), 0)
    s = lax.broadcasted_iota(jnp.int32, (T, WINDOW + T), 1)
    mask = (s >= t) & (s <= t + WINDOW)
    for j in range(p_ref.shape[0]):
        k_new = _head_group_norm(p_ref[j, :, SWA_Q:SWA_Q + SWA_KV], gk_ref[...], gsum)
        v_new = p_ref[j, :, SWA_Q + SWA_KV:]
        k_all = jnp.concatenate([kc_ref[j], k_new], axis=0)
        v_all = jnp.concatenate([vc_ref[j], v_new], axis=0)
        q_groups = [_head_group_norm(p_ref[j, :, g * SWA_KV:(g + 1) * SWA_KV], gq_ref[...], gsum)
                    * (SWA_HEAD_DIM ** -0.5) for g in range(SWA_GROUP)]
        out = _swa_attend(q_groups, k_all.astype(BF16), v_all.astype(BF16), mask, sink_ref)
        o_ref[j] = jnp.concatenate(out, axis=1)
        k_ref[j] = k_all[T:]
        v_ref[j] = v_all[T:]


def _swa_sample(proj, k_cache, v_cache, gq, gk, sinks, batch, seq_len):
    nb = SWA_SAMPLE_BATCH
    W = WINDOW
    width = SWA_Q + 2 * SWA_KV
    p3 = proj.reshape(batch, seq_len, width)
    nbytes = 2 * (nb * seq_len * (width + SWA_Q) * 4 + 4 * nb * W * SWA_KV * 4)
    o, k, v = pl.pallas_call(
        _swa_sample_kernel,
        grid=(batch // nb,),
        in_specs=[pl.BlockSpec(memory_space=pltpu.SMEM),
                  pl.BlockSpec((nb, seq_len, width), lambda b: (b, 0, 0)),
                  pl.BlockSpec((nb, W, SWA_KV), lambda b: (b, 0, 0)),
                  pl.BlockSpec((nb, W, SWA_KV), lambda b: (b, 0, 0)),
                  pl.BlockSpec((1, SWA_KV), lambda b: (0, 0)),
                  pl.BlockSpec((1, SWA_KV), lambda b: (0, 0)),
                  pl.BlockSpec((SWA_KV, SWA_KV), lambda b: (0, 0))],
        out_specs=[pl.BlockSpec((nb, seq_len, SWA_Q), lambda b: (b, 0, 0)),
                   pl.BlockSpec((nb, W, SWA_KV), lambda b: (b, 0, 0)),
                   pl.BlockSpec((nb, W, SWA_KV), lambda b: (b, 0, 0))],
        out_shape=[jax.ShapeDtypeStruct((batch, seq_len, SWA_Q), F32),
                   jax.ShapeDtypeStruct((batch, W, SWA_KV), F32),
                   jax.ShapeDtypeStruct((batch, W, SWA_KV), F32)],
        compiler_params=_params(("parallel",), nbytes),
        name="swa_sample",
    )(sinks, p3, k_cache, v_cache, jnp.tile(gq, SWA_KV_HEADS).reshape(1, SWA_KV),
      jnp.tile(gk, SWA_KV_HEADS).reshape(1, SWA_KV), _swa_gsum())
    return o.reshape(batch * seq_len, SWA_Q), k, v


def _lru_conv(x, shifted, cw_ref, cb_ref):
    y = cb_ref[...] + cw_ref[CONV_WIDTH - 1:CONV_WIDTH] * x
    for s in range(1, CONV_WIDTH):
        y = y + cw_ref[CONV_WIDTH - 1 - s:CONV_WIDTH - s] * shifted[s - 1]
    return y


def _block_diag_dot(x, w_ref):
    xb = x.astype(BF16)
    return jnp.concatenate(
        [jnp.dot(xb[:, n * LRU_BLOCK:(n + 1) * LRU_BLOCK], w_ref[n], preferred_element_type=F32)
         for n in range(LRU_BLOCKS)], axis=1)


def _lru_terms(xc, wga_ref, bga_ref, wgx_ref, bgx_ref, lam_ref):
    r = _sigmoid(_block_diag_dot(xc, wga_ref) + bga_ref[...])
    i = _sigmoid(_block_diag_dot(xc, wgx_ref) + bgx_ref[...])
    log_a = (-LRU_C) * r * _softplus(-lam_ref[...])
    a = jnp.exp(log_a)
    y2 = 2.0 * log_a
    u = a * a
    em1 = jnp.where(u == 1.0, y2, jnp.where(u == 0.0, -1.0, (u - 1.0) * y2 / jnp.log(u)))
    mult = jnp.sqrt(-em1)
    return a, mult * i * xc


def _scan_rows(a, b, group):
    rows = a.shape[0]
    pos = lax.broadcasted_iota(jnp.int32, a.shape, 0) % group
    d = 1
    while d < group:
        keep = pos >= d
        b = jnp.where(keep, a * pltpu.roll(b, d, 0) + b, b)
        a = jnp.where(keep, a * pltpu.roll(a, d, 0), a)
        d *= 2
    return a, b


def _lru_prompt_kernel(p_ref, cw_ref, cb_ref, wga_ref, bga_ref, wgx_ref, bgx_ref, lam_ref,
                       o_ref, h_ref, tail_ref, hc_ref):
    n = pl.program_id(1)
    rows = p_ref.shape[1]

    @pl.when(n == 0)
    def _():
        tail_ref[...] = jnp.zeros_like(tail_ref)
        hc_ref[...] = jnp.zeros_like(hc_ref)

    y = p_ref[0, :, :D_RNN]
    x = p_ref[0, :, D_RNN:]
    tail = tail_ref[...]
    r8 = lax.broadcasted_iota(jnp.int32, (V7X_SUBLANES, D_RNN), 0)
    shifted = []
    for s in range(1, CONV_WIDTH):
        xs = pltpu.roll(x, s, 0)
        head = jnp.where(r8 < s, pltpu.roll(tail, s, 0), xs[:V7X_SUBLANES])
        shifted.append(jnp.concatenate([head, xs[V7X_SUBLANES:]], axis=0))
    xc = _lru_conv(x, shifted, cw_ref, cb_ref)
    a, bterm = _lru_terms(xc, wga_ref, bga_ref, wgx_ref, bgx_ref, lam_ref)
    acum, hzero = _scan_rows(a, bterm, rows)
    hs = acum * hc_ref[0:1] + hzero
    o_ref[0] = _gelu_tanh(y) * hs
    last = hs[rows - 1:rows]
    h_ref[0] = last
    hc_ref[...] = jnp.broadcast_to(last, hc_ref.shape)
    tail_ref[...] = x[rows - V7X_SUBLANES:]


LRU_TILE = 256


def _lru_weight_specs(imap):
    return [pl.BlockSpec((CONV_WIDTH, D_RNN), imap(2)),
            pl.BlockSpec((1, D_RNN), imap(2)),
            pl.BlockSpec((LRU_BLOCKS, LRU_BLOCK, LRU_BLOCK), imap(3)),
            pl.BlockSpec((1, D_RNN), imap(2)),
            pl.BlockSpec((LRU_BLOCKS, LRU_BLOCK, LRU_BLOCK), imap(3)),
            pl.BlockSpec((1, D_RNN), imap(2)),
            pl.BlockSpec((1, D_RNN), imap(2))]


def _lru_prompt(proj, cw, cb, wga, bga, wgx, bgx, lam, batch, seq_len):
    R = LRU_TILE
    p3 = proj.reshape(batch, seq_len, 2 * D_RNN)
    nbytes = 2 * (R * 3 * D_RNN * 4 + 2 * LRU_BLOCKS * LRU_BLOCK * LRU_BLOCK * 2) + 24 * R * D_RNN * 4
    o, h = pl.pallas_call(
        _lru_prompt_kernel,
        grid=(batch, seq_len // R),
        in_specs=[pl.BlockSpec((1, R, 2 * D_RNN), lambda b, n: (b, n, 0))]
        + _lru_weight_specs(lambda nd: (lambda b, n: (0,) * nd)),
        out_specs=[pl.BlockSpec((1, R, D_RNN), lambda b, n: (b, n, 0)),
                   pl.BlockSpec((1, 1, D_RNN), lambda b, n: (b, 0, 0))],
        out_shape=[jax.ShapeDtypeStruct((batch, seq_len, D_RNN), F32),
                   jax.ShapeDtypeStruct((batch, 1, D_RNN), F32)],
        scratch_shapes=[pltpu.VMEM((V7X_SUBLANES, D_RNN), F32), pltpu.VMEM((V7X_SUBLANES, D_RNN), F32)],
        compiler_params=_params(("parallel", "arbitrary"), nbytes),
        name="lru_prompt",
    )(p3, cw, cb.reshape(1, D_RNN), wga.astype(BF16), bga.reshape(1, D_RNN), wgx.astype(BF16),
      bgx.reshape(1, D_RNN), lam.reshape(1, D_RNN))
    return o.reshape(batch * seq_len, D_RNN), h.reshape(batch, D_RNN)


def _lru_sample_kernel(p_ref, prev_ref, h0_ref, cw_ref, cb_ref, wga_ref, bga_ref, wgx_ref, bgx_ref, lam_ref,
                       o_ref, hs_ref, *, seq_len):
    rows = p_ref.shape[0]
    y = p_ref[:, :D_RNN]
    x = p_ref[:, D_RNN:]
    prev = prev_ref[...]
    pos = lax.broadcasted_iota(jnp.int32, (rows, D_RNN), 0) % seq_len
    shifted = [jnp.where(pos < s, pltpu.roll(prev, rows - seq_len + s, 0), pltpu.roll(x, s, 0))
               for s in range(1, CONV_WIDTH)]
    xc = _lru_conv(x, shifted, cw_ref, cb_ref)
    a, bterm = _lru_terms(xc, wga_ref, bga_ref, wgx_ref, bgx_ref, lam_ref)
    acum, hzero = _scan_rows(a, bterm, seq_len)
    hs = acum * h0_ref[...] + hzero
    o_ref[...] = _gelu_tanh(y) * hs
    hs_ref[...] = hs


def _lru_sample(proj, conv_state, h0, cw, cb, wga, bga, wgx, bgx, lam, batch, seq_len):
    assert seq_len == V7X_SUBLANES
    n = batch * seq_len
    R = LRU_TILE
    prev = jnp.pad(conv_state, ((0, 0), (seq_len - (CONV_WIDTH - 1), 0), (0, 0))).reshape(n, D_RNN)
    h0_rows = jnp.repeat(h0, seq_len, axis=0)
    nbytes = 2 * (R * 6 * D_RNN * 4 + 2 * LRU_BLOCKS * LRU_BLOCK * LRU_BLOCK * 2) + 24 * R * D_RNN * 4
    return pl.pallas_call(
        functools.partial(_lru_sample_kernel, seq_len=seq_len),
        grid=(n // R,),
        in_specs=[pl.BlockSpec((R, 2 * D_RNN), lambda i: (i, 0)),
                  pl.BlockSpec((R, D_RNN), lambda i: (i, 0)),
                  pl.BlockSpec((R, D_RNN), lambda i: (i, 0))]
        + _lru_weight_specs(lambda nd: (lambda i: (0,) * nd)),
        out_specs=[pl.BlockSpec((R, D_RNN), lambda i: (i, 0)),
                   pl.BlockSpec((R, D_RNN), lambda i: (i, 0))],
        out_shape=[jax.ShapeDtypeStruct((n, D_RNN), F32), jax.ShapeDtypeStruct((n, D_RNN), F32)],
        compiler_params=_params(("parallel",), nbytes),
        name="lru_sample",
    )(proj, prev, h0_rows, cw, cb.reshape(1, D_RNN), wga.astype(BF16), bga.reshape(1, D_RNN), wgx.astype(BF16),
      bgx.reshape(1, D_RNN), lam.reshape(1, D_RNN))


def _topk_rows(s, k):
    n = s.shape[0]
    rid = lax.broadcasted_iota(jnp.int32, s.shape, 0)
    vals, ids = [], []
    for _ in range(k):
        m = jnp.max(s, axis=0, keepdims=True)
        ix = jnp.min(jnp.where(s == m, rid, n), axis=0, keepdims=True)
        vals.append(m)
        ids.append(ix)
        s = jnp.where(rid == ix, -jnp.inf, s)
    return jnp.concatenate(vals, axis=0), jnp.concatenate(ids, axis=0)


def _route_kernel(q_ref, sk_ref, e1_ref, e2_ref, g_ref):
    K = PEER_TOPK
    halves = []
    for p in range(2):
        qh = q_ref[:, p * PEER_HALF:(p + 1) * PEER_HALF].astype(BF16)
        st = lax.dot_general(sk_ref[0, p].astype(BF16), qh, (((1,), (1,)), ((), ())),
                             preferred_element_type=F32)
        halves.append(_topk_rows(st, K))
    (s1, i1), (s2, i2) = halves
    cand = jnp.concatenate([s1[a:a + 1] + s2 for a in range(K)], axis=0)
    top, ci = _topk_rows(cand, K)
    a_id = ci // K
    b_id = ci % K
    e1 = jnp.zeros_like(ci)
    e2 = jnp.zeros_like(ci)
    for a in range(K):
        e1 = jnp.where(a_id == a, i1[a:a + 1], e1)
        e2 = jnp.where(b_id == a, i2[a:a + 1], e2)
    e = jnp.exp(top - top[0:1])
    e1_ref[0] = e1
    e2_ref[0] = e2
    g_ref[0] = e / jnp.sum(e, axis=0, keepdims=True)


ROUTE_TILE = 256


def _peer_route(q, sub_keys):
    n = q.shape[0]
    tb = ROUTE_TILE
    spec = pl.BlockSpec((1, PEER_TOPK, tb), lambda i, h: (h, 0, i))
    nbytes = 2 * (tb * PEER_KEY_DIM * 4 + 2 * PEER_NKEYS * PEER_HALF * 4) + 16 * 2 * PEER_NKEYS * tb * 4
    return pl.pallas_call(
        _route_kernel,
        grid=(n // tb, PEER_HEADS),
        in_specs=[pl.BlockSpec((tb, PEER_KEY_DIM), lambda i, h: (i, h)),
                  pl.BlockSpec((1, 2, PEER_NKEYS, PEER_HALF), lambda i, h: (h, 0, 0, 0))],
        out_specs=[spec, spec, spec],
        out_shape=[jax.ShapeDtypeStruct((PEER_HEADS, PEER_TOPK, n), jnp.int32),
                   jax.ShapeDtypeStruct((PEER_HEADS, PEER_TOPK, n), jnp.int32),
                   jax.ShapeDtypeStruct((PEER_HEADS, PEER_TOPK, n), F32)],
        compiler_params=_params(("parallel", "parallel"), nbytes),
        name="peer_route",
    )(q, sub_keys)


EXPERT_TILE = 256
EXPERT_CHUNK = 2048
EXPERT_SUB = 512


def _expert_kernel(h_ref, e1_ref, e2_ref, g_ref, ut_ref, v_ref, x_ref, gt_ref, o_ref,
                   gate_ref, acc_ref, e1t_ref, e2t_ref, gtt_ref):
    c = pl.program_id(1)
    tb = h_ref.shape[0]
    NK = PEER_NKEYS

    @pl.when(c == 0)
    def _():
        acc_ref[...] = jnp.zeros_like(acc_ref)
        e1t_ref[...] = jnp.transpose(e1_ref[...].reshape(PEER_PAIRS, tb))
        e2t_ref[...] = jnp.transpose(e2_ref[...].reshape(PEER_PAIRS, tb))
        gtt_ref[...] = jnp.transpose(g_ref[...].reshape(PEER_PAIRS, tb))
        kid = lax.broadcasted_iota(jnp.int32, (NK, PEER_PAIRS), 0)

        def per_token(n, carry):
            i1 = e1t_ref[pl.ds(n, 1), :]
            i2 = e2t_ref[pl.ds(n, 1), :]
            gg = gtt_ref[pl.ds(n, 1), :]
            a_t = jnp.where(kid == i1, 1.0, 0.0).astype(BF16)
            b_t = jnp.where(kid == i2, gg, 0.0).astype(BF16)
            gn = lax.dot_general(a_t, b_t, (((1,), (1,)), ((), ())), preferred_element_type=F32)
            gate_ref[pl.ds(pl.multiple_of(n * NK, NK), NK), :] = gn
            return carry

        lax.fori_loop(0, tb, per_token, 0)

    h = h_ref[...]
    keys_per_chunk = EXPERT_CHUNK // NK
    for j in range(EXPERT_CHUNK // EXPERT_SUB):
        lo = j * EXPERT_SUB
        s = jnp.dot(h, ut_ref[:, lo:lo + EXPERT_SUB], preferred_element_type=F32)
        gsel = jnp.concatenate(
            [gate_ref[pl.ds(c * keys_per_chunk + lo // NK + t, tb, stride=NK), :]
             for t in range(EXPERT_SUB // NK)], axis=1)
        w = (_gelu_tanh(s) * gsel).astype(BF16)
        acc_ref[...] += jnp.dot(w, v_ref[lo:lo + EXPERT_SUB, :], preferred_element_type=F32)

    @pl.when(c == pl.num_programs(1) - 1)
    def _():
        o_ref[...] = x_ref[...] + gt_ref[0] * acc_ref[...]


def _peer_experts(h_bf16, e1, e2, g, ut_bf16, v_bf16, x, gate, seq_len):
    n, d = x.shape
    tb = EXPERT_TILE
    ec = EXPERT_CHUNK
    gt_arr, gt_spec0 = _row_operand(gate, seq_len, tb)
    gt_spec = pl.BlockSpec(gt_spec0.block_shape, lambda i, c: gt_spec0.index_map(i))
    rspec = pl.BlockSpec((PEER_HEADS, PEER_TOPK, tb), lambda i, c: (0, 0, i))
    nbytes = (2 * (tb * d * 2 + 3 * PEER_PAIRS * tb * 4 + 2 * d * ec * 2 + 3 * tb * d * 4)
              + tb * PEER_NKEYS * PEER_NKEYS * 4 + tb * d * 4 + 3 * tb * PEER_PAIRS * 4 + 8 * tb * EXPERT_SUB * 4)
    return pl.pallas_call(
        _expert_kernel,
        grid=(n // tb, PEER_EXPERTS // ec),
        in_specs=[pl.BlockSpec((tb, d), lambda i, c: (i, 0)),
                  rspec, rspec, rspec,
                  pl.BlockSpec((d, ec), lambda i, c: (0, c)),
                  pl.BlockSpec((ec, d), lambda i, c: (c, 0)),
                  pl.BlockSpec((tb, d), lambda i, c: (i, 0)),
                  gt_spec],
        out_specs=pl.BlockSpec((tb, d), lambda i, c: (i, 0)),
        out_shape=jax.ShapeDtypeStruct((n, d), F32),
        scratch_shapes=[pltpu.VMEM((tb * PEER_NKEYS, PEER_NKEYS), F32),
                        pltpu.VMEM((tb, d), F32),
                        pltpu.VMEM((tb, PEER_PAIRS), jnp.int32),
                        pltpu.VMEM((tb, PEER_PAIRS), jnp.int32),
                        pltpu.VMEM((tb, PEER_PAIRS), F32)],
        compiler_params=_params(("parallel", "arbitrary"), nbytes),
        name="peer_experts",
    )(h_bf16, e1, e2, g, ut_bf16, v_bf16, x, gt_arr)


def _swa_permute_in(w_in):
    d = w_in.shape[0]
    wq = w_in[:, :SWA_Q].reshape(d, SWA_KV_HEADS, SWA_GROUP, SWA_HEAD_DIM).transpose(0, 2, 1, 3).reshape(d, SWA_Q)
    return jnp.concatenate([wq, w_in[:, SWA_Q:]], axis=1)


def _swa_permute_out(w_out):
    d = w_out.shape[1]
    return w_out.reshape(SWA_KV_HEADS, SWA_GROUP, SWA_HEAD_DIM, d).transpose(1, 0, 2, 3).reshape(SWA_Q, d)


def _prepare_weights(p):
    w = {}
    w['gla_in'] = [jnp.pad(p['w_gla_in'][j], ((0, 0), (0, GLA_IN_PAD - p['w_gla_in'].shape[2]))).astype(BF16)
                   for j in range(p['w_gla_in'].shape[0])]
    w['gla_a2'] = [jnp.pad(p['w_gla_a2'][j], ((0, V7X_LANES - GLA_GATE_RANK), (0, 0)))
                   for j in range(p['w_gla_a2'].shape[0])]
    w['gla_out'] = [m.astype(BF16) for m in p['w_gla_out']]
    w['swa_in'] = [_swa_permute_in(m).astype(BF16) for m in p['w_swa_in']]
    w['swa_out'] = [_swa_permute_out(m).astype(BF16) for m in p['w_swa_out']]
    w['lru_in'] = [m.astype(BF16) for m in p['w_lru_in']]
    w['lru_out'] = [m.astype(BF16) for m in p['w_lru_out']]
    w['peer_q'] = [m.astype(BF16) for m in p['w_peer_q']]
    w['peer_ut'] = [m.T.astype(BF16) for m in p['peer_u']]
    w['peer_v'] = [m.astype(BF16) for m in p['peer_v']]
    return w


def _trunk(x3, mod, states, p, w):
    batch, seq_len, d = x3.shape
    x = x3.reshape(batch * seq_len, d)
    new_gla, new_k, new_v, new_conv, new_h = [], [], [], [], []
    for i in range(DEPTH):
        kind, j = i % N_MIXERS, i // N_MIXERS
        sh_m, sc_m, gt_m, sh_f, sc_f, gt_f = [mod[i][:, k * d:(k + 1) * d] for k in range(6)]
        if kind == 0:
            proj = _norm_proj(x, p['g_ln_mix'][i], sc_m, sh_m, w['gla_in'][j], seq_len)
            if states is None:
                mix, s_new = _gla_prompt(proj, w['gla_a2'][j], p['b_gla_a'][j], p['g_gla_norm'][j], batch, seq_len)
            else:
                mix, s_new = _gla_sample(proj, states[0][j], w['gla_a2'][j], p['b_gla_a'][j], p['g_gla_norm'][j],
                                         batch, seq_len)
            new_gla.append(s_new)
            w_out = w['gla_out'][j]
        elif kind == 1:
            proj = _norm_proj(x, p['g_ln_mix'][i], sc_m, sh_m, w['swa_in'][j], seq_len)
            if states is None:
                mix, k_n, v_n = _swa_prompt(proj, p['g_swa_q'][j], p['g_swa_k'][j], p['swa_sinks'][j], batch, seq_len)
            else:
                kc = states[1][j].reshape(batch, WINDOW, SWA_KV)
                vc = states[2][j].reshape(batch, WINDOW, SWA_KV)
                mix, k_n, v_n = _swa_sample(proj, kc, vc, p['g_swa_q'][j], p['g_swa_k'][j], p['swa_sinks'][j],
                                            batch, seq_len)
            new_k.append(k_n.reshape(batch, WINDOW, SWA_KV_HEADS, SWA_HEAD_DIM))
            new_v.append(v_n.reshape(batch, WINDOW, SWA_KV_HEADS, SWA_HEAD_DIM))
            w_out = w['swa_out'][j]
        else:
            proj = _norm_proj(x, p['g_ln_mix'][i], sc_m, sh_m, w['lru_in'][j], seq_len)
            lru_args = (p['lru_conv_w'][j], p['lru_conv_b'][j], p['w_lru_ga'][j], p['b_lru_ga'][j],
                        p['w_lru_gx'][j], p['b_lru_gx'][j], p['lru_lam'][j], batch, seq_len)
            assert seq_len >= CONV_WIDTH - 1
            if states is None:
                mix, h_n = _lru_prompt(proj, *lru_args)
            else:
                mix, hs = _lru_sample(proj, states[3][j], states[4][j], *lru_args)
                h_n = hs.reshape(batch, seq_len, D_RNN)[:, -1]
            new_conv.append(proj[:, D_RNN:].reshape(batch, seq_len, D_RNN)[:, seq_len - (CONV_WIDTH - 1):])
            new_h.append(h_n)
            w_out = w['lru_out'][j]
        x = _proj_residual(mix, w_out, x, gt_m, seq_len)
        q, hb = _norm_proj(x, p['g_ln_ffn'][i], sc_f, sh_f, w['peer_q'][i], seq_len, with_h=True)
        e1, e2, g = _peer_route(q, p['peer_sub_keys'][i])
        x = _peer_experts(hb, e1, e2, g, w['peer_ut'][i], w['peer_v'][i], x, gt_f, seq_len)
    y = x.reshape(batch, seq_len, d)
    return y, (jnp.stack(new_gla), jnp.stack(new_k), jnp.stack(new_v), jnp.stack(new_conv), jnp.stack(new_h))


def kernel(x_prompt, x_sample, state_gla, cache_swa_k, cache_swa_v, state_lru_conv, state_lru_h,
           c_prompt, c_sample, g_ln_mix, g_ln_ffn, w_mod, b_mod,
           w_gla_in, w_gla_a2, b_gla_a, g_gla_norm, w_gla_out,
           w_swa_in, g_swa_q, g_swa_k, swa_sinks, w_swa_out,
           w_lru_in, lru_conv_w, lru_conv_b, w_lru_ga, b_lru_ga, w_lru_gx, b_lru_gx, lru_lam, w_lru_out,
           w_peer_q, peer_sub_keys, peer_u, peer_v):
    p = {'g_ln_mix': g_ln_mix, 'g_ln_ffn': g_ln_ffn,
         'w_gla_in': w_gla_in, 'w_gla_a2': w_gla_a2, 'b_gla_a': b_gla_a, 'g_gla_norm': g_gla_norm,
         'w_gla_out': w_gla_out,
         'w_swa_in': w_swa_in, 'g_swa_q': g_swa_q, 'g_swa_k': g_swa_k, 'swa_sinks': swa_sinks,
         'w_swa_out': w_swa_out,
         'w_lru_in': w_lru_in, 'lru_conv_w': lru_conv_w, 'lru_conv_b': lru_conv_b,
         'w_lru_ga': w_lru_ga, 'b_lru_ga': b_lru_ga, 'w_lru_gx': w_lru_gx, 'b_lru_gx': b_lru_gx,
         'lru_lam': lru_lam, 'w_lru_out': w_lru_out,
         'w_peer_q': w_peer_q, 'peer_sub_keys': peer_sub_keys, 'peer_u': peer_u, 'peer_v': peer_v}
    w = _prepare_weights(p)
    nb_p, nb_s = c_prompt.shape[0], c_sample.shape[0]
    rows = -(-(nb_p + nb_s) // V7X_SUBLANES) * V7X_SUBLANES
    c_all = jnp.pad(jnp.concatenate([c_prompt, c_sample], axis=0), ((0, rows - nb_p - nb_s), (0, 0)))
    mod = _modulation(c_all, w_mod, b_mod)
    y_p, (gla_p, k_p, v_p, conv_p, h_p) = _trunk(x_prompt, mod[:, :nb_p], None, p, w)
    y_s, (gla_s, k_s, v_s, conv_s, h_s) = _trunk(
        x_sample, mod[:, nb_p:nb_p + nb_s],
        (state_gla, cache_swa_k, cache_swa_v, state_lru_conv, state_lru_h), p, w)
    return (y_p, y_s, gla_p, gla_s, k_p, k_s, v_p, v_s, conv_p, conv_s, h_p, h_s)
```

```python
import functools
import math

import jax
import jax.numpy as jnp
from jax import lax
from jax.experimental import pallas as pl
from jax.experimental.pallas import tpu as pltpu

F32 = jnp.float32
BF16 = jnp.bfloat16

D_MODEL = 1024
DEPTH = 4
N_MIXERS = 3
RMS_EPS = 1e-6

GLA_HEADS = 4
GLA_QK = D_MODEL // 2
GLA_V = D_MODEL
GLA_DK = GLA_QK // GLA_HEADS
GLA_DV = GLA_V // GLA_HEADS
GLA_GATE_RANK = 16
GLA_TAU = 16.0
GLA_SUB = 16
GLA_CHUNK = 128
GLA_IN_PAD = 2 * GLA_QK + 2 * GLA_V + 128

SWA_HEAD_DIM = 64
SWA_Q_HEADS = D_MODEL // SWA_HEAD_DIM
SWA_KV_HEADS = 4
SWA_GROUP = SWA_Q_HEADS // SWA_KV_HEADS
SWA_Q = SWA_Q_HEADS * SWA_HEAD_DIM
SWA_KV = SWA_KV_HEADS * SWA_HEAD_DIM
WINDOW = 128

D_RNN = D_MODEL
LRU_BLOCKS = 4
LRU_BLOCK = D_RNN // LRU_BLOCKS
CONV_WIDTH = 4
LRU_C = 8.0

PEER_HEADS = 8
PEER_NKEYS = 128
PEER_EXPERTS = PEER_NKEYS * PEER_NKEYS
PEER_KEY_DIM = 256
PEER_HALF = PEER_KEY_DIM // 2
PEER_TOPK = 16
PEER_PAIRS = PEER_HEADS * PEER_TOPK

V7X_LANES = 128
V7X_SUBLANES = 8
V7X_VMEM_BYTES = 64 * 1024 * 1024

TOKEN_TILE = 256
NEG_BIG = -1e30


def _vmem_limit(nbytes):
    return int(min(max(nbytes * 3 // 2, 16 * 1024 * 1024), V7X_VMEM_BYTES - 8 * 1024 * 1024))


def _params(semantics, nbytes):
    return pltpu.CompilerParams(dimension_semantics=semantics, vmem_limit_bytes=_vmem_limit(nbytes))


def _rms(x, g):
    return x * lax.rsqrt(jnp.mean(x * x, axis=-1, keepdims=True) + RMS_EPS) * g


def _gelu_tanh(x):
    return 0.5 * x * (1.0 + jnp.tanh(math.sqrt(2.0 / math.pi) * (x + 0.044715 * (x * x * x))))


def _sigmoid(x):
    return 1.0 / (1.0 + jnp.exp(-x))


def _softplus(x):
    return jnp.maximum(x, 0.0) + jnp.log1p(jnp.exp(-jnp.abs(x)))


def _row_operand(vec, seq_len, tile):
    b, d = vec.shape
    if seq_len % tile == 0:
        per_seq = seq_len // tile
        return vec.reshape(b, 1, d), pl.BlockSpec((1, 1, d), lambda i: (i // per_seq, 0, 0))
    assert tile % seq_len == 0
    rep = jnp.repeat(vec, seq_len, axis=0).reshape(b * seq_len // tile, tile, d)
    return rep, pl.BlockSpec((1, tile, d), lambda i: (i, 0, 0))


def _mod_kernel(c_ref, w_ref, b_ref, o_ref):
    c = c_ref[...]
    sc = c * _sigmoid(c)
    o_ref[0] = jnp.dot(sc, w_ref[0], preferred_element_type=F32) + b_ref[0]


def _modulation(c, w_mod, b_mod):
    bp, d = c.shape
    tn = 1024
    nt = 6 * d // tn
    return pl.pallas_call(
        _mod_kernel,
        grid=(DEPTH, nt),
        in_specs=[pl.BlockSpec((bp, d), lambda l, j: (0, 0)),
                  pl.BlockSpec((1, d, tn), lambda l, j: (l, 0, j)),
                  pl.BlockSpec((1, 1, tn), lambda l, j: (l, 0, j))],
        out_specs=pl.BlockSpec((1, bp, tn), lambda l, j: (l, 0, j)),
        out_shape=jax.ShapeDtypeStruct((DEPTH, bp, 6 * d), F32),
        compiler_params=_params(("parallel", "parallel"), 2 * (d * tn * 4 + 2 * bp * tn * 4)),
        name="modulation",
    )(c, w_mod, b_mod.reshape(DEPTH, 1, 6 * d))


def _norm_proj_kernel(x_ref, g_ref, sc_ref, sh_ref, w_ref, o_ref, *h_ref):
    h = _rms(x_ref[...], g_ref[...]) * (1.0 + sc_ref[0]) + sh_ref[0]
    hb = h.astype(BF16)
    o_ref[...] = jnp.dot(hb, w_ref[...], preferred_element_type=F32)
    if h_ref:
        h_ref[0][...] = hb


def _norm_proj(x, g, scale, shift, w_bf16, seq_len, with_h=False):
    n, d = x.shape
    nout = w_bf16.shape[1]
    tm = TOKEN_TILE
    sc_arr, sc_spec = _row_operand(scale, seq_len, tm)
    sh_arr, sh_spec = _row_operand(shift, seq_len, tm)
    out_shape = [jax.ShapeDtypeStruct((n, nout), F32)]
    out_specs = [pl.BlockSpec((tm, nout), lambda i: (i, 0))]
    if with_h:
        out_shape.append(jax.ShapeDtypeStruct((n, d), BF16))
        out_specs.append(pl.BlockSpec((tm, d), lambda i: (i, 0)))
    nbytes = 2 * (tm * d * 4 + d * nout * 2 + tm * nout * 4 + 3 * tm * d * 4)
    res = pl.pallas_call(
        _norm_proj_kernel,
        grid=(n // tm,),
        in_specs=[pl.BlockSpec((tm, d), lambda i: (i, 0)),
                  pl.BlockSpec((1, d), lambda i: (0, 0)),
                  sc_spec, sh_spec,
                  pl.BlockSpec((d, nout), lambda i: (0, 0))],
        out_specs=out_specs,
        out_shape=out_shape,
        compiler_params=_params(("parallel",), nbytes),
        name="norm_proj",
    )(x, g.reshape(1, d), sc_arr, sh_arr, w_bf16)
    return res if with_h else res[0]


def _proj_residual_kernel(a_ref, w_ref, x_ref, gt_ref, o_ref):
    y = jnp.dot(a_ref[...].astype(BF16), w_ref[...], preferred_element_type=F32)
    o_ref[...] = x_ref[...] + gt_ref[0] * y


def _proj_residual(a, w_bf16, x, gate, seq_len):
    n, k = a.shape
    d = x.shape[1]
    tm = TOKEN_TILE
    gt_arr, gt_spec = _row_operand(gate, seq_len, tm)
    nbytes = 2 * (tm * k * 4 + k * d * 2 + 3 * tm * d * 4)
    return pl.pallas_call(
        _proj_residual_kernel,
        grid=(n // tm,),
        in_specs=[pl.BlockSpec((tm, k), lambda i: (i, 0)),
                  pl.BlockSpec((k, d), lambda i: (0, 0)),
                  pl.BlockSpec((tm, d), lambda i: (i, 0)),
                  gt_spec],
        out_specs=pl.BlockSpec((tm, d), lambda i: (i, 0)),
        out_shape=jax.ShapeDtypeStruct((n, d), F32),
        compiler_params=_params(("parallel",), nbytes),
        name="proj_residual",
    )(a, w_bf16, x, gt_arr)


def _log_decay(lr, wa2, ba):
    z = jnp.dot(lr, wa2, preferred_element_type=F32) + ba
    return (jnp.minimum(z, 0.0) - jnp.log1p(jnp.exp(-jnp.abs(z)))) * (1.0 / GLA_TAU)


def _col_bcast(row):
    return jnp.transpose(jnp.broadcast_to(row, (V7X_LANES, V7X_LANES)))


def _head_out(o, gate, gn):
    return _rms(o, gn) * (gate * _sigmoid(gate))


def _gla_prompt_kernel(p_ref, wa2_ref, ba_ref, gn_ref, o_ref, s_ref):
    c = pl.program_id(1)
    C = GLA_CHUNK
    nsub = C // GLA_SUB

    @pl.when(c == 0)
    def _():
        s_ref[...] = jnp.zeros_like(s_ref)

    row = lax.broadcasted_iota(jnp.int32, (C, C), 0)
    col = lax.broadcasted_iota(jnp.int32, (C, C), 1)
    tri = (col <= row).astype(F32)
    later = ((col > row) & (col // GLA_SUB == row // GLA_SUB)).astype(F32)
    sums = jnp.concatenate([tri, later], axis=0)
    causal = col <= row
    rsub = lax.broadcasted_iota(jnp.int32, (C, GLA_DK), 0) // GLA_SUB

    lr = p_ref[0, :, 2 * GLA_QK + 2 * GLA_V:]
    for h in range(GLA_HEADS):
        q = p_ref[0, :, h * GLA_DK:(h + 1) * GLA_DK] * (GLA_DK ** -0.5)
        k = p_ref[0, :, GLA_QK + h * GLA_DK:GLA_QK + (h + 1) * GLA_DK]
        v = p_ref[0, :, 2 * GLA_QK + h * GLA_DV:2 * GLA_QK + (h + 1) * GLA_DV]
        gate = p_ref[0, :, 2 * GLA_QK + GLA_V + h * GLA_DV:2 * GLA_QK + GLA_V + (h + 1) * GLA_DV]
        la = _log_decay(lr, wa2_ref[:, h * GLA_DK:(h + 1) * GLA_DK], ba_ref[:, h * GLA_DK:(h + 1) * GLA_DK])
        cs = jnp.dot(sums, la, preferred_element_type=F32, precision=lax.Precision.HIGHEST)
        b = cs[:C]
        to_sub_end = cs[C:]
        b_last = b[C - 1:C]
        k_sub = k * jnp.exp(to_sub_end)
        q_parts, k_parts = [], []
        for m in range(nsub):
            ref_row = b[m * GLA_SUB + GLA_SUB - 1:m * GLA_SUB + GLA_SUB]
            e = jnp.where(rsub >= m, b - ref_row, NEG_BIG)
            q_parts.append((q * jnp.exp(e)).astype(BF16))
            k_parts.append(jnp.where(rsub == m, k_sub, 0.0).astype(BF16))
        qcat = jnp.concatenate(q_parts, axis=1)
        kcat = jnp.concatenate(k_parts, axis=1)
        att = lax.dot_general(qcat, kcat, (((1,), (1,)), ((), ())), preferred_element_type=F32)
        att = jnp.where(causal, att, 0.0)
        s_old = s_ref[0, h]
        o = jnp.dot(att.astype(BF16), v.astype(BF16), preferred_element_type=F32)
        o = o + jnp.dot((q * jnp.exp(b)).astype(BF16), s_old.astype(BF16), preferred_element_type=F32)
        k_end = (k * jnp.exp(b_last - b)).astype(BF16)
        upd = lax.dot_general(k_end, v.astype(BF16), (((0,), (0,)), ((), ())), preferred_element_type=F32)
        decay = _col_bcast(jnp.exp(b_last))
        s_ref[0, h] = jnp.concatenate([decay] * (GLA_DV // V7X_LANES), axis=1) * s_old + upd
        o_ref[0, :, h * GLA_DV:(h + 1) * GLA_DV] = _head_out(o, gate, gn_ref[...])


def _gla_prompt(proj, wa2_pad, ba, gn, batch, seq_len):
    C = GLA_CHUNK
    p3 = proj.reshape(batch, seq_len, GLA_IN_PAD)
    nbytes = 2 * (C * GLA_IN_PAD * 4 + C * GLA_V * 4 + GLA_HEADS * GLA_DK * GLA_DV * 4) + 64 * C * C * 4
    o, s = pl.pallas_call(
        _gla_prompt_kernel,
        grid=(batch, seq_len // C),
        in_specs=[pl.BlockSpec((1, C, GLA_IN_PAD), lambda b, c: (b, c, 0)),
                  pl.BlockSpec((V7X_LANES, GLA_QK), lambda b, c: (0, 0)),
                  pl.BlockSpec((1, GLA_QK), lambda b, c: (0, 0)),
                  pl.BlockSpec((1, GLA_DV), lambda b, c: (0, 0))],
        out_specs=[pl.BlockSpec((1, C, GLA_V), lambda b, c: (b, c, 0)),
                   pl.BlockSpec((1, GLA_HEADS, GLA_DK, GLA_DV), lambda b, c: (b, 0, 0, 0))],
        out_shape=[jax.ShapeDtypeStruct((batch, seq_len, GLA_V), F32),
                   jax.ShapeDtypeStruct((batch, GLA_HEADS, GLA_DK, GLA_DV), F32)],
        compiler_params=_params(("parallel", "arbitrary"), nbytes),
        name="gla_prompt",
    )(p3, wa2_pad, ba.reshape(1, GLA_QK), gn.reshape(1, GLA_DV))
    return o.reshape(batch * seq_len, GLA_V), s


GLA_SAMPLE_BATCH = 8


def _gla_sample_kernel(p_ref, s0_ref, wa2_ref, ba_ref, gn_ref, o_ref, s_ref):
    T = p_ref.shape[1]
    row = lax.broadcasted_iota(jnp.int32, (T, T), 0)
    col = lax.broadcasted_iota(jnp.int32, (T, T), 1)
    causal = col <= row
    tri = causal.astype(F32)
    for j in range(p_ref.shape[0]):
        lr = p_ref[j, :, 2 * GLA_QK + 2 * GLA_V:]
        for h in range(GLA_HEADS):
            q = p_ref[j, :, h * GLA_DK:(h + 1) * GLA_DK] * (GLA_DK ** -0.5)
            k = p_ref[j, :, GLA_QK + h * GLA_DK:GLA_QK + (h + 1) * GLA_DK]
            v = p_ref[j, :, 2 * GLA_QK + h * GLA_DV:2 * GLA_QK + (h + 1) * GLA_DV]
            gate = p_ref[j, :, 2 * GLA_QK + GLA_V + h * GLA_DV:2 * GLA_QK + GLA_V + (h + 1) * GLA_DV]
            la = _log_decay(lr, wa2_ref[:, h * GLA_DK:(h + 1) * GLA_DK], ba_ref[:, h * GLA_DK:(h + 1) * GLA_DK])
            b = jnp.dot(tri, la, preferred_element_type=F32, precision=lax.Precision.HIGHEST)
            b_last = b[T - 1:T]
            k_end = (k * jnp.exp(b_last - b)).astype(BF16)
            q_rel = (q * jnp.exp(b - b_last)).astype(BF16)
            att = lax.dot_general(q_rel, k_end, (((1,), (1,)), ((), ())), preferred_element_type=F32)
            att = jnp.where(causal, att, 0.0)
            s_old = s0_ref[j, h]
            vb = v.astype(BF16)
            o = jnp.dot(att.astype(BF16), vb, preferred_element_type=F32)
            o = o + jnp.dot((q * jnp.exp(b)).astype(BF16), s_old.astype(BF16), preferred_element_type=F32)
            upd = lax.dot_general(k_end, vb, (((0,), (0,)), ((), ())), preferred_element_type=F32)
            decay = _col_bcast(jnp.exp(b_last))
            s_ref[j, h] = jnp.concatenate([decay] * (GLA_DV // V7X_LANES), axis=1) * s_old + upd
            o_ref[j, :, h * GLA_DV:(h + 1) * GLA_DV] = _head_out(o, gate, gn_ref[...])


def _gla_sample(proj, s0, wa2_pad, ba, gn, batch, seq_len):
    assert seq_len <= GLA_SUB
    nb = GLA_SAMPLE_BATCH
    p3 = proj.reshape(batch, seq_len, GLA_IN_PAD)
    state_block = nb * GLA_HEADS * GLA_DK * GLA_DV * 4
    nbytes = 2 * (nb * seq_len * (GLA_IN_PAD + GLA_V) * 4 + 2 * state_block)
    o, s = pl.pallas_call(
        _gla_sample_kernel,
        grid=(batch // nb,),
        in_specs=[pl.BlockSpec((nb, seq_len, GLA_IN_PAD), lambda b: (b, 0, 0)),
                  pl.BlockSpec((nb, GLA_HEADS, GLA_DK, GLA_DV), lambda b: (b, 0, 0, 0)),
                  pl.BlockSpec((V7X_LANES, GLA_QK), lambda b: (0, 0)),
                  pl.BlockSpec((1, GLA_QK), lambda b: (0, 0)),
                  pl.BlockSpec((1, GLA_DV), lambda b: (0, 0))],
        out_specs=[pl.BlockSpec((nb, seq_len, GLA_V), lambda b: (b, 0, 0)),
                   pl.BlockSpec((nb, GLA_HEADS, GLA_DK, GLA_DV), lambda b: (b, 0, 0, 0))],
        out_shape=[jax.ShapeDtypeStruct((batch, seq_len, GLA_V), F32),
                   jax.ShapeDtypeStruct((batch, GLA_HEADS, GLA_DK, GLA_DV), F32)],
        compiler_params=_params(("parallel",), nbytes),
        name="gla_sample",
    )(p3, s0, wa2_pad, ba.reshape(1, GLA_QK), gn.reshape(1, GLA_DV))
    return o.reshape(batch * seq_len, GLA_V), s


def _head_group_norm(x, gain, gsum):
    sq = x * x
    hi = sq.astype(BF16)
    lo = (sq - hi.astype(F32)).astype(BF16)
    ms = (jnp.dot(hi, gsum, preferred_element_type=F32) + jnp.dot(lo, gsum, preferred_element_type=F32))
    return x * lax.rsqrt(ms * (1.0 / SWA_HEAD_DIM) + RMS_EPS) * gain


def _swa_attend(q_groups, k_all, v_all, mask, sink_ref):
    tq = q_groups[0].shape[0]
    lane_head = lax.broadcasted_iota(jnp.int32, (tq, SWA_KV), 1) // SWA_HEAD_DIM
    mask_rows = jnp.concatenate([mask] * SWA_GROUP, axis=0)
    out = [jnp.zeros((tq, SWA_KV), F32) for _ in range(SWA_GROUP)]
    for kv in range(SWA_KV_HEADS):
        in_head = lane_head == kv
        qs = jnp.concatenate([jnp.where(in_head, qg, 0.0) for qg in q_groups], axis=0).astype(BF16)
        s = lax.dot_general(qs, k_all, (((1,), (1,)), ((), ())), preferred_element_type=F32)
        s = jnp.where(mask_rows, s, -jnp.inf)
        sink = jnp.concatenate(
            [jnp.full((tq, 1), sink_ref[kv * SWA_GROUP + g], F32) for g in range(SWA_GROUP)], axis=0)
        m = jnp.maximum(jnp.max(s, axis=1, keepdims=True), sink)
        p = jnp.exp(s - m)
        denom = jnp.sum(p, axis=1, keepdims=True) + jnp.exp(sink - m)
        pv = jnp.dot(p.astype(BF16), v_all, preferred_element_type=F32) / denom
        for g in range(SWA_GROUP):
            out[g] = jnp.where(in_head, pv[g * tq:(g + 1) * tq], out[g])
    return out


def _swa_prompt_kernel(sink_ref, cur_ref, prev_ref, gq_ref, gk_ref, gsum_ref, o_ref, k_ref, v_ref):
    n = pl.program_id(1)
    gsum = gsum_ref[...]
    k_cur = _head_group_norm(cur_ref[0, :, SWA_Q:SWA_Q + SWA_KV], gk_ref[...], gsum)
    k_prev = _head_group_norm(prev_ref[0, :, SWA_Q:SWA_Q + SWA_KV], gk_ref[...], gsum)
    v_cur = cur_ref[0, :, SWA_Q + SWA_KV:]
    v_prev = prev_ref[0, :, SWA_Q + SWA_KV:]
    k_all = jnp.concatenate([k_prev, k_cur], axis=0).astype(BF16)
    v_all = jnp.concatenate([v_prev, v_cur], axis=0).astype(BF16)
    t = lax.broadcasted_iota(jnp.int32, (WINDOW, 2 * WINDOW), 0)
    s = lax.broadcasted_iota(jnp.int32, (WINDOW, 2 * WINDOW), 1)
    mask = (s >= t) & (s <= t + WINDOW) & ((s >= WINDOW) | (n > 0))
    q_groups = [_head_group_norm(cur_ref[0, :, g * SWA_KV:(g + 1) * SWA_KV], gq_ref[...], gsum)
                * (SWA_HEAD_DIM ** -0.5) for g in range(SWA_GROUP)]
    out = _swa_attend(q_groups, k_all, v_all, mask, sink_ref)
    o_ref[0] = jnp.concatenate(out, axis=1)
    k_ref[0] = k_cur
    v_ref[0] = v_cur


def _swa_gsum():
    head = jnp.arange(SWA_KV) // SWA_HEAD_DIM
    return (head[:, None] == head[None, :]).astype(BF16)


def _swa_prompt(proj, gq, gk, sinks, batch, seq_len):
    W = WINDOW
    width = SWA_Q + 2 * SWA_KV
    p3 = proj.reshape(batch, seq_len, width)
    nbytes = 2 * (2 * W * width * 4 + W * SWA_Q * 4 + 2 * W * SWA_KV * 4) + 48 * W * 2 * W * 4
    o, k, v = pl.pallas_call(
        _swa_prompt_kernel,
        grid=(batch, seq_len // W),
        in_specs=[pl.BlockSpec(memory_space=pltpu.SMEM),
                  pl.BlockSpec((1, W, width), lambda b, n: (b, n, 0)),
                  pl.BlockSpec((1, W, width), lambda b, n: (b, jnp.maximum(n - 1, 0), 0)),
                  pl.BlockSpec((1, SWA_KV), lambda b, n: (0, 0)),
                  pl.BlockSpec((1, SWA_KV), lambda b, n: (0, 0)),
                  pl.BlockSpec((SWA_KV, SWA_KV), lambda b, n: (0, 0))],
        out_specs=[pl.BlockSpec((1, W, SWA_Q), lambda b, n: (b, n, 0)),
                   pl.BlockSpec((1, W, SWA_KV), lambda b, n: (b, 0, 0)),
                   pl.BlockSpec((1, W, SWA_KV), lambda b, n: (b, 0, 0))],
        out_shape=[jax.ShapeDtypeStruct((batch, seq_len, SWA_Q), F32),
                   jax.ShapeDtypeStruct((batch, W, SWA_KV), F32),
                   jax.ShapeDtypeStruct((batch, W, SWA_KV), F32)],
        compiler_params=_params(("parallel", "arbitrary"), nbytes),
        name="swa_prompt",
    )(sinks, p3, p3, jnp.tile(gq, SWA_KV_HEADS).reshape(1, SWA_KV), jnp.tile(gk, SWA_KV_HEADS).reshape(1, SWA_KV),
      _swa_gsum())
    return o.reshape(batch * seq_len, SWA_Q), k, v


SWA_SAMPLE_BATCH = 8


def _swa_sample_kernel(sink_ref, p_ref, kc_ref, vc_ref, gq_ref, gk_ref, gsum_ref, o_ref, k_ref, v_ref):
    T = p_ref.shape[1]
    gsum = gsum_ref[...]
    t = lax.broadcasted_iota(jnp.int32, (T, WINDOW + T), 0)
    s = lax.broadcasted_iota(jnp.int32, (T, WINDOW + T), 1)
    mask = (s >= t) & (s <= t + WINDOW)
    for j in range(p_ref.shape[0]):
        k_new = _head_group_norm(p_ref[j, :, SWA_Q:SWA_Q + SWA_KV], gk_ref[...], gsum)
        v_new = p_ref[j, :, SWA_Q + SWA_KV:]
        k_all = jnp.concatenate([kc_ref[j], k_new], axis=0)
        v_all = jnp.concatenate([vc_ref[j], v_new], axis=0)
        q_groups = [_head_group_norm(p_ref[j, :, g * SWA_KV:(g + 1) * SWA_KV], gq_ref[...], gsum)
                    * (SWA_HEAD_DIM ** -0.5) for g in range(SWA_GROUP)]
        out = _swa_attend(q_groups, k_all.astype(BF16), v_all.astype(BF16), mask, sink_ref)
        o_ref[j] = jnp.concatenate(out, axis=1)
        k_ref[j] = k_all[T:]
        v_ref[j] = v_all[T:]


def _swa_sample(proj, k_cache, v_cache, gq, gk, sinks, batch, seq_len):
    nb = SWA_SAMPLE_BATCH
    W = WINDOW
    width = SWA_Q + 2 * SWA_KV
    p3 = proj.reshape(batch, seq_len, width)
    nbytes = 2 * (nb * seq_len * (width + SWA_Q) * 4 + 4 * nb * W * SWA_KV * 4)
    o, k, v = pl.pallas_call(
        _swa_sample_kernel,
        grid=(batch // nb,),
        in_specs=[pl.BlockSpec(memory_space=pltpu.SMEM),
                  pl.BlockSpec((nb, seq_len, width), lambda b: (b, 0, 0)),
                  pl.BlockSpec((nb, W, SWA_KV), lambda b: (b, 0, 0)),
                  pl.BlockSpec((nb, W, SWA_KV), lambda b: (b, 0, 0)),
                  pl.BlockSpec((1, SWA_KV), lambda b: (0, 0)),
                  pl.BlockSpec((1, SWA_KV), lambda b: (0, 0)),
                  pl.BlockSpec((SWA_KV, SWA_KV), lambda b: (0, 0))],
        out_specs=[pl.BlockSpec((nb, seq_len, SWA_Q), lambda b: (b, 0, 0)),
                   pl.BlockSpec((nb, W, SWA_KV), lambda b: (b, 0, 0)),
                   pl.BlockSpec((nb, W, SWA_KV), lambda b: (b, 0, 0))],
        out_shape=[jax.ShapeDtypeStruct((batch, seq_len, SWA_Q), F32),
                   jax.ShapeDtypeStruct((batch, W, SWA_KV), F32),
                   jax.ShapeDtypeStruct((batch, W, SWA_KV), F32)],
        compiler_params=_params(("parallel",), nbytes),
        name="swa_sample",
    )(sinks, p3, k_cache, v_cache, jnp.tile(gq, SWA_KV_HEADS).reshape(1, SWA_KV),
      jnp.tile(gk, SWA_KV_HEADS).reshape(1, SWA_KV), _swa_gsum())
    return o.reshape(batch * seq_len, SWA_Q), k, v


def _lru_conv(x, shifted, cw_ref, cb_ref):
    y = cb_ref[...] + cw_ref[CONV_WIDTH - 1:CONV_WIDTH] * x
    for s in range(1, CONV_WIDTH):
        y = y + cw_ref[CONV_WIDTH - 1 - s:CONV_WIDTH - s] * shifted[s - 1]
    return y


def _block_diag_dot(x, w_ref):
    xb = x.astype(BF16)
    return jnp.concatenate(
        [jnp.dot(xb[:, n * LRU_BLOCK:(n + 1) * LRU_BLOCK], w_ref[n], preferred_element_type=F32)
         for n in range(LRU_BLOCKS)], axis=1)


def _lru_terms(xc, wga_ref, bga_ref, wgx_ref, bgx_ref, lam_ref):
    r = _sigmoid(_block_diag_dot(xc, wga_ref) + bga_ref[...])
    i = _sigmoid(_block_diag_dot(xc, wgx_ref) + bgx_ref[...])
    log_a = (-LRU_C) * r * _softplus(-lam_ref[...])
    a = jnp.exp(log_a)
    y2 = 2.0 * log_a
    u = a * a
    em1 = jnp.where(u == 1.0, y2, jnp.where(u == 0.0, -1.0, (u - 1.0) * y2 / jnp.log(u)))
    mult = jnp.sqrt(-em1)
    return a, mult * i * xc


def _scan_rows(a, b, group):
    rows = a.shape[0]
    pos = lax.broadcasted_iota(jnp.int32, a.shape, 0) % group
    d = 1
    while d < group:
        keep = pos >= d
        b = jnp.where(keep, a * pltpu.roll(b, d, 0) + b, b)
        a = jnp.where(keep, a * pltpu.roll(a, d, 0), a)
        d *= 2
    return a, b


def _lru_prompt_kernel(p_ref, cw_ref, cb_ref, wga_ref, bga_ref, wgx_ref, bgx_ref, lam_ref,
                       o_ref, h_ref, tail_ref, hc_ref):
    n = pl.program_id(1)
    rows = p_ref.shape[1]

    @pl.when(n == 0)
    def _():
        tail_ref[...] = jnp.zeros_like(tail_ref)
        hc_ref[...] = jnp.zeros_like(hc_ref)

    y = p_ref[0, :, :D_RNN]
    x = p_ref[0, :, D_RNN:]
    tail = tail_ref[...]
    r8 = lax.broadcasted_iota(jnp.int32, (V7X_SUBLANES, D_RNN), 0)
    shifted = []
    for s in range(1, CONV_WIDTH):
        xs = pltpu.roll(x, s, 0)
        head = jnp.where(r8 < s, pltpu.roll(tail, s, 0), xs[:V7X_SUBLANES])
        shifted.append(jnp.concatenate([head, xs[V7X_SUBLANES:]], axis=0))
    xc = _lru_conv(x, shifted, cw_ref, cb_ref)
    a, bterm = _lru_terms(xc, wga_ref, bga_ref, wgx_ref, bgx_ref, lam_ref)
    acum, hzero = _scan_rows(a, bterm, rows)
    hs = acum * hc_ref[0:1] + hzero
    o_ref[0] = _gelu_tanh(y) * hs
    last = hs[rows - 1:rows]
    h_ref[0] = last
    hc_ref[...] = jnp.broadcast_to(last, hc_ref.shape)
    tail_ref[...] = x[rows - V7X_SUBLANES:]


LRU_TILE = 256


def _lru_weight_specs(imap):
    return [pl.BlockSpec((CONV_WIDTH, D_RNN), imap(2)),
            pl.BlockSpec((1, D_RNN), imap(2)),
            pl.BlockSpec((LRU_BLOCKS, LRU_BLOCK, LRU_BLOCK), imap(3)),
            pl.BlockSpec((1, D_RNN), imap(2)),
            pl.BlockSpec((LRU_BLOCKS, LRU_BLOCK, LRU_BLOCK), imap(3)),
            pl.BlockSpec((1, D_RNN), imap(2)),
            pl.BlockSpec((1, D_RNN), imap(2))]


def _lru_prompt(proj, cw, cb, wga, bga, wgx, bgx, lam, batch, seq_len):
    R = LRU_TILE
    p3 = proj.reshape(batch, seq_len, 2 * D_RNN)
    nbytes = 2 * (R * 3 * D_RNN * 4 + 2 * LRU_BLOCKS * LRU_BLOCK * LRU_BLOCK * 2) + 24 * R * D_RNN * 4
    o, h = pl.pallas_call(
        _lru_prompt_kernel,
        grid=(batch, seq_len // R),
        in_specs=[pl.BlockSpec((1, R, 2 * D_RNN), lambda b, n: (b, n, 0))]
        + _lru_weight_specs(lambda nd: (lambda b, n: (0,) * nd)),
        out_specs=[pl.BlockSpec((1, R, D_RNN), lambda b, n: (b, n, 0)),
                   pl.BlockSpec((1, 1, D_RNN), lambda b, n: (b, 0, 0))],
        out_shape=[jax.ShapeDtypeStruct((batch, seq_len, D_RNN), F32),
                   jax.ShapeDtypeStruct((batch, 1, D_RNN), F32)],
        scratch_shapes=[pltpu.VMEM((V7X_SUBLANES, D_RNN), F32), pltpu.VMEM((V7X_SUBLANES, D_RNN), F32)],
        compiler_params=_params(("parallel", "arbitrary"), nbytes),
        name="lru_prompt",
    )(p3, cw, cb.reshape(1, D_RNN), wga.astype(BF16), bga.reshape(1, D_RNN), wgx.astype(BF16),
      bgx.reshape(1, D_RNN), lam.reshape(1, D_RNN))
    return o.reshape(batch * seq_len, D_RNN), h.reshape(batch, D_RNN)


def _lru_sample_kernel(p_ref, prev_ref, h0_ref, cw_ref, cb_ref, wga_ref, bga_ref, wgx_ref, bgx_ref, lam_ref,
                       o_ref, hs_ref, *, seq_len):
    rows = p_ref.shape[0]
    y = p_ref[:, :D_RNN]
    x = p_ref[:, D_RNN:]
    prev = prev_ref[...]
    pos = lax.broadcasted_iota(jnp.int32, (rows, D_RNN), 0) % seq_len
    shifted = [jnp.where(pos < s, pltpu.roll(prev, rows - seq_len + s, 0), pltpu.roll(x, s, 0))
               for s in range(1, CONV_WIDTH)]
    xc = _lru_conv(x, shifted, cw_ref, cb_ref)
    a, bterm = _lru_terms(xc, wga_ref, bga_ref, wgx_ref, bgx_ref, lam_ref)
    acum, hzero = _scan_rows(a, bterm, seq_len)
    hs = acum * h0_ref[...] + hzero
    o_ref[...] = _gelu_tanh(y) * hs
    hs_ref[...] = hs


def _lru_sample(proj, conv_state, h0, cw, cb, wga, bga, wgx, bgx, lam, batch, seq_len):
    assert seq_len == V7X_SUBLANES
    n = batch * seq_len
    R = LRU_TILE
    prev = jnp.pad(conv_state, ((0, 0), (seq_len - (CONV_WIDTH - 1), 0), (0, 0))).reshape(n, D_RNN)
    h0_rows = jnp.repeat(h0, seq_len, axis=0)
    nbytes = 2 * (R * 6 * D_RNN * 4 + 2 * LRU_BLOCKS * LRU_BLOCK * LRU_BLOCK * 2) + 24 * R * D_RNN * 4
    return pl.pallas_call(
        functools.partial(_lru_sample_kernel, seq_len=seq_len),
        grid=(n // R,),
        in_specs=[pl.BlockSpec((R, 2 * D_RNN), lambda i: (i, 0)),
                  pl.BlockSpec((R, D_RNN), lambda i: (i, 0)),
                  pl.BlockSpec((R, D_RNN), lambda i: (i, 0))]
        + _lru_weight_specs(lambda nd: (lambda i: (0,) * nd)),
        out_specs=[pl.BlockSpec((R, D_RNN), lambda i: (i, 0)),
                   pl.BlockSpec((R, D_RNN), lambda i: (i, 0))],
        out_shape=[jax.ShapeDtypeStruct((n, D_RNN), F32), jax.ShapeDtypeStruct((n, D_RNN), F32)],
        compiler_params=_params(("parallel",), nbytes),
        name="lru_sample",
    )(proj, prev, h0_rows, cw, cb.reshape(1, D_RNN), wga.astype(BF16), bga.reshape(1, D_RNN), wgx.astype(BF16),
      bgx.reshape(1, D_RNN), lam.reshape(1, D_RNN))


def _topk_rows(s, k):
    n = s.shape[0]
    rid = lax.broadcasted_iota(jnp.int32, s.shape, 0).astype(F32)
    vals, ids = [], []
    for _ in range(k):
        m = jnp.max(s, axis=0, keepdims=True)
        ix = jnp.min(jnp.where(s == m, rid, float(n)), axis=0, keepdims=True)
        vals.append(m)
        ids.append(ix)
        s = jnp.where(rid == ix, -jnp.inf, s)
    return jnp.concatenate(vals, axis=0), jnp.concatenate(ids, axis=0).astype(jnp.int32)


def _staircase_candidates(s1, s2):
    K = s1.shape[0]
    sub = V7X_SUBLANES
    first_single = next(a for a in range(K) if K // (a + 1) == 1)
    assert (K - first_single) % sub == 0
    pieces, starts, at = [], [], 0
    for a in range(first_single):
        nb = K // (a + 1)
        rows = -(-nb // sub) * sub
        piece = s1[a:a + 1] + s2[:rows]
        if rows != nb:
            piece = jnp.where(lax.broadcasted_iota(jnp.int32, piece.shape, 0) < nb, piece, -jnp.inf)
        pieces.append(piece)
        starts.append(at)
        at += rows
    pieces.append(s1[first_single:] + s2[0:1])
    return jnp.concatenate(pieces, axis=0), starts, at


def _route_kernel(q_ref, sk_ref, e1_ref, e2_ref, g_ref):
    K = PEER_TOPK
    halves = []
    for p in range(2):
        qh = q_ref[:, p * PEER_HALF:(p + 1) * PEER_HALF].astype(BF16)
        st = lax.dot_general(sk_ref[0, p].astype(BF16), qh, (((1,), (1,)), ((), ())),
                             preferred_element_type=F32)
        halves.append(_topk_rows(st, K))
    (s1, i1), (s2, i2) = halves
    cand, starts, single_start = _staircase_candidates(s1, s2)
    top, ci = _topk_rows(cand, K)
    a_id = jnp.zeros_like(ci)
    group_start = jnp.zeros_like(ci)
    for a in range(1, len(starts)):
        a_id = jnp.where(ci >= starts[a], a, a_id)
        group_start = jnp.where(ci >= starts[a], starts[a], group_start)
    single = ci >= single_start
    a_id = jnp.where(single, len(starts) + ci - single_start, a_id)
    b_id = jnp.where(single, 0, ci - group_start)
    e1 = jnp.zeros_like(ci)
    e2 = jnp.zeros_like(ci)
    for a in range(K):
        e1 = jnp.where(a_id == a, i1[a:a + 1], e1)
        e2 = jnp.where(b_id == a, i2[a:a + 1], e2)
    e = jnp.exp(top - top[0:1])
    e1_ref[0] = e1
    e2_ref[0] = e2
    g_ref[0] = e / jnp.sum(e, axis=0, keepdims=True)


ROUTE_TILE = 256


def _peer_route(q, sub_keys):
    n = q.shape[0]
    tb = ROUTE_TILE
    spec = pl.BlockSpec((1, PEER_TOPK, tb), lambda i, h: (h, 0, i))
    nbytes = 2 * (tb * PEER_KEY_DIM * 4 + 2 * PEER_NKEYS * PEER_HALF * 4) + 16 * 2 * PEER_NKEYS * tb * 4
    return pl.pallas_call(
        _route_kernel,
        grid=(n // tb, PEER_HEADS),
        in_specs=[pl.BlockSpec((tb, PEER_KEY_DIM), lambda i, h: (i, h)),
                  pl.BlockSpec((1, 2, PEER_NKEYS, PEER_HALF), lambda i, h: (h, 0, 0, 0))],
        out_specs=[spec, spec, spec],
        out_shape=[jax.ShapeDtypeStruct((PEER_HEADS, PEER_TOPK, n), jnp.int32),
                   jax.ShapeDtypeStruct((PEER_HEADS, PEER_TOPK, n), jnp.int32),
                   jax.ShapeDtypeStruct((PEER_HEADS, PEER_TOPK, n), F32)],
        compiler_params=_params(("parallel", "parallel"), nbytes),
        name="peer_route",
    )(q, sub_keys)


EXPERT_TILE = 256
EXPERT_CHUNK = 2048
EXPERT_SUB = 512
GATE_ROW_PAD = V7X_SUBLANES
GATE_UNROLL = 16


def _expert_kernel(h_ref, e1_ref, e2_ref, g_ref, ut_ref, v_ref, x_ref, gt_ref, o_ref,
                   gate_ref, acc_ref, e1t_ref, e2t_ref, gtt_ref):
    c = pl.program_id(1)
    tb = h_ref.shape[0]
    NK = PEER_NKEYS
    stride = tb + GATE_ROW_PAD

    @pl.when(c == 0)
    def _():
        acc_ref[...] = jnp.zeros_like(acc_ref)
        e1t_ref[...] = jnp.transpose(e1_ref[...].reshape(PEER_PAIRS, tb))
        e2t_ref[...] = jnp.transpose(e2_ref[...].reshape(PEER_PAIRS, tb))
        gtt_ref[...] = jnp.transpose(g_ref[...].reshape(PEER_PAIRS, tb))
        kid = lax.broadcasted_iota(jnp.int32, (NK, PEER_PAIRS), 0)

        def per_token(n, carry):
            i1 = e1t_ref[pl.ds(n, 1), :]
            i2 = e2t_ref[pl.ds(n, 1), :]
            gg = gtt_ref[pl.ds(n, 1), :]
            a_t = jnp.where(kid == i1, 1.0, 0.0).astype(BF16)
            b_t = jnp.where(kid == i2, gg, 0.0).astype(BF16)
            gn = lax.dot_general(a_t, b_t, (((1,), (1,)), ((), ())), preferred_element_type=F32)
            gate_ref[pl.ds(n, NK, stride=stride), :] = gn
            return carry

        lax.fori_loop(0, tb, per_token, 0, unroll=GATE_UNROLL)

    h = h_ref[...]
    keys_per_chunk = EXPERT_CHUNK // NK
    for j in range(EXPERT_CHUNK // EXPERT_SUB):
        lo = j * EXPERT_SUB
        s = jnp.dot(h, ut_ref[:, lo:lo + EXPERT_SUB], preferred_element_type=F32)
        gsel = jnp.concatenate(
            [gate_ref[pl.ds(pl.multiple_of((c * keys_per_chunk + lo // NK + t) * stride, V7X_SUBLANES), tb), :]
             for t in range(EXPERT_SUB // NK)], axis=1)
        w = (_gelu_tanh(s) * gsel).astype(BF16)
        acc_ref[...] += jnp.dot(w, v_ref[lo:lo + EXPERT_SUB, :], preferred_element_type=F32)

    @pl.when(c == pl.num_programs(1) - 1)
    def _():
        o_ref[...] = x_ref[...] + gt_ref[0] * acc_ref[...]


def _peer_experts(h_bf16, e1, e2, g, ut_bf16, v_bf16, x, gate, seq_len):
    n, d = x.shape
    tb = EXPERT_TILE
    ec = EXPERT_CHUNK
    gt_arr, gt_spec0 = _row_operand(gate, seq_len, tb)
    gt_spec = pl.BlockSpec(gt_spec0.block_shape, lambda i, c: gt_spec0.index_map(i))
    rspec = pl.BlockSpec((PEER_HEADS, PEER_TOPK, tb), lambda i, c: (0, 0, i))
    nbytes = (2 * (tb * d * 2 + 3 * PEER_PAIRS * tb * 4 + 2 * d * ec * 2 + 3 * tb * d * 4)
              + tb * PEER_NKEYS * PEER_NKEYS * 4 + tb * d * 4 + 3 * tb * PEER_PAIRS * 4 + 8 * tb * EXPERT_SUB * 4)
    return pl.pallas_call(
        _expert_kernel,
        grid=(n // tb, PEER_EXPERTS // ec),
        in_specs=[pl.BlockSpec((tb, d), lambda i, c: (i, 0)),
                  rspec, rspec, rspec,
                  pl.BlockSpec((d, ec), lambda i, c: (0, c)),
                  pl.BlockSpec((ec, d), lambda i, c: (c, 0)),
                  pl.BlockSpec((tb, d), lambda i, c: (i, 0)),
                  gt_spec],
        out_specs=pl.BlockSpec((tb, d), lambda i, c: (i, 0)),
        out_shape=jax.ShapeDtypeStruct((n, d), F32),
        scratch_shapes=[pltpu.VMEM(((tb + GATE_ROW_PAD) * PEER_NKEYS, PEER_NKEYS), F32),
                        pltpu.VMEM((tb, d), F32),
                        pltpu.VMEM((tb, PEER_PAIRS), jnp.int32),
                        pltpu.VMEM((tb, PEER_PAIRS), jnp.int32),
                        pltpu.VMEM((tb, PEER_PAIRS), F32)],
        compiler_params=_params(("parallel", "arbitrary"), nbytes),
        name="peer_experts",
    )(h_bf16, e1, e2, g, ut_bf16, v_bf16, x, gt_arr)


def _swa_permute_in(w_in):
    d = w_in.shape[0]
    wq = w_in[:, :SWA_Q].reshape(d, SWA_KV_HEADS, SWA_GROUP, SWA_HEAD_DIM).transpose(0, 2, 1, 3).reshape(d, SWA_Q)
    return jnp.concatenate([wq, w_in[:, SWA_Q:]], axis=1)


def _swa_permute_out(w_out):
    d = w_out.shape[1]
    return w_out.reshape(SWA_KV_HEADS, SWA_GROUP, SWA_HEAD_DIM, d).transpose(1, 0, 2, 3).reshape(SWA_Q, d)


def _prepare_weights(p):
    w = {}
    w['gla_in'] = [jnp.pad(p['w_gla_in'][j], ((0, 0), (0, GLA_IN_PAD - p['w_gla_in'].shape[2]))).astype(BF16)
                   for j in range(p['w_gla_in'].shape[0])]
    w['gla_a2'] = [jnp.pad(p['w_gla_a2'][j], ((0, V7X_LANES - GLA_GATE_RANK), (0, 0)))
                   for j in range(p['w_gla_a2'].shape[0])]
    w['gla_out'] = [m.astype(BF16) for m in p['w_gla_out']]
    w['swa_in'] = [_swa_permute_in(m).astype(BF16) for m in p['w_swa_in']]
    w['swa_out'] = [_swa_permute_out(m).astype(BF16) for m in p['w_swa_out']]
    w['lru_in'] = [m.astype(BF16) for m in p['w_lru_in']]
    w['lru_out'] = [m.astype(BF16) for m in p['w_lru_out']]
    w['peer_q'] = [m.astype(BF16) for m in p['w_peer_q']]
    w['peer_ut'] = [m.T.astype(BF16) for m in p['peer_u']]
    w['peer_v'] = [m.astype(BF16) for m in p['peer_v']]
    return w


def _trunk(x3, mod, states, p, w):
    batch, seq_len, d = x3.shape
    x = x3.reshape(batch * seq_len, d)
    new_gla, new_k, new_v, new_conv, new_h = [], [], [], [], []
    for i in range(DEPTH):
        kind, j = i % N_MIXERS, i // N_MIXERS
        sh_m, sc_m, gt_m, sh_f, sc_f, gt_f = [mod[i][:, k * d:(k + 1) * d] for k in range(6)]
        if kind == 0:
            proj = _norm_proj(x, p['g_ln_mix'][i], sc_m, sh_m, w['gla_in'][j], seq_len)
            if states is None:
                mix, s_new = _gla_prompt(proj, w['gla_a2'][j], p['b_gla_a'][j], p['g_gla_norm'][j], batch, seq_len)
            else:
                mix, s_new = _gla_sample(proj, states[0][j], w['gla_a2'][j], p['b_gla_a'][j], p['g_gla_norm'][j],
                                         batch, seq_len)
            new_gla.append(s_new)
            w_out = w['gla_out'][j]
        elif kind == 1:
            proj = _norm_proj(x, p['g_ln_mix'][i], sc_m, sh_m, w['swa_in'][j], seq_len)
            if states is None:
                mix, k_n, v_n = _swa_prompt(proj, p['g_swa_q'][j], p['g_swa_k'][j], p['swa_sinks'][j], batch, seq_len)
            else:
                kc = states[1][j].reshape(batch, WINDOW, SWA_KV)
                vc = states[2][j].reshape(batch, WINDOW, SWA_KV)
                mix, k_n, v_n = _swa_sample(proj, kc, vc, p['g_swa_q'][j], p['g_swa_k'][j], p['swa_sinks'][j],
                                            batch, seq_len)
            new_k.append(k_n.reshape(batch, WINDOW, SWA_KV_HEADS, SWA_HEAD_DIM))
            new_v.append(v_n.reshape(batch, WINDOW, SWA_KV_HEADS, SWA_HEAD_DIM))
            w_out = w['swa_out'][j]
        else:
            proj = _norm_proj(x, p['g_ln_mix'][i], sc_m, sh_m, w['lru_in'][j], seq_len)
            lru_args = (p['lru_conv_w'][j], p['lru_conv_b'][j], p['w_lru_ga'][j], p['b_lru_ga'][j],
                        p['w_lru_gx'][j], p['b_lru_gx'][j], p['lru_lam'][j], batch, seq_len)
            assert seq_len >= CONV_WIDTH - 1
            if states is None:
                mix, h_n = _lru_prompt(proj, *lru_args)
            else:
                mix, hs = _lru_sample(proj, states[3][j], states[4][j], *lru_args)
                h_n = hs.reshape(batch, seq_len, D_RNN)[:, -1]
            new_conv.append(proj[:, D_RNN:].reshape(batch, seq_len, D_RNN)[:, seq_len - (CONV_WIDTH - 1):])
            new_h.append(h_n)
            w_out = w['lru_out'][j]
        x = _proj_residual(mix, w_out, x, gt_m, seq_len)
        q, hb = _norm_proj(x, p['g_ln_ffn'][i], sc_f, sh_f, w['peer_q'][i], seq_len, with_h=True)
        e1, e2, g = _peer_route(q, p['peer_sub_keys'][i])
        x = _peer_experts(hb, e1, e2, g, w['peer_ut'][i], w['peer_v'][i], x, gt_f, seq_len)
    y = x.reshape(batch, seq_len, d)
    return y, (jnp.stack(new_gla), jnp.stack(new_k), jnp.stack(new_v), jnp.stack(new_conv), jnp.stack(new_h))


def kernel(x_prompt, x_sample, state_gla, cache_swa_k, cache_swa_v, state_lru_conv, state_lru_h,
           c_prompt, c_sample, g_ln_mix, g_ln_ffn, w_mod, b_mod,
           w_gla_in, w_gla_a2, b_gla_a, g_gla_norm, w_gla_out,
           w_swa_in, g_swa_q, g_swa_k, swa_sinks, w_swa_out,
           w_lru_in, lru_conv_w, lru_conv_b, w_lru_ga, b_lru_ga, w_lru_gx, b_lru_gx, lru_lam, w_lru_out,
           w_peer_q, peer_sub_keys, peer_u, peer_v):
    p = {'g_ln_mix': g_ln_mix, 'g_ln_ffn': g_ln_ffn,
         'w_gla_in': w_gla_in, 'w_gla_a2': w_gla_a2, 'b_gla_a': b_gla_a, 'g_gla_norm': g_gla_norm,
         'w_gla_out': w_gla_out,
         'w_swa_in': w_swa_in, 'g_swa_q': g_swa_q, 'g_swa_k': g_swa_k, 'swa_sinks': swa_sinks,
         'w_swa_out': w_swa_out,
         'w_lru_in': w_lru_in, 'lru_conv_w': lru_conv_w, 'lru_conv_b': lru_conv_b,
         'w_lru_ga': w_lru_ga, 'b_lru_ga': b_lru_ga, 'w_lru_gx': w_lru_gx, 'b_lru_gx': b_lru_gx,
         'lru_lam': lru_lam, 'w_lru_out': w_lru_out,
         'w_peer_q': w_peer_q, 'peer_sub_keys': peer_sub_keys, 'peer_u': peer_u, 'peer_v': peer_v}
    w = _prepare_weights(p)
    nb_p, nb_s = c_prompt.shape[0], c_sample.shape[0]
    rows = -(-(nb_p + nb_s) // V7X_SUBLANES) * V7X_SUBLANES
    c_all = jnp.pad(jnp.concatenate([c_prompt, c_sample], axis=0), ((0, rows - nb_p - nb_s), (0, 0)))
    mod = _modulation(c_all, w_mod, b_mod)
    y_p, (gla_p, k_p, v_p, conv_p, h_p) = _trunk(x_prompt, mod[:, :nb_p], None, p, w)
    y_s, (gla_s, k_s, v_s, conv_s, h_s) = _trunk(
        x_sample, mod[:, nb_p:nb_p + nb_s],
        (state_gla, cache_swa_k, cache_swa_v, state_lru_conv, state_lru_h), p, w)
    return (y_p, y_s, gla_p, gla_s, k_p, k_s, v_p, v_s, conv_p, conv_s, h_p, h_s)
```

```python
import functools
import math

import jax
import jax.numpy as jnp
from jax import lax
from jax.experimental import pallas as pl
from jax.experimental.pallas import tpu as pltpu

F32 = jnp.float32
BF16 = jnp.bfloat16

D_MODEL = 1024
DEPTH = 4
N_MIXERS = 3
RMS_EPS = 1e-6

GLA_HEADS = 4
GLA_QK = D_MODEL // 2
GLA_V = D_MODEL
GLA_DK = GLA_QK // GLA_HEADS
GLA_DV = GLA_V // GLA_HEADS
GLA_GATE_RANK = 16
GLA_TAU = 16.0
GLA_SUB = 16
GLA_CHUNK = 128
GLA_IN_PAD = 2 * GLA_QK + 2 * GLA_V + 128

SWA_HEAD_DIM = 64
SWA_Q_HEADS = D_MODEL // SWA_HEAD_DIM
SWA_KV_HEADS = 4
SWA_GROUP = SWA_Q_HEADS // SWA_KV_HEADS
SWA_Q = SWA_Q_HEADS * SWA_HEAD_DIM
SWA_KV = SWA_KV_HEADS * SWA_HEAD_DIM
WINDOW = 128

D_RNN = D_MODEL
LRU_BLOCKS = 4
LRU_BLOCK = D_RNN // LRU_BLOCKS
CONV_WIDTH = 4
LRU_C = 8.0

PEER_HEADS = 8
PEER_NKEYS = 128
PEER_EXPERTS = PEER_NKEYS * PEER_NKEYS
PEER_KEY_DIM = 256
PEER_HALF = PEER_KEY_DIM // 2
PEER_TOPK = 16
PEER_PAIRS = PEER_HEADS * PEER_TOPK

V7X_LANES = 128
V7X_SUBLANES = 8
V7X_VMEM_BYTES = 64 * 1024 * 1024

TOKEN_TILE = 256
NEG_BIG = -1e30


def _vmem_limit(nbytes):
    return int(min(max(nbytes * 3 // 2, 16 * 1024 * 1024), V7X_VMEM_BYTES - 8 * 1024 * 1024))


def _params(semantics, nbytes):
    return pltpu.CompilerParams(dimension_semantics=semantics, vmem_limit_bytes=_vmem_limit(nbytes))


def _rms(x, g):
    return x * lax.rsqrt(jnp.mean(x * x, axis=-1, keepdims=True) + RMS_EPS) * g


def _gelu_tanh(x):
    return 0.5 * x * (1.0 + jnp.tanh(math.sqrt(2.0 / math.pi) * (x + 0.044715 * (x * x * x))))


def _gelu_times_half_gate(x, half_gate):
    c1 = math.sqrt(2.0 / math.pi)
    inner = x * (c1 + (c1 * 0.044715) * (x * x))
    return (x * half_gate) * (1.0 + jnp.tanh(inner))


def _sigmoid(x):
    return 1.0 / (1.0 + jnp.exp(-x))


def _softplus(x):
    return jnp.maximum(x, 0.0) + jnp.log1p(jnp.exp(-jnp.abs(x)))


def _row_operand(vec, seq_len, tile):
    b, d = vec.shape
    if seq_len % tile == 0:
        per_seq = seq_len // tile
        return vec.reshape(b, 1, d), pl.BlockSpec((1, 1, d), lambda i: (i // per_seq, 0, 0))
    assert tile % seq_len == 0
    rep = jnp.repeat(vec, seq_len, axis=0).reshape(b * seq_len // tile, tile, d)
    return rep, pl.BlockSpec((1, tile, d), lambda i: (i, 0, 0))


def _mod_kernel(c_ref, w_ref, b_ref, o_ref):
    c = c_ref[...]
    sc = c * _sigmoid(c)
    o_ref[0] = jnp.dot(sc, w_ref[0], preferred_element_type=F32) + b_ref[0]


def _modulation(c, w_mod, b_mod):
    bp, d = c.shape
    tn = 1024
    nt = 6 * d // tn
    return pl.pallas_call(
        _mod_kernel,
        grid=(DEPTH, nt),
        in_specs=[pl.BlockSpec((bp, d), lambda l, j: (0, 0)),
                  pl.BlockSpec((1, d, tn), lambda l, j: (l, 0, j)),
                  pl.BlockSpec((1, 1, tn), lambda l, j: (l, 0, j))],
        out_specs=pl.BlockSpec((1, bp, tn), lambda l, j: (l, 0, j)),
        out_shape=jax.ShapeDtypeStruct((DEPTH, bp, 6 * d), F32),
        compiler_params=_params(("parallel", "parallel"), 2 * (d * tn * 4 + 2 * bp * tn * 4)),
        name="modulation",
    )(c, w_mod, b_mod.reshape(DEPTH, 1, 6 * d))


def _norm_proj_kernel(x_ref, g_ref, sc_ref, sh_ref, w_ref, o_ref, *h_ref):
    h = _rms(x_ref[...], g_ref[...]) * (1.0 + sc_ref[0]) + sh_ref[0]
    hb = h.astype(BF16)
    o_ref[...] = jnp.dot(hb, w_ref[...], preferred_element_type=F32)
    if h_ref:
        h_ref[0][...] = hb


def _norm_proj(x, g, scale, shift, w_bf16, seq_len, with_h=False):
    n, d = x.shape
    nout = w_bf16.shape[1]
    tm = TOKEN_TILE
    sc_arr, sc_spec = _row_operand(scale, seq_len, tm)
    sh_arr, sh_spec = _row_operand(shift, seq_len, tm)
    out_shape = [jax.ShapeDtypeStruct((n, nout), F32)]
    out_specs = [pl.BlockSpec((tm, nout), lambda i: (i, 0))]
    if with_h:
        out_shape.append(jax.ShapeDtypeStruct((n, d), BF16))
        out_specs.append(pl.BlockSpec((tm, d), lambda i: (i, 0)))
    nbytes = 2 * (tm * d * 4 + d * nout * 2 + tm * nout * 4 + 3 * tm * d * 4)
    res = pl.pallas_call(
        _norm_proj_kernel,
        grid=(n // tm,),
        in_specs=[pl.BlockSpec((tm, d), lambda i: (i, 0)),
                  pl.BlockSpec((1, d), lambda i: (0, 0)),
                  sc_spec, sh_spec,
                  pl.BlockSpec((d, nout), lambda i: (0, 0))],
        out_specs=out_specs,
        out_shape=out_shape,
        compiler_params=_params(("parallel",), nbytes),
        name="norm_proj",
    )(x, g.reshape(1, d), sc_arr, sh_arr, w_bf16)
    return res if with_h else res[0]


def _proj_residual_kernel(a_ref, w_ref, x_ref, gt_ref, o_ref):
    y = jnp.dot(a_ref[...].astype(BF16), w_ref[...], preferred_element_type=F32)
    o_ref[...] = x_ref[...] + gt_ref[0] * y


def _proj_residual(a, w_bf16, x, gate, seq_len):
    n, k = a.shape
    d = x.shape[1]
    tm = TOKEN_TILE
    gt_arr, gt_spec = _row_operand(gate, seq_len, tm)
    nbytes = 2 * (tm * k * 4 + k * d * 2 + 3 * tm * d * 4)
    return pl.pallas_call(
        _proj_residual_kernel,
        grid=(n // tm,),
        in_specs=[pl.BlockSpec((tm, k), lambda i: (i, 0)),
                  pl.BlockSpec((k, d), lambda i: (0, 0)),
                  pl.BlockSpec((tm, d), lambda i: (i, 0)),
                  gt_spec],
        out_specs=pl.BlockSpec((tm, d), lambda i: (i, 0)),
        out_shape=jax.ShapeDtypeStruct((n, d), F32),
        compiler_params=_params(("parallel",), nbytes),
        name="proj_residual",
    )(a, w_bf16, x, gt_arr)


def _log_decay(lr, wa2, ba):
    z = jnp.dot(lr, wa2, preferred_element_type=F32) + ba
    return (jnp.minimum(z, 0.0) - jnp.log1p(jnp.exp(-jnp.abs(z)))) * (1.0 / GLA_TAU)


def _col_bcast(row):
    return jnp.transpose(jnp.broadcast_to(row, (V7X_LANES, V7X_LANES)))


def _head_out(o, gate, gn):
    return _rms(o, gn) * (gate * _sigmoid(gate))


def _gla_prompt_kernel(p_ref, wa2_ref, ba_ref, gn_ref, o_ref, s_ref):
    c = pl.program_id(1)
    C = GLA_CHUNK
    nsub = C // GLA_SUB

    @pl.when(c == 0)
    def _():
        s_ref[...] = jnp.zeros_like(s_ref)

    row = lax.broadcasted_iota(jnp.int32, (C, C), 0)
    col = lax.broadcasted_iota(jnp.int32, (C, C), 1)
    tri = (col <= row).astype(F32)
    later = ((col > row) & (col // GLA_SUB == row // GLA_SUB)).astype(F32)
    sums = jnp.concatenate([tri, later], axis=0)
    causal = col <= row
    rsub = lax.broadcasted_iota(jnp.int32, (C, GLA_DK), 0) // GLA_SUB

    lr = p_ref[0, :, 2 * GLA_QK + 2 * GLA_V:]
    for h in range(GLA_HEADS):
        q = p_ref[0, :, h * GLA_DK:(h + 1) * GLA_DK] * (GLA_DK ** -0.5)
        k = p_ref[0, :, GLA_QK + h * GLA_DK:GLA_QK + (h + 1) * GLA_DK]
        v = p_ref[0, :, 2 * GLA_QK + h * GLA_DV:2 * GLA_QK + (h + 1) * GLA_DV]
        gate = p_ref[0, :, 2 * GLA_QK + GLA_V + h * GLA_DV:2 * GLA_QK + GLA_V + (h + 1) * GLA_DV]
        la = _log_decay(lr, wa2_ref[:, h * GLA_DK:(h + 1) * GLA_DK], ba_ref[:, h * GLA_DK:(h + 1) * GLA_DK])
        cs = jnp.dot(sums, la, preferred_element_type=F32, precision=lax.Precision.HIGHEST)
        b = cs[:C]
        to_sub_end = cs[C:]
        b_last = b[C - 1:C]
        k_sub = k * jnp.exp(to_sub_end)
        q_parts, k_parts = [], []
        for m in range(nsub):
            ref_row = b[m * GLA_SUB + GLA_SUB - 1:m * GLA_SUB + GLA_SUB]
            e = jnp.where(rsub >= m, b - ref_row, NEG_BIG)
            q_parts.append((q * jnp.exp(e)).astype(BF16))
            k_parts.append(jnp.where(rsub == m, k_sub, 0.0).astype(BF16))
        qcat = jnp.concatenate(q_parts, axis=1)
        kcat = jnp.concatenate(k_parts, axis=1)
        att = lax.dot_general(qcat, kcat, (((1,), (1,)), ((), ())), preferred_element_type=F32)
        att = jnp.where(causal, att, 0.0)
        s_old = s_ref[0, h]
        o = jnp.dot(att.astype(BF16), v.astype(BF16), preferred_element_type=F32)
        o = o + jnp.dot((q * jnp.exp(b)).astype(BF16), s_old.astype(BF16), preferred_element_type=F32)
        k_end = (k * jnp.exp(b_last - b)).astype(BF16)
        upd = lax.dot_general(k_end, v.astype(BF16), (((0,), (0,)), ((), ())), preferred_element_type=F32)
        decay = _col_bcast(jnp.exp(b_last))
        s_ref[0, h] = jnp.concatenate([decay] * (GLA_DV // V7X_LANES), axis=1) * s_old + upd
        o_ref[0, :, h * GLA_DV:(h + 1) * GLA_DV] = _head_out(o, gate, gn_ref[...])


def _gla_prompt(proj, wa2_pad, ba, gn, batch, seq_len):
    C = GLA_CHUNK
    p3 = proj.reshape(batch, seq_len, GLA_IN_PAD)
    nbytes = 2 * (C * GLA_IN_PAD * 4 + C * GLA_V * 4 + GLA_HEADS * GLA_DK * GLA_DV * 4) + 64 * C * C * 4
    o, s = pl.pallas_call(
        _gla_prompt_kernel,
        grid=(batch, seq_len // C),
        in_specs=[pl.BlockSpec((1, C, GLA_IN_PAD), lambda b, c: (b, c, 0)),
                  pl.BlockSpec((V7X_LANES, GLA_QK), lambda b, c: (0, 0)),
                  pl.BlockSpec((1, GLA_QK), lambda b, c: (0, 0)),
                  pl.BlockSpec((1, GLA_DV), lambda b, c: (0, 0))],
        out_specs=[pl.BlockSpec((1, C, GLA_V), lambda b, c: (b, c, 0)),
                   pl.BlockSpec((1, GLA_HEADS, GLA_DK, GLA_DV), lambda b, c: (b, 0, 0, 0))],
        out_shape=[jax.ShapeDtypeStruct((batch, seq_len, GLA_V), F32),
                   jax.ShapeDtypeStruct((batch, GLA_HEADS, GLA_DK, GLA_DV), F32)],
        compiler_params=_params(("parallel", "arbitrary"), nbytes),
        name="gla_prompt",
    )(p3, wa2_pad, ba.reshape(1, GLA_QK), gn.reshape(1, GLA_DV))
    return o.reshape(batch * seq_len, GLA_V), s


GLA_SAMPLE_BATCH = 8


def _gla_sample_kernel(p_ref, s0_ref, wa2_ref, ba_ref, gn_ref, o_ref, s_ref):
    T = p_ref.shape[1]
    row = lax.broadcasted_iota(jnp.int32, (T, T), 0)
    col = lax.broadcasted_iota(jnp.int32, (T, T), 1)
    causal = col <= row
    tri = causal.astype(F32)
    for j in range(p_ref.shape[0]):
        lr = p_ref[j, :, 2 * GLA_QK + 2 * GLA_V:]
        for h in range(GLA_HEADS):
            q = p_ref[j, :, h * GLA_DK:(h + 1) * GLA_DK] * (GLA_DK ** -0.5)
            k = p_ref[j, :, GLA_QK + h * GLA_DK:GLA_QK + (h + 1) * GLA_DK]
            v = p_ref[j, :, 2 * GLA_QK + h * GLA_DV:2 * GLA_QK + (h + 1) * GLA_DV]
            gate = p_ref[j, :, 2 * GLA_QK + GLA_V + h * GLA_DV:2 * GLA_QK + GLA_V + (h + 1) * GLA_DV]
            la = _log_decay(lr, wa2_ref[:, h * GLA_DK:(h + 1) * GLA_DK], ba_ref[:, h * GLA_DK:(h + 1) * GLA_DK])
            b = jnp.dot(tri, la, preferred_element_type=F32, precision=lax.Precision.HIGHEST)
            b_last = b[T - 1:T]
            k_end = (k * jnp.exp(b_last - b)).astype(BF16)
            q_rel = (q * jnp.exp(b - b_last)).astype(BF16)
            att = lax.dot_general(q_rel, k_end, (((1,), (1,)), ((), ())), preferred_element_type=F32)
            att = jnp.where(causal, att, 0.0)
            s_old = s0_ref[j, h]
            vb = v.astype(BF16)
            o = jnp.dot(att.astype(BF16), vb, preferred_element_type=F32)
            o = o + jnp.dot((q * jnp.exp(b)).astype(BF16), s_old.astype(BF16), preferred_element_type=F32)
            upd = lax.dot_general(k_end, vb, (((0,), (0,)), ((), ())), preferred_element_type=F32)
            decay = _col_bcast(jnp.exp(b_last))
            s_ref[j, h] = jnp.concatenate([decay] * (GLA_DV // V7X_LANES), axis=1) * s_old + upd
            o_ref[j, :, h * GLA_DV:(h + 1) * GLA_DV] = _head_out(o, gate, gn_ref[...])


def _gla_sample(proj, s0, wa2_pad, ba, gn, batch, seq_len):
    assert seq_len <= GLA_SUB
    nb = GLA_SAMPLE_BATCH
    p3 = proj.reshape(batch, seq_len, GLA_IN_PAD)
    state_block = nb * GLA_HEADS * GLA_DK * GLA_DV * 4
    nbytes = 2 * (nb * seq_len * (GLA_IN_PAD + GLA_V) * 4 + 2 * state_block)
    o, s = pl.pallas_call(
        _gla_sample_kernel,
        grid=(batch // nb,),
        in_specs=[pl.BlockSpec((nb, seq_len, GLA_IN_PAD), lambda b: (b, 0, 0)),
                  pl.BlockSpec((nb, GLA_HEADS, GLA_DK, GLA_DV), lambda b: (b, 0, 0, 0)),
                  pl.BlockSpec((V7X_LANES, GLA_QK), lambda b: (0, 0)),
                  pl.BlockSpec((1, GLA_QK), lambda b: (0, 0)),
                  pl.BlockSpec((1, GLA_DV), lambda b: (0, 0))],
        out_specs=[pl.BlockSpec((nb, seq_len, GLA_V), lambda b: (b, 0, 0)),
                   pl.BlockSpec((nb, GLA_HEADS, GLA_DK, GLA_DV), lambda b: (b, 0, 0, 0))],
        out_shape=[jax.ShapeDtypeStruct((batch, seq_len, GLA_V), F32),
                   jax.ShapeDtypeStruct((batch, GLA_HEADS, GLA_DK, GLA_DV), F32)],
        compiler_params=_params(("parallel",), nbytes),
        name="gla_sample",
    )(p3, s0, wa2_pad, ba.reshape(1, GLA_QK), gn.reshape(1, GLA_DV))
    return o.reshape(batch * seq_len, GLA_V), s


def _head_group_norm(x, gain, gsum):
    sq = x * x
    hi = sq.astype(BF16)
    lo = (sq - hi.astype(F32)).astype(BF16)
    ms = (jnp.dot(hi, gsum, preferred_element_type=F32) + jnp.dot(lo, gsum, preferred_element_type=F32))
    return x * lax.rsqrt(ms * (1.0 / SWA_HEAD_DIM) + RMS_EPS) * gain


def _swa_attend(q_groups, k_all, v_all, mask, sink_ref):
    tq = q_groups[0].shape[0]
    lane_head = lax.broadcasted_iota(jnp.int32, (tq, SWA_KV), 1) // SWA_HEAD_DIM
    mask_rows = jnp.concatenate([mask] * SWA_GROUP, axis=0)
    out = [jnp.zeros((tq, SWA_KV), F32) for _ in range(SWA_GROUP)]
    for kv in range(SWA_KV_HEADS):
        in_head = lane_head == kv
        qs = jnp.concatenate([jnp.where(in_head, qg, 0.0) for qg in q_groups], axis=0).astype(BF16)
        s = lax.dot_general(qs, k_all, (((1,), (1,)), ((), ())), preferred_element_type=F32)
        s = jnp.where(mask_rows, s, -jnp.inf)
        sink = jnp.concatenate(
            [jnp.full((tq, 1), sink_ref[kv * SWA_GROUP + g], F32) for g in range(SWA_GROUP)], axis=0)
        m = jnp.maximum(jnp.max(s, axis=1, keepdims=True), sink)
        p = jnp.exp(s - m)
        denom = jnp.sum(p, axis=1, keepdims=True) + jnp.exp(sink - m)
        pv = jnp.dot(p.astype(BF16), v_all, preferred_element_type=F32) / denom
        for g in range(SWA_GROUP):
            out[g] = jnp.where(in_head, pv[g * tq:(g + 1) * tq], out[g])
    return out


def _swa_prompt_kernel(sink_ref, cur_ref, prev_ref, gq_ref, gk_ref, gsum_ref, o_ref, k_ref, v_ref):
    n = pl.program_id(1)
    gsum = gsum_ref[...]
    k_cur = _head_group_norm(cur_ref[0, :, SWA_Q:SWA_Q + SWA_KV], gk_ref[...], gsum)
    k_prev = _head_group_norm(prev_ref[0, :, SWA_Q:SWA_Q + SWA_KV], gk_ref[...], gsum)
    v_cur = cur_ref[0, :, SWA_Q + SWA_KV:]
    v_prev = prev_ref[0, :, SWA_Q + SWA_KV:]
    k_all = jnp.concatenate([k_prev, k_cur], axis=0).astype(BF16)
    v_all = jnp.concatenate([v_prev, v_cur], axis=0).astype(BF16)
    t = lax.broadcasted_iota(jnp.int32, (WINDOW, 2 * WINDOW), 0)
    s = lax.broadcasted_iota(jnp.int32, (WINDOW, 2 * WINDOW), 1)
    mask = (s >= t) & (s <= t + WINDOW) & ((s >= WINDOW) | (n > 0))
    q_groups = [_head_group_norm(cur_ref[0, :, g * SWA_KV:(g + 1) * SWA_KV], gq_ref[...], gsum)
                * (SWA_HEAD_DIM ** -0.5) for g in range(SWA_GROUP)]
    out = _swa_attend(q_groups, k_all, v_all, mask, sink_ref)
    o_ref[0] = jnp.concatenate(out, axis=1)
    k_ref[0] = k_cur
    v_ref[0] = v_cur


def _swa_gsum():
    head = jnp.arange(SWA_KV) // SWA_HEAD_DIM
    return (head[:, None] == head[None, :]).astype(BF16)


def _swa_prompt(proj, gq, gk, sinks, batch, seq_len):
    W = WINDOW
    width = SWA_Q + 2 * SWA_KV
    p3 = proj.reshape(batch, seq_len, width)
    nbytes = 2 * (2 * W * width * 4 + W * SWA_Q * 4 + 2 * W * SWA_KV * 4) + 48 * W * 2 * W * 4
    o, k, v = pl.pallas_call(
        _swa_prompt_kernel,
        grid=(batch, seq_len // W),
        in_specs=[pl.BlockSpec(memory_space=pltpu.SMEM),
                  pl.BlockSpec((1, W, width), lambda b, n: (b, n, 0)),
                  pl.BlockSpec((1, W, width), lambda b, n: (b, jnp.maximum(n - 1, 0), 0)),
                  pl.BlockSpec((1, SWA_KV), lambda b, n: (0, 0)),
                  pl.BlockSpec((1, SWA_KV), lambda b, n: (0, 0)),
                  pl.BlockSpec((SWA_KV, SWA_KV), lambda b, n: (0, 0))],
        out_specs=[pl.BlockSpec((1, W, SWA_Q), lambda b, n: (b, n, 0)),
                   pl.BlockSpec((1, W, SWA_KV), lambda b, n: (b, 0, 0)),
                   pl.BlockSpec((1, W, SWA_KV), lambda b, n: (b, 0, 0))],
        out_shape=[jax.ShapeDtypeStruct((batch, seq_len, SWA_Q), F32),
                   jax.ShapeDtypeStruct((batch, W, SWA_KV), F32),
                   jax.ShapeDtypeStruct((batch, W, SWA_KV), F32)],
        compiler_params=_params(("parallel", "arbitrary"), nbytes),
        name="swa_prompt",
    )(sinks, p3, p3, jnp.tile(gq, SWA_KV_HEADS).reshape(1, SWA_KV), jnp.tile(gk, SWA_KV_HEADS).reshape(1, SWA_KV),
      _swa_gsum())
    return o.reshape(batch * seq_len, SWA_Q), k, v


SWA_SAMPLE_BATCH = 8


def _swa_sample_kernel(sink_ref, p_ref, kc_ref, vc_ref, gq_ref, gk_ref, gsum_ref, o_ref, k_ref, v_ref):
    T = p_ref.shape[1]
    gsum = gsum_ref[...]
    t = lax.broadcasted_iota(jnp.int32, (T, WINDOW + T), 0)
    s = lax.broadcasted_iota(jnp.int32, (T, WINDOW + T), 1)
    mask = (s >= t) & (s <= t + WINDOW)
    for j in range(p_ref.shape[0]):
        k_new = _head_group_norm(p_ref[j, :, SWA_Q:SWA_Q + SWA_KV], gk_ref[...], gsum)
        v_new = p_ref[j, :, SWA_Q + SWA_KV:]
        k_all = jnp.concatenate([kc_ref[j], k_new], axis=0)
        v_all = jnp.concatenate([vc_ref[j], v_new], axis=0)
        q_groups = [_head_group_norm(p_ref[j, :, g * SWA_KV:(g + 1) * SWA_KV], gq_ref[...], gsum)
                    * (SWA_HEAD_DIM ** -0.5) for g in range(SWA_GROUP)]
        out = _swa_attend(q_groups, k_all.astype(BF16), v_all.astype(BF16), mask, sink_ref)
        o_ref[j] = jnp.concatenate(out, axis=1)
        k_ref[j] = k_all[T:]
        v_ref[j] = v_all[T:]


def _swa_sample(proj, k_cache, v_cache, gq, gk, sinks, batch, seq_len):
    nb = SWA_SAMPLE_BATCH
    W = WINDOW
    width = SWA_Q + 2 * SWA_KV
    p3 = proj.reshape(batch, seq_len, width)
    nbytes = 2 * (nb * seq_len * (width + SWA_Q) * 4 + 4 * nb * W * SWA_KV * 4)
    o, k, v = pl.pallas_call(
        _swa_sample_kernel,
        grid=(batch // nb,),
        in_specs=[pl.BlockSpec(memory_space=pltpu.SMEM),
                  pl.BlockSpec((nb, seq_len, width), lambda b: (b, 0, 0)),
                  pl.BlockSpec((nb, W, SWA_KV), lambda b: (b, 0, 0)),
                  pl.BlockSpec((nb, W, SWA_KV), lambda b: (b, 0, 0)),
                  pl.BlockSpec((1, SWA_KV), lambda b: (0, 0)),
                  pl.BlockSpec((1, SWA_KV), lambda b: (0, 0)),
                  pl.BlockSpec((SWA_KV, SWA_KV), lambda b: (0, 0))],
        out_specs=[pl.BlockSpec((nb, seq_len, SWA_Q), lambda b: (b, 0, 0)),
                   pl.BlockSpec((nb, W, SWA_KV), lambda b: (b, 0, 0)),
                   pl.BlockSpec((nb, W, SWA_KV), lambda b: (b, 0, 0))],
        out_shape=[jax.ShapeDtypeStruct((batch, seq_len, SWA_Q), F32),
                   jax.ShapeDtypeStruct((batch, W, SWA_KV), F32),
                   jax.ShapeDtypeStruct((batch, W, SWA_KV), F32)],
        compiler_params=_params(("parallel",), nbytes),
        name="swa_sample",
    )(sinks, p3, k_cache, v_cache, jnp.tile(gq, SWA_KV_HEADS).reshape(1, SWA_KV),
      jnp.tile(gk, SWA_KV_HEADS).reshape(1, SWA_KV), _swa_gsum())
    return o.reshape(batch * seq_len, SWA_Q), k, v


def _lru_conv(x, shifted, cw_ref, cb_ref):
    y = cb_ref[...] + cw_ref[CONV_WIDTH - 1:CONV_WIDTH] * x
    for s in range(1, CONV_WIDTH):
        y = y + cw_ref[CONV_WIDTH - 1 - s:CONV_WIDTH - s] * shifted[s - 1]
    return y


def _block_diag_dot(x, w_ref):
    xb = x.astype(BF16)
    return jnp.concatenate(
        [jnp.dot(xb[:, n * LRU_BLOCK:(n + 1) * LRU_BLOCK], w_ref[n], preferred_element_type=F32)
         for n in range(LRU_BLOCKS)], axis=1)


def _lru_terms(xc, wga_ref, bga_ref, wgx_ref, bgx_ref, lam_ref):
    r = _sigmoid(_block_diag_dot(xc, wga_ref) + bga_ref[...])
    i = _sigmoid(_block_diag_dot(xc, wgx_ref) + bgx_ref[...])
    log_a = (-LRU_C) * r * _softplus(-lam_ref[...])
    a = jnp.exp(log_a)
    y2 = 2.0 * log_a
    u = a * a
    em1 = jnp.where(u == 1.0, y2, jnp.where(u == 0.0, -1.0, (u - 1.0) * y2 / jnp.log(u)))
    mult = jnp.sqrt(-em1)
    return a, mult * i * xc


def _scan_rows(a, b, group):
    rows = a.shape[0]
    pos = lax.broadcasted_iota(jnp.int32, a.shape, 0) % group
    d = 1
    while d < group:
        keep = pos >= d
        b = jnp.where(keep, a * pltpu.roll(b, d, 0) + b, b)
        a = jnp.where(keep, a * pltpu.roll(a, d, 0), a)
        d *= 2
    return a, b


def _lru_prompt_kernel(p_ref, cw_ref, cb_ref, wga_ref, bga_ref, wgx_ref, bgx_ref, lam_ref,
                       o_ref, h_ref, tail_ref, hc_ref):
    n = pl.program_id(1)
    rows = p_ref.shape[1]

    @pl.when(n == 0)
    def _():
        tail_ref[...] = jnp.zeros_like(tail_ref)
        hc_ref[...] = jnp.zeros_like(hc_ref)

    y = p_ref[0, :, :D_RNN]
    x = p_ref[0, :, D_RNN:]
    tail = tail_ref[...]
    r8 = lax.broadcasted_iota(jnp.int32, (V7X_SUBLANES, D_RNN), 0)
    shifted = []
    for s in range(1, CONV_WIDTH):
        xs = pltpu.roll(x, s, 0)
        head = jnp.where(r8 < s, pltpu.roll(tail, s, 0), xs[:V7X_SUBLANES])
        shifted.append(jnp.concatenate([head, xs[V7X_SUBLANES:]], axis=0))
    xc = _lru_conv(x, shifted, cw_ref, cb_ref)
    a, bterm = _lru_terms(xc, wga_ref, bga_ref, wgx_ref, bgx_ref, lam_ref)
    acum, hzero = _scan_rows(a, bterm, rows)
    hs = acum * hc_ref[0:1] + hzero
    o_ref[0] = _gelu_tanh(y) * hs
    last = hs[rows - 1:rows]
    h_ref[0] = last
    hc_ref[...] = jnp.broadcast_to(last, hc_ref.shape)
    tail_ref[...] = x[rows - V7X_SUBLANES:]


LRU_TILE = 256


def _lru_weight_specs(imap):
    return [pl.BlockSpec((CONV_WIDTH, D_RNN), imap(2)),
            pl.BlockSpec((1, D_RNN), imap(2)),
            pl.BlockSpec((LRU_BLOCKS, LRU_BLOCK, LRU_BLOCK), imap(3)),
            pl.BlockSpec((1, D_RNN), imap(2)),
            pl.BlockSpec((LRU_BLOCKS, LRU_BLOCK, LRU_BLOCK), imap(3)),
            pl.BlockSpec((1, D_RNN), imap(2)),
            pl.BlockSpec((1, D_RNN), imap(2))]


def _lru_prompt(proj, cw, cb, wga, bga, wgx, bgx, lam, batch, seq_len):
    R = LRU_TILE
    p3 = proj.reshape(batch, seq_len, 2 * D_RNN)
    nbytes = 2 * (R * 3 * D_RNN * 4 + 2 * LRU_BLOCKS * LRU_BLOCK * LRU_BLOCK * 2) + 24 * R * D_RNN * 4
    o, h = pl.pallas_call(
        _lru_prompt_kernel,
        grid=(batch, seq_len // R),
        in_specs=[pl.BlockSpec((1, R, 2 * D_RNN), lambda b, n: (b, n, 0))]
        + _lru_weight_specs(lambda nd: (lambda b, n: (0,) * nd)),
        out_specs=[pl.BlockSpec((1, R, D_RNN), lambda b, n: (b, n, 0)),
                   pl.BlockSpec((1, 1, D_RNN), lambda b, n: (b, 0, 0))],
        out_shape=[jax.ShapeDtypeStruct((batch, seq_len, D_RNN), F32),
                   jax.ShapeDtypeStruct((batch, 1, D_RNN), F32)],
        scratch_shapes=[pltpu.VMEM((V7X_SUBLANES, D_RNN), F32), pltpu.VMEM((V7X_SUBLANES, D_RNN), F32)],
        compiler_params=_params(("parallel", "arbitrary"), nbytes),
        name="lru_prompt",
    )(p3, cw, cb.reshape(1, D_RNN), wga.astype(BF16), bga.reshape(1, D_RNN), wgx.astype(BF16),
      bgx.reshape(1, D_RNN), lam.reshape(1, D_RNN))
    return o.reshape(batch * seq_len, D_RNN), h.reshape(batch, D_RNN)


def _lru_sample_kernel(p_ref, prev_ref, h0_ref, cw_ref, cb_ref, wga_ref, bga_ref, wgx_ref, bgx_ref, lam_ref,
                       o_ref, hs_ref, *, seq_len):
    rows = p_ref.shape[0]
    y = p_ref[:, :D_RNN]
    x = p_ref[:, D_RNN:]
    prev = prev_ref[...]
    pos = lax.broadcasted_iota(jnp.int32, (rows, D_RNN), 0) % seq_len
    shifted = [jnp.where(pos < s, pltpu.roll(prev, rows - seq_len + s, 0), pltpu.roll(x, s, 0))
               for s in range(1, CONV_WIDTH)]
    xc = _lru_conv(x, shifted, cw_ref, cb_ref)
    a, bterm = _lru_terms(xc, wga_ref, bga_ref, wgx_ref, bgx_ref, lam_ref)
    acum, hzero = _scan_rows(a, bterm, seq_len)
    hs = acum * h0_ref[...] + hzero
    o_ref[...] = _gelu_tanh(y) * hs
    hs_ref[...] = hs


def _lru_sample(proj, conv_state, h0, cw, cb, wga, bga, wgx, bgx, lam, batch, seq_len):
    assert seq_len == V7X_SUBLANES
    n = batch * seq_len
    R = LRU_TILE
    prev = jnp.pad(conv_state, ((0, 0), (seq_len - (CONV_WIDTH - 1), 0), (0, 0))).reshape(n, D_RNN)
    h0_rows = jnp.repeat(h0, seq_len, axis=0)
    nbytes = 2 * (R * 6 * D_RNN * 4 + 2 * LRU_BLOCKS * LRU_BLOCK * LRU_BLOCK * 2) + 24 * R * D_RNN * 4
    return pl.pallas_call(
        functools.partial(_lru_sample_kernel, seq_len=seq_len),
        grid=(n // R,),
        in_specs=[pl.BlockSpec((R, 2 * D_RNN), lambda i: (i, 0)),
                  pl.BlockSpec((R, D_RNN), lambda i: (i, 0)),
                  pl.BlockSpec((R, D_RNN), lambda i: (i, 0))]
        + _lru_weight_specs(lambda nd: (lambda i: (0,) * nd)),
        out_specs=[pl.BlockSpec((R, D_RNN), lambda i: (i, 0)),
                   pl.BlockSpec((R, D_RNN), lambda i: (i, 0))],
        out_shape=[jax.ShapeDtypeStruct((n, D_RNN), F32), jax.ShapeDtypeStruct((n, D_RNN), F32)],
        compiler_params=_params(("parallel",), nbytes),
        name="lru_sample",
    )(proj, prev, h0_rows, cw, cb.reshape(1, D_RNN), wga.astype(BF16), bga.reshape(1, D_RNN), wgx.astype(BF16),
      bgx.reshape(1, D_RNN), lam.reshape(1, D_RNN))


def _run(gen):
    while True:
        try:
            next(gen)
        except StopIteration as stop:
            return stop.value


def _lockstep(gens):
    results = [None] * len(gens)
    live = list(range(len(gens)))
    anchor = None
    while live:
        for idx in list(live):
            try:
                gens[idx].send(anchor)
            except StopIteration as stop:
                results[idx] = stop.value
                live.remove(idx)
        if live:
            anchor = yield
    return results


def _topk_rows_steps(s, k):
    n = s.shape[0]
    rid = lax.broadcasted_iota(jnp.int32, s.shape, 0).astype(F32)
    vals, ids = [], []
    for _ in range(k):
        m = jnp.max(s, axis=0, keepdims=True)
        ix = jnp.min(jnp.where(s == m, rid, float(n)), axis=0, keepdims=True)
        vals.append(m)
        ids.append(ix)
        s = jnp.where(rid == ix, -jnp.inf, s)
        anchor = yield
        if anchor is not None:
            s = s + anchor
    return jnp.concatenate(vals, axis=0), jnp.concatenate(ids, axis=0).astype(jnp.int32)


def _zero_from(parts):
    acc = None
    for x in parts:
        bits = pltpu.bitcast(x, jnp.uint32)
        bits = bits.reshape(bits.shape[0] // V7X_SUBLANES, V7X_SUBLANES, bits.shape[1])
        folded = bits[0]
        for r in range(1, bits.shape[0]):
            folded = folded | bits[r]
        acc = folded if acc is None else acc | folded
    cols = [acc[:, t * V7X_LANES:(t + 1) * V7X_LANES] for t in range(acc.shape[1] // V7X_LANES)]
    one = cols[0]
    for t in cols[1:]:
        one = one | t
    zero = lax.shift_right_logical(lax.shift_right_logical(one, jnp.uint32(16)), jnp.uint32(16))
    return pltpu.bitcast(zero, F32)[0:1, 0:1]


def _staircase_candidates(s1, s2):
    K = s1.shape[0]
    sub = V7X_SUBLANES
    first_single = next(a for a in range(K) if K // (a + 1) == 1)
    assert (K - first_single) % sub == 0
    pieces, starts, at = [], [], 0
    for a in range(first_single):
        nb = K // (a + 1)
        rows = -(-nb // sub) * sub
        piece = s1[a:a + 1] + s2[:rows]
        if rows != nb:
            piece = jnp.where(lax.broadcasted_iota(jnp.int32, piece.shape, 0) < nb, piece, -jnp.inf)
        pieces.append(piece)
        starts.append(at)
        at += rows
    pieces.append(s1[first_single:] + s2[0:1])
    return jnp.concatenate(pieces, axis=0), starts, at


ROUTE_STEPS = 2 + 2 * PEER_TOPK


def _route_head_steps(q_ref, sk_ref, stage_ref):
    K = PEER_TOPK
    par = pl.program_id(1) % 2
    for p in range(2):
        qh = q_ref[:, p * PEER_HALF:(p + 1) * PEER_HALF].astype(BF16)
        stage_ref[par, p] = lax.dot_general(sk_ref[0, p].astype(BF16), qh, (((1,), (1,)), ((), ())),
                                            preferred_element_type=F32)
    anchor = yield
    scores = [stage_ref[par, 0], stage_ref[par, 1]]
    if anchor is not None:
        scores = [st + anchor for st in scores]
    (s1, i1), (s2, i2) = yield from _lockstep([_topk_rows_steps(st, K) for st in scores])
    anchor = yield
    cand, starts, single_start = _staircase_candidates(s1, s2)
    if anchor is not None:
        cand = cand + anchor
    top, ci = yield from _topk_rows_steps(cand, K)
    a_id = jnp.zeros_like(ci)
    group_start = jnp.zeros_like(ci)
    for a in range(1, len(starts)):
        a_id = jnp.where(ci >= starts[a], a, a_id)
        group_start = jnp.where(ci >= starts[a], starts[a], group_start)
    single = ci >= single_start
    a_id = jnp.where(single, len(starts) + ci - single_start, a_id)
    b_id = jnp.where(single, 0, ci - group_start)
    e1 = jnp.zeros_like(ci)
    e2 = jnp.zeros_like(ci)
    for a in range(K):
        e1 = jnp.where(a_id == a, i1[a:a + 1], e1)
        e2 = jnp.where(b_id == a, i2[a:a + 1], e2)
    e = jnp.exp(top - top[0:1])
    return e1, e2, e / jnp.sum(e, axis=0, keepdims=True)


def _route_kernel(q_ref, sk_ref, e1_ref, e2_ref, g_ref, stage_ref):
    e1_ref[0], e2_ref[0], g_ref[0] = _run(_route_head_steps(q_ref, sk_ref, stage_ref))


EXPERT_TILE = 256


def _peer_route(q, sub_keys):
    n = q.shape[0]
    tb = EXPERT_TILE
    spec = pl.BlockSpec((1, PEER_TOPK, tb), lambda i, h: (h, 0, i))
    nbytes = 2 * (tb * PEER_KEY_DIM * 4 + 2 * PEER_NKEYS * PEER_HALF * 4) + 16 * 2 * PEER_NKEYS * tb * 4
    return pl.pallas_call(
        _route_kernel,
        grid=(n // tb, PEER_HEADS),
        in_specs=[pl.BlockSpec((tb, PEER_KEY_DIM), lambda i, h: (i, h)),
                  pl.BlockSpec((1, 2, PEER_NKEYS, PEER_HALF), lambda i, h: (h, 0, 0, 0))],
        out_specs=[spec, spec, spec],
        out_shape=[jax.ShapeDtypeStruct((PEER_HEADS, PEER_TOPK, n), jnp.int32),
                   jax.ShapeDtypeStruct((PEER_HEADS, PEER_TOPK, n), jnp.int32),
                   jax.ShapeDtypeStruct((PEER_HEADS, PEER_TOPK, n), F32)],
        scratch_shapes=[pltpu.VMEM((2, 2, PEER_NKEYS, tb), F32)],
        compiler_params=_params(("parallel", "parallel"), nbytes),
        name="peer_route",
    )(q, sub_keys)


EXPERT_CHUNK = PEER_EXPERTS // PEER_HEADS
EXPERT_SUB = 256
WEIGHT_BLOCK = 512
GATE_ROW_PAD = V7X_SUBLANES
GATE_UNROLL = 16


def _expert_kernel(h_ref, q_ref, sk_ref, e1_ref, e2_ref, g_ref, ut_ref, v_ref, x_ref, gt_ref, o_ref,
                   gate_ref, acc_ref, e1t_ref, e2t_ref, gtt_ref, re1_ref, re2_ref, rg_ref, stage_ref, w_ref):
    i = pl.program_id(0)
    c = pl.program_id(1)
    tb = h_ref.shape[0]
    NK = PEER_NKEYS
    stride = tb + GATE_ROW_PAD
    slot = i % 2

    @pl.when((i == 0) & (c == 0))
    def _():
        re1_ref[0] = e1_ref[...].reshape(PEER_PAIRS, tb)
        re2_ref[0] = e2_ref[...].reshape(PEER_PAIRS, tb)
        rg_ref[0] = g_ref[...].reshape(PEER_PAIRS, tb)

    @pl.when(c == 0)
    def _():
        acc_ref[...] = jnp.zeros_like(acc_ref)
        e1t_ref[...] = jnp.transpose(re1_ref[slot])
        e2t_ref[...] = jnp.transpose(re2_ref[slot])
        gtt_ref[...] = jnp.transpose(rg_ref[slot])
        kid = lax.broadcasted_iota(jnp.int32, (NK, PEER_PAIRS), 0)

        def per_token(n, carry):
            i1 = e1t_ref[pl.ds(n, 1), :]
            i2 = e2t_ref[pl.ds(n, 1), :]
            gg = gtt_ref[pl.ds(n, 1), :]
            a_t = jnp.where(kid == i1, 1.0, 0.0).astype(BF16)
            b_t = jnp.where(kid == i2, 0.5 * gg, 0.0).astype(BF16)
            gn = lax.dot_general(a_t, b_t, (((1,), (1,)), ((), ())), preferred_element_type=F32)
            gate_ref[pl.ds(n, NK, stride=stride), :] = gn
            return carry

        lax.fori_loop(0, tb, per_token, 0, unroll=GATE_UNROLL)

    route = _route_head_steps(q_ref, sk_ref, stage_ref)
    routed = []
    n_sub = EXPERT_CHUNK // EXPERT_SUB

    def advance_route(rounds, anchor=None):
        for _ in range(rounds):
            if routed:
                return
            try:
                route.send(anchor)
            except StopIteration as stop:
                routed.append(stop.value)
            anchor = None

    h = h_ref[...]
    keys_per_chunk = EXPERT_CHUNK // NK
    advance_route(1)
    WB = WEIGHT_BLOCK
    w_parts = []
    for j in range(n_sub):
        lo = j * EXPERT_SUB
        blk, off = divmod(lo, WB)
        s = jnp.dot(h, ut_ref[blk, :, off:off + EXPERT_SUB], preferred_element_type=F32)
        gsel = jnp.concatenate(
            [gate_ref[pl.ds(pl.multiple_of((c * keys_per_chunk + lo // NK + t) * stride, V7X_SUBLANES), tb), :]
             for t in range(EXPERT_SUB // NK)], axis=1)
        wb = _gelu_times_half_gate(s, gsel).astype(BF16)
        w_parts.append(wb)
        w_ref[blk, :, off:off + EXPERT_SUB] = wb
    anchor = _zero_from(w_parts)
    for n in range(v_ref.shape[0]):
        part = None
        for k in range(EXPERT_CHUNK // WB):
            d = jnp.dot(w_ref[k], v_ref[n, k * WB:(k + 1) * WB, :], preferred_element_type=F32)
            part = d if part is None else part + d
        acc_ref[:, n * WB:(n + 1) * WB] += part
    advance_route(ROUTE_STEPS + 1, anchor)
    n1, n2, ng = routed[0]
    rows = pl.ds(pl.multiple_of(c * PEER_TOPK, PEER_TOPK), PEER_TOPK)
    re1_ref[1 - slot, rows, :] = n1
    re2_ref[1 - slot, rows, :] = n2
    rg_ref[1 - slot, rows, :] = ng

    @pl.when(c == pl.num_programs(1) - 1)
    def _():
        o_ref[...] = x_ref[...] + gt_ref[0] * acc_ref[...]


def _peer_u_blocks(u):
    e, d = u.shape
    return u.reshape(e // WEIGHT_BLOCK, WEIGHT_BLOCK, d).transpose(0, 2, 1).astype(BF16)


def _peer_v_blocks(v):
    e, d = v.shape
    return v.reshape(e, d // WEIGHT_BLOCK, WEIGHT_BLOCK).transpose(1, 0, 2).astype(BF16)


def _peer_experts(h_bf16, q, sub_keys, u_blocks, v_blocks, x, gate, seq_len):
    n, d = x.shape
    tb = EXPERT_TILE
    ec = EXPERT_CHUNK
    wb = WEIGHT_BLOCK
    nt = n // tb
    e1, e2, g = _peer_route(q[:tb], sub_keys)
    gt_arr, gt_spec0 = _row_operand(gate, seq_len, tb)
    gt_spec = pl.BlockSpec(gt_spec0.block_shape, lambda i, c: gt_spec0.index_map(i))
    rspec = pl.BlockSpec((PEER_HEADS, PEER_TOPK, tb), lambda i, c: (0, 0, 0))
    nbytes = (2 * (tb * d * 2 + 3 * PEER_PAIRS * tb * 4 + 2 * d * ec * 2 + 3 * tb * d * 4 + tb * PEER_KEY_DIM * 4)
              + (tb + GATE_ROW_PAD) * PEER_NKEYS * PEER_NKEYS * 4 + tb * d * 4 + 9 * tb * PEER_PAIRS * 4
              + 8 * tb * EXPERT_SUB * 4 + 8 * PEER_NKEYS * tb * 4)
    return pl.pallas_call(
        _expert_kernel,
        grid=(nt, PEER_HEADS),
        in_specs=[pl.BlockSpec((tb, d), lambda i, c: (i, 0)),
                  pl.BlockSpec((tb, PEER_KEY_DIM), lambda i, c: (jnp.minimum(i + 1, nt - 1), c)),
                  pl.BlockSpec((1, 2, PEER_NKEYS, PEER_HALF), lambda i, c: (c, 0, 0, 0)),
                  rspec, rspec, rspec,
                  pl.BlockSpec((ec // wb, d, wb), lambda i, c: (c, 0, 0)),
                  pl.BlockSpec((d // wb, ec, wb), lambda i, c: (0, c, 0)),
                  pl.BlockSpec((tb, d), lambda i, c: (i, 0)),
                  gt_spec],
        out_specs=pl.BlockSpec((tb, d), lambda i, c: (i, 0)),
        out_shape=jax.ShapeDtypeStruct((n, d), F32),
        scratch_shapes=[pltpu.VMEM(((tb + GATE_ROW_PAD) * PEER_NKEYS, PEER_NKEYS), F32),
                        pltpu.VMEM((tb, d), F32),
                        pltpu.VMEM((tb, PEER_PAIRS), jnp.int32),
                        pltpu.VMEM((tb, PEER_PAIRS), jnp.int32),
                        pltpu.VMEM((tb, PEER_PAIRS), F32),
                        pltpu.VMEM((2, PEER_PAIRS, tb), jnp.int32),
                        pltpu.VMEM((2, PEER_PAIRS, tb), jnp.int32),
                        pltpu.VMEM((2, PEER_PAIRS, tb), F32),
                        pltpu.VMEM((2, 2, PEER_NKEYS, tb), F32),
                        pltpu.VMEM((ec // wb, tb, wb), BF16)],
        compiler_params=_params(("arbitrary", "arbitrary"), nbytes),
        name="peer_experts",
    )(h_bf16, q, sub_keys, e1, e2, g, u_blocks, v_blocks, x, gt_arr)


def _swa_permute_in(w_in):
    d = w_in.shape[0]
    wq = w_in[:, :SWA_Q].reshape(d, SWA_KV_HEADS, SWA_GROUP, SWA_HEAD_DIM).transpose(0, 2, 1, 3).reshape(d, SWA_Q)
    return jnp.concatenate([wq, w_in[:, SWA_Q:]], axis=1)


def _swa_permute_out(w_out):
    d = w_out.shape[1]
    return w_out.reshape(SWA_KV_HEADS, SWA_GROUP, SWA_HEAD_DIM, d).transpose(1, 0, 2, 3).reshape(SWA_Q, d)


def _prepare_weights(p):
    w = {}
    w['gla_in'] = [jnp.pad(p['w_gla_in'][j], ((0, 0), (0, GLA_IN_PAD - p['w_gla_in'].shape[2]))).astype(BF16)
                   for j in range(p['w_gla_in'].shape[0])]
    w['gla_a2'] = [jnp.pad(p['w_gla_a2'][j], ((0, V7X_LANES - GLA_GATE_RANK), (0, 0)))
                   for j in range(p['w_gla_a2'].shape[0])]
    w['gla_out'] = [m.astype(BF16) for m in p['w_gla_out']]
    w['swa_in'] = [_swa_permute_in(m).astype(BF16) for m in p['w_swa_in']]
    w['swa_out'] = [_swa_permute_out(m).astype(BF16) for m in p['w_swa_out']]
    w['lru_in'] = [m.astype(BF16) for m in p['w_lru_in']]
    w['lru_out'] = [m.astype(BF16) for m in p['w_lru_out']]
    w['peer_q'] = [m.astype(BF16) for m in p['w_peer_q']]
    w['peer_ut'] = [_peer_u_blocks(m) for m in p['peer_u']]
    w['peer_v'] = [_peer_v_blocks(m) for m in p['peer_v']]
    return w


def _trunk(x3, mod, states, p, w):
    batch, seq_len, d = x3.shape
    x = x3.reshape(batch * seq_len, d)
    new_gla, new_k, new_v, new_conv, new_h = [], [], [], [], []
    for i in range(DEPTH):
        kind, j = i % N_MIXERS, i // N_MIXERS
        sh_m, sc_m, gt_m, sh_f, sc_f, gt_f = [mod[i][:, k * d:(k + 1) * d] for k in range(6)]
        if kind == 0:
            proj = _norm_proj(x, p['g_ln_mix'][i], sc_m, sh_m, w['gla_in'][j], seq_len)
            if states is None:
                mix, s_new = _gla_prompt(proj, w['gla_a2'][j], p['b_gla_a'][j], p['g_gla_norm'][j], batch, seq_len)
            else:
                mix, s_new = _gla_sample(proj, states[0][j], w['gla_a2'][j], p['b_gla_a'][j], p['g_gla_norm'][j],
                                         batch, seq_len)
            new_gla.append(s_new)
            w_out = w['gla_out'][j]
        elif kind == 1:
            proj = _norm_proj(x, p['g_ln_mix'][i], sc_m, sh_m, w['swa_in'][j], seq_len)
            if states is None:
                mix, k_n, v_n = _swa_prompt(proj, p['g_swa_q'][j], p['g_swa_k'][j], p['swa_sinks'][j], batch, seq_len)
            else:
                kc = states[1][j].reshape(batch, WINDOW, SWA_KV)
                vc = states[2][j].reshape(batch, WINDOW, SWA_KV)
                mix, k_n, v_n = _swa_sample(proj, kc, vc, p['g_swa_q'][j], p['g_swa_k'][j], p['swa_sinks'][j],
                                            batch, seq_len)
            new_k.append(k_n.reshape(batch, WINDOW, SWA_KV_HEADS, SWA_HEAD_DIM))
            new_v.append(v_n.reshape(batch, WINDOW, SWA_KV_HEADS, SWA_HEAD_DIM))
            w_out = w['swa_out'][j]
        else:
            proj = _norm_proj(x, p['g_ln_mix'][i], sc_m, sh_m, w['lru_in'][j], seq_len)
            lru_args = (p['lru_conv_w'][j], p['lru_conv_b'][j], p['w_lru_ga'][j], p['b_lru_ga'][j],
                        p['w_lru_gx'][j], p['b_lru_gx'][j], p['lru_lam'][j], batch, seq_len)
            assert seq_len >= CONV_WIDTH - 1
            if states is None:
                mix, h_n = _lru_prompt(proj, *lru_args)
            else:
                mix, hs = _lru_sample(proj, states[3][j], states[4][j], *lru_args)
                h_n = hs.reshape(batch, seq_len, D_RNN)[:, -1]
            new_conv.append(proj[:, D_RNN:].reshape(batch, seq_len, D_RNN)[:, seq_len - (CONV_WIDTH - 1):])
            new_h.append(h_n)
            w_out = w['lru_out'][j]
        x = _proj_residual(mix, w_out, x, gt_m, seq_len)
        q, hb = _norm_proj(x, p['g_ln_ffn'][i], sc_f, sh_f, w['peer_q'][i], seq_len, with_h=True)
        x = _peer_experts(hb, q, p['peer_sub_keys'][i], w['peer_ut'][i], w['peer_v'][i], x, gt_f, seq_len)
    y = x.reshape(batch, seq_len, d)
    return y, (jnp.stack(new_gla), jnp.stack(new_k), jnp.stack(new_v), jnp.stack(new_conv), jnp.stack(new_h))


def kernel(x_prompt, x_sample, state_gla, cache_swa_k, cache_swa_v, state_lru_conv, state_lru_h,
           c_prompt, c_sample, g_ln_mix, g_ln_ffn, w_mod, b_mod,
           w_gla_in, w_gla_a2, b_gla_a, g_gla_norm, w_gla_out,
           w_swa_in, g_swa_q, g_swa_k, swa_sinks, w_swa_out,
           w_lru_in, lru_conv_w, lru_conv_b, w_lru_ga, b_lru_ga, w_lru_gx, b_lru_gx, lru_lam, w_lru_out,
           w_peer_q, peer_sub_keys, peer_u, peer_v):
    p = {'g_ln_mix': g_ln_mix, 'g_ln_ffn': g_ln_ffn,
         'w_gla_in': w_gla_in, 'w_gla_a2': w_gla_a2, 'b_gla_a': b_gla_a, 'g_gla_norm': g_gla_norm,
         'w_gla_out': w_gla_out,
         'w_swa_in': w_swa_in, 'g_swa_q': g_swa_q, 'g_swa_k': g_swa_k, 'swa_sinks': swa_sinks,
         'w_swa_out': w_swa_out,
         'w_lru_in': w_lru_in, 'lru_conv_w': lru_conv_w, 'lru_conv_b': lru_conv_b,
         'w_lru_ga': w_lru_ga, 'b_lru_ga': b_lru_ga, 'w_lru_gx': w_lru_gx, 'b_lru_gx': b_lru_gx,
         'lru_lam': lru_lam, 'w_lru_out': w_lru_out,
         'w_peer_q': w_peer_q, 'peer_sub_keys': peer_sub_keys, 'peer_u': peer_u, 'peer_v': peer_v}
    w = _prepare_weights(p)
    nb_p, nb_s = c_prompt.shape[0], c_sample.shape[0]
    rows = -(-(nb_p + nb_s) // V7X_SUBLANES) * V7X_SUBLANES
    c_all = jnp.pad(jnp.concatenate([c_prompt, c_sample], axis=0), ((0, rows - nb_p - nb_s), (0, 0)))
    mod = _modulation(c_all, w_mod, b_mod)
    y_p, (gla_p, k_p, v_p, conv_p, h_p) = _trunk(x_prompt, mod[:, :nb_p], None, p, w)
    y_s, (gla_s, k_s, v_s, conv_s, h_s) = _trunk(
        x_sample, mod[:, nb_p:nb_p + nb_s],
        (state_gla, cache_swa_k, cache_swa_v, state_lru_conv, state_lru_h), p, w)
    return (y_p, y_s, gla_p, gla_s, k_p, k_s, v_p, v_s, conv_p, conv_s, h_p, h_s)
```

```python
import functools
import math

import jax
import jax.numpy as jnp
from jax import lax
from jax.experimental import pallas as pl
from jax.experimental.pallas import tpu as pltpu

F32 = jnp.float32
BF16 = jnp.bfloat16

D_MODEL = 1024
DEPTH = 4
N_MIXERS = 3
RMS_EPS = 1e-6

GLA_HEADS = 4
GLA_QK = D_MODEL // 2
GLA_V = D_MODEL
GLA_DK = GLA_QK // GLA_HEADS
GLA_DV = GLA_V // GLA_HEADS
GLA_GATE_RANK = 16
GLA_TAU = 16.0
GLA_SUB = 16
GLA_CHUNK = 128
GLA_IN_PAD = 2 * GLA_QK + 2 * GLA_V + 128

SWA_HEAD_DIM = 64
SWA_Q_HEADS = D_MODEL // SWA_HEAD_DIM
SWA_KV_HEADS = 4
SWA_GROUP = SWA_Q_HEADS // SWA_KV_HEADS
SWA_Q = SWA_Q_HEADS * SWA_HEAD_DIM
SWA_KV = SWA_KV_HEADS * SWA_HEAD_DIM
WINDOW = 128

D_RNN = D_MODEL
LRU_BLOCKS = 4
LRU_BLOCK = D_RNN // LRU_BLOCKS
CONV_WIDTH = 4
LRU_C = 8.0

PEER_HEADS = 8
PEER_NKEYS = 128
PEER_EXPERTS = PEER_NKEYS * PEER_NKEYS
PEER_KEY_DIM = 256
PEER_HALF = PEER_KEY_DIM // 2
PEER_TOPK = 16
PEER_PAIRS = PEER_HEADS * PEER_TOPK

V7X_LANES = 128
V7X_SUBLANES = 8
V7X_VMEM_BYTES = 64 * 1024 * 1024

TOKEN_TILE = 256
NEG_BIG = -1e30


def _vmem_limit(nbytes):
    return int(min(max(nbytes * 3 // 2, 16 * 1024 * 1024), V7X_VMEM_BYTES - 8 * 1024 * 1024))


def _params(semantics, nbytes):
    return pltpu.CompilerParams(dimension_semantics=semantics, vmem_limit_bytes=_vmem_limit(nbytes))


def _rms(x, g):
    return x * lax.rsqrt(jnp.mean(x * x, axis=-1, keepdims=True) + RMS_EPS) * g


def _gelu_tanh(x):
    return 0.5 * x * (1.0 + jnp.tanh(math.sqrt(2.0 / math.pi) * (x + 0.044715 * (x * x * x))))


def _gelu_times_half_gate(x, half_gate):
    c1 = math.sqrt(2.0 / math.pi)
    inner = x * (c1 + (c1 * 0.044715) * (x * x))
    return (x * half_gate) * (1.0 + jnp.tanh(inner))


def _sigmoid(x):
    return 1.0 / (1.0 + jnp.exp(-x))


def _softplus(x):
    return jnp.maximum(x, 0.0) + jnp.log1p(jnp.exp(-jnp.abs(x)))


def _row_operand(vec, seq_len, tile):
    b, d = vec.shape
    if seq_len % tile == 0:
        per_seq = seq_len // tile
        return vec.reshape(b, 1, d), pl.BlockSpec((1, 1, d), lambda i: (i // per_seq, 0, 0))
    assert tile % seq_len == 0
    rep = jnp.repeat(vec, seq_len, axis=0).reshape(b * seq_len // tile, tile, d)
    return rep, pl.BlockSpec((1, tile, d), lambda i: (i, 0, 0))


def _mod_kernel(c_ref, w_ref, b_ref, o_ref):
    c = c_ref[...]
    sc = c * _sigmoid(c)
    o_ref[0] = jnp.dot(sc, w_ref[0], preferred_element_type=F32) + b_ref[0]


def _modulation(c, w_mod, b_mod):
    bp, d = c.shape
    tn = 1024
    nt = 6 * d // tn
    return pl.pallas_call(
        _mod_kernel,
        grid=(DEPTH, nt),
        in_specs=[pl.BlockSpec((bp, d), lambda l, j: (0, 0)),
                  pl.BlockSpec((1, d, tn), lambda l, j: (l, 0, j)),
                  pl.BlockSpec((1, 1, tn), lambda l, j: (l, 0, j))],
        out_specs=pl.BlockSpec((1, bp, tn), lambda l, j: (l, 0, j)),
        out_shape=jax.ShapeDtypeStruct((DEPTH, bp, 6 * d), F32),
        compiler_params=_params(("parallel", "parallel"), 2 * (d * tn * 4 + 2 * bp * tn * 4)),
        name="modulation",
    )(c, w_mod, b_mod.reshape(DEPTH, 1, 6 * d))


def _norm_proj_kernel(x_ref, g_ref, sc_ref, sh_ref, w_ref, o_ref, *h_ref):
    h = _rms(x_ref[...], g_ref[...]) * (1.0 + sc_ref[0]) + sh_ref[0]
    hb = h.astype(BF16)
    o_ref[...] = jnp.dot(hb, w_ref[...], preferred_element_type=F32).astype(o_ref.dtype)
    if h_ref:
        h_ref[0][...] = hb


def _norm_proj(x, g, scale, shift, w_bf16, seq_len, with_h=False):
    n, d = x.shape
    nout = w_bf16.shape[1]
    tm = TOKEN_TILE
    sc_arr, sc_spec = _row_operand(scale, seq_len, tm)
    sh_arr, sh_spec = _row_operand(shift, seq_len, tm)
    out_shape = [jax.ShapeDtypeStruct((n, nout), BF16 if with_h else F32)]
    out_specs = [pl.BlockSpec((tm, nout), lambda i: (i, 0))]
    if with_h:
        out_shape.append(jax.ShapeDtypeStruct((n, d), BF16))
        out_specs.append(pl.BlockSpec((tm, d), lambda i: (i, 0)))
    nbytes = 2 * (tm * d * 4 + d * nout * 2 + tm * nout * 4 + 3 * tm * d * 4)
    res = pl.pallas_call(
        _norm_proj_kernel,
        grid=(n // tm,),
        in_specs=[pl.BlockSpec((tm, d), lambda i: (i, 0)),
                  pl.BlockSpec((1, d), lambda i: (0, 0)),
                  sc_spec, sh_spec,
                  pl.BlockSpec((d, nout), lambda i: (0, 0))],
        out_specs=out_specs,
        out_shape=out_shape,
        compiler_params=_params(("parallel",), nbytes),
        name="norm_proj",
    )(x, g.reshape(1, d), sc_arr, sh_arr, w_bf16)
    return res if with_h else res[0]


def _proj_residual_kernel(a_ref, w_ref, x_ref, gt_ref, o_ref):
    y = jnp.dot(a_ref[...].astype(BF16), w_ref[...], preferred_element_type=F32)
    o_ref[...] = x_ref[...] + gt_ref[0] * y


def _proj_residual(a, w_bf16, x, gate, seq_len):
    n, k = a.shape
    d = x.shape[1]
    tm = TOKEN_TILE
    gt_arr, gt_spec = _row_operand(gate, seq_len, tm)
    nbytes = 2 * (tm * k * 4 + k * d * 2 + 3 * tm * d * 4)
    return pl.pallas_call(
        _proj_residual_kernel,
        grid=(n // tm,),
        in_specs=[pl.BlockSpec((tm, k), lambda i: (i, 0)),
                  pl.BlockSpec((k, d), lambda i: (0, 0)),
                  pl.BlockSpec((tm, d), lambda i: (i, 0)),
                  gt_spec],
        out_specs=pl.BlockSpec((tm, d), lambda i: (i, 0)),
        out_shape=jax.ShapeDtypeStruct((n, d), F32),
        compiler_params=_params(("parallel",), nbytes),
        name="proj_residual",
    )(a, w_bf16, x, gt_arr)


def _log_decay(lr, wa2, ba):
    z = jnp.dot(lr, wa2, preferred_element_type=F32) + ba
    return (jnp.minimum(z, 0.0) - jnp.log1p(jnp.exp(-jnp.abs(z)))) * (1.0 / GLA_TAU)


def _col_bcast(row):
    return jnp.transpose(jnp.broadcast_to(row, (V7X_LANES, V7X_LANES)))


def _head_out(o, gate, gn):
    return _rms(o, gn) * (gate * _sigmoid(gate))


def _gla_prompt_kernel(p_ref, wa2_ref, ba_ref, gn_ref, o_ref, s_ref):
    c = pl.program_id(1)
    C = GLA_CHUNK
    nsub = C // GLA_SUB

    @pl.when(c == 0)
    def _():
        s_ref[...] = jnp.zeros_like(s_ref)

    row = lax.broadcasted_iota(jnp.int32, (C, C), 0)
    col = lax.broadcasted_iota(jnp.int32, (C, C), 1)
    tri = (col <= row).astype(F32)
    later = ((col > row) & (col // GLA_SUB == row // GLA_SUB)).astype(F32)
    sums = jnp.concatenate([tri, later], axis=0)
    causal = col <= row
    rsub = lax.broadcasted_iota(jnp.int32, (C, GLA_DK), 0) // GLA_SUB

    lr = p_ref[0, :, 2 * GLA_QK + 2 * GLA_V:]
    for h in range(GLA_HEADS):
        q = p_ref[0, :, h * GLA_DK:(h + 1) * GLA_DK] * (GLA_DK ** -0.5)
        k = p_ref[0, :, GLA_QK + h * GLA_DK:GLA_QK + (h + 1) * GLA_DK]
        v = p_ref[0, :, 2 * GLA_QK + h * GLA_DV:2 * GLA_QK + (h + 1) * GLA_DV]
        gate = p_ref[0, :, 2 * GLA_QK + GLA_V + h * GLA_DV:2 * GLA_QK + GLA_V + (h + 1) * GLA_DV]
        la = _log_decay(lr, wa2_ref[:, h * GLA_DK:(h + 1) * GLA_DK], ba_ref[:, h * GLA_DK:(h + 1) * GLA_DK])
        cs = jnp.dot(sums, la, preferred_element_type=F32, precision=lax.Precision.HIGHEST)
        b = cs[:C]
        to_sub_end = cs[C:]
        b_last = b[C - 1:C]
        k_sub = k * jnp.exp(to_sub_end)
        q_parts, k_parts = [], []
        for m in range(nsub):
            ref_row = b[m * GLA_SUB + GLA_SUB - 1:m * GLA_SUB + GLA_SUB]
            e = jnp.where(rsub >= m, b - ref_row, NEG_BIG)
            q_parts.append((q * jnp.exp(e)).astype(BF16))
            k_parts.append(jnp.where(rsub == m, k_sub, 0.0).astype(BF16))
        qcat = jnp.concatenate(q_parts, axis=1)
        kcat = jnp.concatenate(k_parts, axis=1)
        att = lax.dot_general(qcat, kcat, (((1,), (1,)), ((), ())), preferred_element_type=F32)
        att = jnp.where(causal, att, 0.0)
        s_old = s_ref[0, h]
        o = jnp.dot(att.astype(BF16), v.astype(BF16), preferred_element_type=F32)
        o = o + jnp.dot((q * jnp.exp(b)).astype(BF16), s_old.astype(BF16), preferred_element_type=F32)
        k_end = (k * jnp.exp(b_last - b)).astype(BF16)
        upd = lax.dot_general(k_end, v.astype(BF16), (((0,), (0,)), ((), ())), preferred_element_type=F32)
        decay = _col_bcast(jnp.exp(b_last))
        s_ref[0, h] = jnp.concatenate([decay] * (GLA_DV // V7X_LANES), axis=1) * s_old + upd
        o_ref[0, :, h * GLA_DV:(h + 1) * GLA_DV] = _head_out(o, gate, gn_ref[...])


def _gla_prompt(proj, wa2_pad, ba, gn, batch, seq_len):
    C = GLA_CHUNK
    p3 = proj.reshape(batch, seq_len, GLA_IN_PAD)
    nbytes = 2 * (C * GLA_IN_PAD * 4 + C * GLA_V * 4 + GLA_HEADS * GLA_DK * GLA_DV * 4) + 64 * C * C * 4
    o, s = pl.pallas_call(
        _gla_prompt_kernel,
        grid=(batch, seq_len // C),
        in_specs=[pl.BlockSpec((1, C, GLA_IN_PAD), lambda b, c: (b, c, 0)),
                  pl.BlockSpec((V7X_LANES, GLA_QK), lambda b, c: (0, 0)),
                  pl.BlockSpec((1, GLA_QK), lambda b, c: (0, 0)),
                  pl.BlockSpec((1, GLA_DV), lambda b, c: (0, 0))],
        out_specs=[pl.BlockSpec((1, C, GLA_V), lambda b, c: (b, c, 0)),
                   pl.BlockSpec((1, GLA_HEADS, GLA_DK, GLA_DV), lambda b, c: (b, 0, 0, 0))],
        out_shape=[jax.ShapeDtypeStruct((batch, seq_len, GLA_V), F32),
                   jax.ShapeDtypeStruct((batch, GLA_HEADS, GLA_DK, GLA_DV), F32)],
        compiler_params=_params(("parallel", "arbitrary"), nbytes),
        name="gla_prompt",
    )(p3, wa2_pad, ba.reshape(1, GLA_QK), gn.reshape(1, GLA_DV))
    return o.reshape(batch * seq_len, GLA_V), s


GLA_SAMPLE_BATCH = 8


def _gla_sample_kernel(p_ref, s0_ref, wa2_ref, ba_ref, gn_ref, *refs):
    o_ref, s_ref = refs[-2:]
    nseq = s0_ref.shape[1]
    rows = p_ref.shape[0]
    T = rows // nseq
    row = lax.broadcasted_iota(jnp.int32, (rows, rows), 0)
    col = lax.broadcasted_iota(jnp.int32, (rows, rows), 1)
    same_seq = (row // T) == (col // T)
    causal = same_seq & (col <= row)
    sums = jnp.concatenate([causal.astype(F32), same_seq.astype(F32)], axis=0)
    lr = p_ref[:, 2 * GLA_QK + 2 * GLA_V:]
    la = _log_decay(lr, wa2_ref[...], ba_ref[...])
    cs = jnp.dot(sums, la, preferred_element_type=F32, precision=lax.Precision.HIGHEST)
    b_all, b_last_all = cs[:rows], cs[rows:]
    for h in range(GLA_HEADS):
        hk = slice(h * GLA_DK, (h + 1) * GLA_DK)
        q = p_ref[:, h * GLA_DK:(h + 1) * GLA_DK] * (GLA_DK ** -0.5)
        k = p_ref[:, GLA_QK + h * GLA_DK:GLA_QK + (h + 1) * GLA_DK]
        v = p_ref[:, 2 * GLA_QK + h * GLA_DV:2 * GLA_QK + (h + 1) * GLA_DV]
        gate = p_ref[:, 2 * GLA_QK + GLA_V + h * GLA_DV:2 * GLA_QK + GLA_V + (h + 1) * GLA_DV]
        b, b_last = b_all[:, hk], b_last_all[:, hk]
        k_end = k * jnp.exp(b_last - b)
        q_rel = (q * jnp.exp(b - b_last)).astype(BF16)
        q_dec = q * jnp.exp(b)
        att = lax.dot_general(q_rel, k_end.astype(BF16), (((1,), (1,)), ((), ())), preferred_element_type=F32)
        att = jnp.where(causal, att, 0.0)
        o_intra = jnp.dot(att.astype(BF16), v.astype(BF16), preferred_element_type=F32)
        o_inter = []
        for j in range(nseq):
            rj = slice(j * T, (j + 1) * T)
            s_old = s0_ref[0, j, h]
            o_inter.append(jnp.dot(q_dec[rj].astype(BF16), s_old.astype(BF16), preferred_element_type=F32))
            upd = lax.dot_general(k_end[rj].astype(BF16), v[rj].astype(BF16), (((0,), (0,)), ((), ())),
                                  preferred_element_type=F32)
            decay = _col_bcast(jnp.exp(b_last[j * T:j * T + 1]))
            s_ref[0, j, h] = jnp.concatenate([decay] * (GLA_DV // V7X_LANES), axis=1) * s_old + upd
        o = o_intra + jnp.concatenate(o_inter, axis=0)
        o_ref[:, h * GLA_DV:(h + 1) * GLA_DV] = _head_out(o, gate, gn_ref[...])


def _gla_sample(proj, states, layer, new_states, wa2_pad, ba, gn, batch, seq_len):
    assert seq_len <= GLA_SUB
    nb = GLA_SAMPLE_BATCH
    rows = nb * seq_len
    state_spec = pl.BlockSpec((1, nb, GLA_HEADS, GLA_DK, GLA_DV), lambda b: (layer, b, 0, 0, 0))
    state_block = nb * GLA_HEADS * GLA_DK * GLA_DV * 4
    nbytes = 2 * (rows * (GLA_IN_PAD + GLA_V) * 4 + 2 * state_block) + 16 * rows * GLA_IN_PAD * 4
    in_specs = [pl.BlockSpec((rows, GLA_IN_PAD), lambda b: (b, 0)),
                state_spec,
                pl.BlockSpec((V7X_LANES, GLA_QK), lambda b: (0, 0)),
                pl.BlockSpec((1, GLA_QK), lambda b: (0, 0)),
                pl.BlockSpec((1, GLA_DV), lambda b: (0, 0))]
    operands = [proj, states, wa2_pad, ba.reshape(1, GLA_QK), gn.reshape(1, GLA_DV)]
    aliases = {}
    if new_states is not None:
        in_specs.append(pl.BlockSpec(memory_space=pl.ANY))
        operands.append(new_states)
        aliases = {len(operands) - 1: 1}
    o, s = pl.pallas_call(
        _gla_sample_kernel,
        grid=(batch // nb,),
        in_specs=in_specs,
        out_specs=[pl.BlockSpec((rows, GLA_V), lambda b: (b, 0)), state_spec],
        out_shape=[jax.ShapeDtypeStruct((batch * seq_len, GLA_V), F32),
                   jax.ShapeDtypeStruct(states.shape, F32)],
        input_output_aliases=aliases,
        compiler_params=_params(("parallel",), nbytes),
        name="gla_sample",
    )(*operands)
    return o, s


def _head_group_norm(x, gain, gsum):
    sq = x * x
    hi = sq.astype(BF16)
    lo = (sq - hi.astype(F32)).astype(BF16)
    ms = (jnp.dot(hi, gsum, preferred_element_type=F32) + jnp.dot(lo, gsum, preferred_element_type=F32))
    return x * lax.rsqrt(ms * (1.0 / SWA_HEAD_DIM) + RMS_EPS) * gain


def _swa_attend(q_groups, k_all, v_all, mask, sink_ref):
    tq = q_groups[0].shape[0]
    lane_head = lax.broadcasted_iota(jnp.int32, (tq, SWA_KV), 1) // SWA_HEAD_DIM
    mask_rows = jnp.concatenate([mask] * SWA_GROUP, axis=0)
    out = [jnp.zeros((tq, SWA_KV), F32) for _ in range(SWA_GROUP)]
    for kv in range(SWA_KV_HEADS):
        in_head = lane_head == kv
        qs = jnp.concatenate([jnp.where(in_head, qg, 0.0) for qg in q_groups], axis=0).astype(BF16)
        s = lax.dot_general(qs, k_all, (((1,), (1,)), ((), ())), preferred_element_type=F32)
        s = jnp.where(mask_rows, s, -jnp.inf)
        sink = jnp.concatenate(
            [jnp.full((tq, 1), sink_ref[kv * SWA_GROUP + g], F32) for g in range(SWA_GROUP)], axis=0)
        m = jnp.maximum(jnp.max(s, axis=1, keepdims=True), sink)
        p = jnp.exp(s - m)
        denom = jnp.sum(p, axis=1, keepdims=True) + jnp.exp(sink - m)
        pv = jnp.dot(p.astype(BF16), v_all, preferred_element_type=F32) / denom
        for g in range(SWA_GROUP):
            out[g] = jnp.where(in_head, pv[g * tq:(g + 1) * tq], out[g])
    return out


def _swa_prompt_kernel(sink_ref, cur_ref, prev_ref, gq_ref, gk_ref, gsum_ref, o_ref, k_ref, v_ref):
    n = pl.program_id(1)
    gsum = gsum_ref[...]
    k_cur = _head_group_norm(cur_ref[0, :, SWA_Q:SWA_Q + SWA_KV], gk_ref[...], gsum)
    k_prev = _head_group_norm(prev_ref[0, :, SWA_Q:SWA_Q + SWA_KV], gk_ref[...], gsum)
    v_cur = cur_ref[0, :, SWA_Q + SWA_KV:]
    v_prev = prev_ref[0, :, SWA_Q + SWA_KV:]
    k_all = jnp.concatenate([k_prev, k_cur], axis=0).astype(BF16)
    v_all = jnp.concatenate([v_prev, v_cur], axis=0).astype(BF16)
    t = lax.broadcasted_iota(jnp.int32, (WINDOW, 2 * WINDOW), 0)
    s = lax.broadcasted_iota(jnp.int32, (WINDOW, 2 * WINDOW), 1)
    mask = (s >= t) & (s <= t + WINDOW) & ((s >= WINDOW) | (n > 0))
    q_groups = [_head_group_norm(cur_ref[0, :, g * SWA_KV:(g + 1) * SWA_KV], gq_ref[...], gsum)
                * (SWA_HEAD_DIM ** -0.5) for g in range(SWA_GROUP)]
    out = _swa_attend(q_groups, k_all, v_all, mask, sink_ref)
    o_ref[0] = jnp.concatenate(out, axis=1)
    k_ref[0] = k_cur
    v_ref[0] = v_cur


def _swa_gsum():
    head = jnp.arange(SWA_KV) // SWA_HEAD_DIM
    return (head[:, None] == head[None, :]).astype(BF16)


def _swa_prompt(proj, gq, gk, sinks, batch, seq_len):
    W = WINDOW
    width = SWA_Q + 2 * SWA_KV
    p3 = proj.reshape(batch, seq_len, width)
    nbytes = 2 * (2 * W * width * 4 + W * SWA_Q * 4 + 2 * W * SWA_KV * 4) + 48 * W * 2 * W * 4
    o, k, v = pl.pallas_call(
        _swa_prompt_kernel,
        grid=(batch, seq_len // W),
        in_specs=[pl.BlockSpec(memory_space=pltpu.SMEM),
                  pl.BlockSpec((1, W, width), lambda b, n: (b, n, 0)),
                  pl.BlockSpec((1, W, width), lambda b, n: (b, jnp.maximum(n - 1, 0), 0)),
                  pl.BlockSpec((1, SWA_KV), lambda b, n: (0, 0)),
                  pl.BlockSpec((1, SWA_KV), lambda b, n: (0, 0)),
                  pl.BlockSpec((SWA_KV, SWA_KV), lambda b, n: (0, 0))],
        out_specs=[pl.BlockSpec((1, W, SWA_Q), lambda b, n: (b, n, 0)),
                   pl.BlockSpec((1, W, SWA_KV), lambda b, n: (b, 0, 0)),
                   pl.BlockSpec((1, W, SWA_KV), lambda b, n: (b, 0, 0))],
        out_shape=[jax.ShapeDtypeStruct((batch, seq_len, SWA_Q), F32),
                   jax.ShapeDtypeStruct((batch, W, SWA_KV), F32),
                   jax.ShapeDtypeStruct((batch, W, SWA_KV), F32)],
        compiler_params=_params(("parallel", "arbitrary"), nbytes),
        name="swa_prompt",
    )(sinks, p3, p3, jnp.tile(gq, SWA_KV_HEADS).reshape(1, SWA_KV), jnp.tile(gk, SWA_KV_HEADS).reshape(1, SWA_KV),
      _swa_gsum())
    return o.reshape(batch * seq_len, SWA_Q), k, v


SWA_SAMPLE_BATCH = 8


def _swa_sample_kernel(sink_ref, p_ref, kc_ref, vc_ref, gq_ref, gk_ref, gsum_ref, o_ref, k_ref, v_ref):
    T = p_ref.shape[1]
    gsum = gsum_ref[...]
    t = lax.broadcasted_iota(jnp.int32, (T, WINDOW + T), 0)
    s = lax.broadcasted_iota(jnp.int32, (T, WINDOW + T), 1)
    mask = (s >= t) & (s <= t + WINDOW)
    for j in range(p_ref.shape[0]):
        k_new = _head_group_norm(p_ref[j, :, SWA_Q:SWA_Q + SWA_KV], gk_ref[...], gsum)
        v_new = p_ref[j, :, SWA_Q + SWA_KV:]
        k_all = jnp.concatenate([kc_ref[j], k_new], axis=0)
        v_all = jnp.concatenate([vc_ref[j], v_new], axis=0)
        q_groups = [_head_group_norm(p_ref[j, :, g * SWA_KV:(g + 1) * SWA_KV], gq_ref[...], gsum)
                    * (SWA_HEAD_DIM ** -0.5) for g in range(SWA_GROUP)]
        out = _swa_attend(q_groups, k_all.astype(BF16), v_all.astype(BF16), mask, sink_ref)
        o_ref[j] = jnp.concatenate(out, axis=1)
        k_ref[j] = k_all[T:]
        v_ref[j] = v_all[T:]


def _swa_sample(proj, k_cache, v_cache, gq, gk, sinks, batch, seq_len):
    nb = SWA_SAMPLE_BATCH
    W = WINDOW
    width = SWA_Q + 2 * SWA_KV
    p3 = proj.reshape(batch, seq_len, width)
    nbytes = 2 * (nb * seq_len * (width + SWA_Q) * 4 + 4 * nb * W * SWA_KV * 4)
    o, k, v = pl.pallas_call(
        _swa_sample_kernel,
        grid=(batch // nb,),
        in_specs=[pl.BlockSpec(memory_space=pltpu.SMEM),
                  pl.BlockSpec((nb, seq_len, width), lambda b: (b, 0, 0)),
                  pl.BlockSpec((nb, W, SWA_KV), lambda b: (b, 0, 0)),
                  pl.BlockSpec((nb, W, SWA_KV), lambda b: (b, 0, 0)),
                  pl.BlockSpec((1, SWA_KV), lambda b: (0, 0)),
                  pl.BlockSpec((1, SWA_KV), lambda b: (0, 0)),
                  pl.BlockSpec((SWA_KV, SWA_KV), lambda b: (0, 0))],
        out_specs=[pl.BlockSpec((nb, seq_len, SWA_Q), lambda b: (b, 0, 0)),
                   pl.BlockSpec((nb, W, SWA_KV), lambda b: (b, 0, 0)),
                   pl.BlockSpec((nb, W, SWA_KV), lambda b: (b, 0, 0))],
        out_shape=[jax.ShapeDtypeStruct((batch, seq_len, SWA_Q), F32),
                   jax.ShapeDtypeStruct((batch, W, SWA_KV), F32),
                   jax.ShapeDtypeStruct((batch, W, SWA_KV), F32)],
        compiler_params=_params(("parallel",), nbytes),
        name="swa_sample",
    )(sinks, p3, k_cache, v_cache, jnp.tile(gq, SWA_KV_HEADS).reshape(1, SWA_KV),
      jnp.tile(gk, SWA_KV_HEADS).reshape(1, SWA_KV), _swa_gsum())
    return o.reshape(batch * seq_len, SWA_Q), k, v


def _lru_conv(x, shifted, cw_ref, cb_ref):
    y = cb_ref[...] + cw_ref[CONV_WIDTH - 1:CONV_WIDTH] * x
    for s in range(1, CONV_WIDTH):
        y = y + cw_ref[CONV_WIDTH - 1 - s:CONV_WIDTH - s] * shifted[s - 1]
    return y


def _block_diag_dot(x, w_ref):
    xb = x.astype(BF16)
    return jnp.concatenate(
        [jnp.dot(xb[:, n * LRU_BLOCK:(n + 1) * LRU_BLOCK], w_ref[n], preferred_element_type=F32)
         for n in range(LRU_BLOCKS)], axis=1)


def _lru_terms(xc, wga_ref, bga_ref, wgx_ref, bgx_ref, lam_ref):
    r = _sigmoid(_block_diag_dot(xc, wga_ref) + bga_ref[...])
    i = _sigmoid(_block_diag_dot(xc, wgx_ref) + bgx_ref[...])
    log_a = (-LRU_C) * r * _softplus(-lam_ref[...])
    a = jnp.exp(log_a)
    y2 = 2.0 * log_a
    u = a * a
    em1 = jnp.where(u == 1.0, y2, jnp.where(u == 0.0, -1.0, (u - 1.0) * y2 / jnp.log(u)))
    mult = jnp.sqrt(-em1)
    return a, mult * i * xc


def _scan_rows(a, b, group):
    rows = a.shape[0]
    pos = lax.broadcasted_iota(jnp.int32, a.shape, 0) % group
    d = 1
    while d < group:
        keep = pos >= d
        b = jnp.where(keep, a * pltpu.roll(b, d, 0) + b, b)
        a = jnp.where(keep, a * pltpu.roll(a, d, 0), a)
        d *= 2
    return a, b


def _lru_prompt_kernel(p_ref, cw_ref, cb_ref, wga_ref, bga_ref, wgx_ref, bgx_ref, lam_ref,
                       o_ref, h_ref, tail_ref, hc_ref):
    n = pl.program_id(1)
    rows = p_ref.shape[1]

    @pl.when(n == 0)
    def _():
        tail_ref[...] = jnp.zeros_like(tail_ref)
        hc_ref[...] = jnp.zeros_like(hc_ref)

    y = p_ref[0, :, :D_RNN]
    x = p_ref[0, :, D_RNN:]
    tail = tail_ref[...]
    r8 = lax.broadcasted_iota(jnp.int32, (V7X_SUBLANES, D_RNN), 0)
    shifted = []
    for s in range(1, CONV_WIDTH):
        xs = pltpu.roll(x, s, 0)
        head = jnp.where(r8 < s, pltpu.roll(tail, s, 0), xs[:V7X_SUBLANES])
        shifted.append(jnp.concatenate([head, xs[V7X_SUBLANES:]], axis=0))
    xc = _lru_conv(x, shifted, cw_ref, cb_ref)
    a, bterm = _lru_terms(xc, wga_ref, bga_ref, wgx_ref, bgx_ref, lam_ref)
    acum, hzero = _scan_rows(a, bterm, rows)
    hs = acum * hc_ref[0:1] + hzero
    o_ref[0] = _gelu_tanh(y) * hs
    last = hs[rows - 1:rows]
    h_ref[0] = last
    hc_ref[...] = jnp.broadcast_to(last, hc_ref.shape)
    tail_ref[...] = x[rows - V7X_SUBLANES:]


LRU_TILE = 256


def _lru_weight_specs(imap):
    return [pl.BlockSpec((CONV_WIDTH, D_RNN), imap(2)),
            pl.BlockSpec((1, D_RNN), imap(2)),
            pl.BlockSpec((LRU_BLOCKS, LRU_BLOCK, LRU_BLOCK), imap(3)),
            pl.BlockSpec((1, D_RNN), imap(2)),
            pl.BlockSpec((LRU_BLOCKS, LRU_BLOCK, LRU_BLOCK), imap(3)),
            pl.BlockSpec((1, D_RNN), imap(2)),
            pl.BlockSpec((1, D_RNN), imap(2))]


def _lru_prompt(proj, cw, cb, wga, bga, wgx, bgx, lam, batch, seq_len):
    R = LRU_TILE
    p3 = proj.reshape(batch, seq_len, 2 * D_RNN)
    nbytes = 2 * (R * 3 * D_RNN * 4 + 2 * LRU_BLOCKS * LRU_BLOCK * LRU_BLOCK * 2) + 24 * R * D_RNN * 4
    o, h = pl.pallas_call(
        _lru_prompt_kernel,
        grid=(batch, seq_len // R),
        in_specs=[pl.BlockSpec((1, R, 2 * D_RNN), lambda b, n: (b, n, 0))]
        + _lru_weight_specs(lambda nd: (lambda b, n: (0,) * nd)),
        out_specs=[pl.BlockSpec((1, R, D_RNN), lambda b, n: (b, n, 0)),
                   pl.BlockSpec((1, 1, D_RNN), lambda b, n: (b, 0, 0))],
        out_shape=[jax.ShapeDtypeStruct((batch, seq_len, D_RNN), F32),
                   jax.ShapeDtypeStruct((batch, 1, D_RNN), F32)],
        scratch_shapes=[pltpu.VMEM((V7X_SUBLANES, D_RNN), F32), pltpu.VMEM((V7X_SUBLANES, D_RNN), F32)],
        compiler_params=_params(("parallel", "arbitrary"), nbytes),
        name="lru_prompt",
    )(p3, cw, cb.reshape(1, D_RNN), wga.astype(BF16), bga.reshape(1, D_RNN), wgx.astype(BF16),
      bgx.reshape(1, D_RNN), lam.reshape(1, D_RNN))
    return o.reshape(batch * seq_len, D_RNN), h.reshape(batch, D_RNN)


def _lru_sample_kernel(p_ref, prev_ref, h0_ref, cw_ref, cb_ref, wga_ref, bga_ref, wgx_ref, bgx_ref, lam_ref,
                       o_ref, hs_ref, *, seq_len):
    rows = p_ref.shape[0]
    y = p_ref[:, :D_RNN]
    x = p_ref[:, D_RNN:]
    prev = prev_ref[...]
    pos = lax.broadcasted_iota(jnp.int32, (rows, D_RNN), 0) % seq_len
    shifted = [jnp.where(pos < s, pltpu.roll(prev, rows - seq_len + s, 0), pltpu.roll(x, s, 0))
               for s in range(1, CONV_WIDTH)]
    xc = _lru_conv(x, shifted, cw_ref, cb_ref)
    a, bterm = _lru_terms(xc, wga_ref, bga_ref, wgx_ref, bgx_ref, lam_ref)
    acum, hzero = _scan_rows(a, bterm, seq_len)
    hs = acum * h0_ref[...] + hzero
    o_ref[...] = _gelu_tanh(y) * hs
    hs_ref[...] = hs


def _lru_sample(proj, conv_state, h0, cw, cb, wga, bga, wgx, bgx, lam, batch, seq_len):
    assert seq_len == V7X_SUBLANES
    n = batch * seq_len
    R = LRU_TILE
    prev = jnp.pad(conv_state, ((0, 0), (seq_len - (CONV_WIDTH - 1), 0), (0, 0))).reshape(n, D_RNN)
    h0_rows = jnp.repeat(h0, seq_len, axis=0)
    nbytes = 2 * (R * 6 * D_RNN * 4 + 2 * LRU_BLOCKS * LRU_BLOCK * LRU_BLOCK * 2) + 24 * R * D_RNN * 4
    return pl.pallas_call(
        functools.partial(_lru_sample_kernel, seq_len=seq_len),
        grid=(n // R,),
        in_specs=[pl.BlockSpec((R, 2 * D_RNN), lambda i: (i, 0)),
                  pl.BlockSpec((R, D_RNN), lambda i: (i, 0)),
                  pl.BlockSpec((R, D_RNN), lambda i: (i, 0))]
        + _lru_weight_specs(lambda nd: (lambda i: (0,) * nd)),
        out_specs=[pl.BlockSpec((R, D_RNN), lambda i: (i, 0)),
                   pl.BlockSpec((R, D_RNN), lambda i: (i, 0))],
        out_shape=[jax.ShapeDtypeStruct((n, D_RNN), F32), jax.ShapeDtypeStruct((n, D_RNN), F32)],
        compiler_params=_params(("parallel",), nbytes),
        name="lru_sample",
    )(proj, prev, h0_rows, cw, cb.reshape(1, D_RNN), wga.astype(BF16), bga.reshape(1, D_RNN), wgx.astype(BF16),
      bgx.reshape(1, D_RNN), lam.reshape(1, D_RNN))


def _run(gen):
    while True:
        try:
            next(gen)
        except StopIteration as stop:
            return stop.value


def _lockstep(gens):
    results = [None] * len(gens)
    live = list(range(len(gens)))
    anchor = None
    while live:
        for idx in list(live):
            try:
                gens[idx].send(anchor)
            except StopIteration as stop:
                results[idx] = stop.value
                live.remove(idx)
        if live:
            anchor = yield
    return results


def _topk_rows_steps(s, k):
    n = s.shape[0]
    rid = lax.broadcasted_iota(jnp.int32, s.shape, 0).astype(F32)
    vals, ids = [], []
    for _ in range(k):
        m = jnp.max(s, axis=0, keepdims=True)
        ix = jnp.min(jnp.where(s == m, rid, float(n)), axis=0, keepdims=True)
        vals.append(m)
        ids.append(ix)
        s = jnp.where(rid == ix, -jnp.inf, s)
        anchor = yield
        if anchor is not None:
            s = s + anchor
    return jnp.concatenate(vals, axis=0), jnp.concatenate(ids, axis=0).astype(jnp.int32)


def _zero_from(parts):
    acc = None
    for x in parts:
        bits = pltpu.bitcast(x, jnp.uint32)
        bits = bits.reshape(bits.shape[0] // V7X_SUBLANES, V7X_SUBLANES, bits.shape[1])
        folded = bits[0]
        for r in range(1, bits.shape[0]):
            folded = folded | bits[r]
        acc = folded if acc is None else acc | folded
    cols = [acc[:, t * V7X_LANES:(t + 1) * V7X_LANES] for t in range(acc.shape[1] // V7X_LANES)]
    one = cols[0]
    for t in cols[1:]:
        one = one | t
    zero = lax.shift_right_logical(lax.shift_right_logical(one, jnp.uint32(16)), jnp.uint32(16))
    return pltpu.bitcast(zero, F32)[0:1, 0:1]


def _staircase_candidates(s1, s2):
    K = s1.shape[0]
    sub = V7X_SUBLANES
    first_single = next(a for a in range(K) if K // (a + 1) == 1)
    assert (K - first_single) % sub == 0
    pieces, starts, at = [], [], 0
    for a in range(first_single):
        nb = K // (a + 1)
        rows = -(-nb // sub) * sub
        piece = s1[a:a + 1] + s2[:rows]
        if rows != nb:
            piece = jnp.where(lax.broadcasted_iota(jnp.int32, piece.shape, 0) < nb, piece, -jnp.inf)
        pieces.append(piece)
        starts.append(at)
        at += rows
    pieces.append(s1[first_single:] + s2[0:1])
    return jnp.concatenate(pieces, axis=0), starts, at


ROUTE_STEPS = 2 + 2 * PEER_TOPK


def _route_head_steps(q_ref, sk_ref, stage_ref):
    K = PEER_TOPK
    par = pl.program_id(1) % 2
    for p in range(2):
        qh = q_ref[:, p * PEER_HALF:(p + 1) * PEER_HALF].astype(BF16)
        stage_ref[par, p] = lax.dot_general(sk_ref[0, p].astype(BF16), qh, (((1,), (1,)), ((), ())),
                                            preferred_element_type=F32)
    anchor = yield
    scores = [stage_ref[par, 0], stage_ref[par, 1]]
    if anchor is not None:
        scores = [st + anchor for st in scores]
    (s1, i1), (s2, i2) = yield from _lockstep([_topk_rows_steps(st, K) for st in scores])
    anchor = yield
    cand, starts, single_start = _staircase_candidates(s1, s2)
    if anchor is not None:
        cand = cand + anchor
    top, ci = yield from _topk_rows_steps(cand, K)
    a_id = jnp.zeros_like(ci)
    group_start = jnp.zeros_like(ci)
    for a in range(1, len(starts)):
        a_id = jnp.where(ci >= starts[a], a, a_id)
        group_start = jnp.where(ci >= starts[a], starts[a], group_start)
    single = ci >= single_start
    a_id = jnp.where(single, len(starts) + ci - single_start, a_id)
    b_id = jnp.where(single, 0, ci - group_start)
    e1 = jnp.zeros_like(ci)
    e2 = jnp.zeros_like(ci)
    for a in range(K):
        e1 = jnp.where(a_id == a, i1[a:a + 1], e1)
        e2 = jnp.where(b_id == a, i2[a:a + 1], e2)
    e = jnp.exp(top - top[0:1])
    return e1, e2, e / jnp.sum(e, axis=0, keepdims=True)


def _route_kernel(q_ref, sk_ref, e1_ref, e2_ref, g_ref, stage_ref):
    e1_ref[0], e2_ref[0], g_ref[0] = _run(_route_head_steps(q_ref, sk_ref, stage_ref))


EXPERT_TILE = 256


def _peer_route(q, sub_keys):
    n = q.shape[0]
    tb = EXPERT_TILE
    spec = pl.BlockSpec((1, PEER_TOPK, tb), lambda i, h: (h, 0, i))
    nbytes = 2 * (tb * PEER_KEY_DIM * 4 + 2 * PEER_NKEYS * PEER_HALF * 4) + 16 * 2 * PEER_NKEYS * tb * 4
    return pl.pallas_call(
        _route_kernel,
        grid=(n // tb, PEER_HEADS),
        in_specs=[pl.BlockSpec((tb, PEER_KEY_DIM), lambda i, h: (i, h)),
                  pl.BlockSpec((1, 2, PEER_NKEYS, PEER_HALF), lambda i, h: (h, 0, 0, 0))],
        out_specs=[spec, spec, spec],
        out_shape=[jax.ShapeDtypeStruct((PEER_HEADS, PEER_TOPK, n), jnp.int32),
                   jax.ShapeDtypeStruct((PEER_HEADS, PEER_TOPK, n), jnp.int32),
                   jax.ShapeDtypeStruct((PEER_HEADS, PEER_TOPK, n), F32)],
        scratch_shapes=[pltpu.VMEM((2, 2, PEER_NKEYS, tb), F32)],
        compiler_params=_params(("parallel", "parallel"), nbytes),
        name="peer_route",
    )(q, sub_keys)


EXPERT_CHUNK = PEER_EXPERTS // PEER_HEADS
EXPERT_SUB = 256
WEIGHT_BLOCK = 512
GATE_ROW_PAD = V7X_SUBLANES
GATE_UNROLL = 16


def _expert_kernel(h_ref, q_ref, sk_ref, e1_ref, e2_ref, g_ref, ut_ref, v_ref, x_ref, gt_ref, o_ref,
                   gate_ref, acc_ref, e1t_ref, e2t_ref, gtt_ref, re1_ref, re2_ref, rg_ref, stage_ref, w_ref):
    i = pl.program_id(0)
    c = pl.program_id(1)
    tb = h_ref.shape[0]
    NK = PEER_NKEYS
    stride = tb + GATE_ROW_PAD
    slot = i % 2

    @pl.when((i == 0) & (c == 0))
    def _():
        re1_ref[0] = e1_ref[...].reshape(PEER_PAIRS, tb)
        re2_ref[0] = e2_ref[...].reshape(PEER_PAIRS, tb)
        rg_ref[0] = g_ref[...].reshape(PEER_PAIRS, tb)

    @pl.when(c == 0)
    def _():
        acc_ref[...] = jnp.zeros_like(acc_ref)
        e1t_ref[...] = jnp.transpose(re1_ref[slot])
        e2t_ref[...] = jnp.transpose(re2_ref[slot])
        gtt_ref[...] = jnp.transpose(rg_ref[slot])
        kid = lax.broadcasted_iota(jnp.int32, (NK, PEER_PAIRS), 0)

        def per_token(n, carry):
            i1 = e1t_ref[pl.ds(n, 1), :]
            i2 = e2t_ref[pl.ds(n, 1), :]
            gg = gtt_ref[pl.ds(n, 1), :]
            a_t = jnp.where(kid == i1, 1.0, 0.0).astype(BF16)
            b_t = jnp.where(kid == i2, 0.5 * gg, 0.0).astype(BF16)
            gn = lax.dot_general(a_t, b_t, (((1,), (1,)), ((), ())), preferred_element_type=F32)
            gate_ref[pl.ds(n, NK, stride=stride), :] = gn
            return carry

        lax.fori_loop(0, tb, per_token, 0, unroll=GATE_UNROLL)

    route = _route_head_steps(q_ref, sk_ref, stage_ref)
    routed = []
    n_sub = EXPERT_CHUNK // EXPERT_SUB

    def advance_route(rounds, anchor=None):
        for _ in range(rounds):
            if routed:
                return
            try:
                route.send(anchor)
            except StopIteration as stop:
                routed.append(stop.value)
            anchor = None

    h = h_ref[...]
    keys_per_chunk = EXPERT_CHUNK // NK
    advance_route(1)
    WB = WEIGHT_BLOCK
    w_parts = []
    for j in range(n_sub):
        lo = j * EXPERT_SUB
        blk, off = divmod(lo, WB)
        s = jnp.dot(h, ut_ref[blk, :, off:off + EXPERT_SUB], preferred_element_type=F32)
        gsel = jnp.concatenate(
            [gate_ref[pl.ds(pl.multiple_of((c * keys_per_chunk + lo // NK + t) * stride, V7X_SUBLANES), tb), :]
             for t in range(EXPERT_SUB // NK)], axis=1)
        wb = _gelu_times_half_gate(s, gsel).astype(BF16)
        w_parts.append(wb)
        w_ref[blk, :, off:off + EXPERT_SUB] = wb
    anchor = _zero_from(w_parts)
    for n in range(v_ref.shape[0]):
        part = None
        for k in range(EXPERT_CHUNK // WB):
            d = jnp.dot(w_ref[k], v_ref[n, k * WB:(k + 1) * WB, :], preferred_element_type=F32)
            part = d if part is None else part + d
        acc_ref[:, n * WB:(n + 1) * WB] += part
    advance_route(ROUTE_STEPS + 1, anchor)
    n1, n2, ng = routed[0]
    rows = pl.ds(pl.multiple_of(c * PEER_TOPK, PEER_TOPK), PEER_TOPK)
    re1_ref[1 - slot, rows, :] = n1
    re2_ref[1 - slot, rows, :] = n2
    rg_ref[1 - slot, rows, :] = ng

    @pl.when(c == pl.num_programs(1) - 1)
    def _():
        o_ref[...] = x_ref[...] + gt_ref[0] * acc_ref[...]


def _peer_u_blocks(u):
    e, d = u.shape
    return u.reshape(e // WEIGHT_BLOCK, WEIGHT_BLOCK, d).transpose(0, 2, 1).astype(BF16)


def _peer_v_blocks(v):
    e, d = v.shape
    return v.reshape(e, d // WEIGHT_BLOCK, WEIGHT_BLOCK).transpose(1, 0, 2).astype(BF16)


def _peer_experts(h_bf16, q, sub_keys, u_blocks, v_blocks, x, gate, seq_len):
    n, d = x.shape
    tb = EXPERT_TILE
    ec = EXPERT_CHUNK
    wb = WEIGHT_BLOCK
    nt = n // tb
    e1, e2, g = _peer_route(q[:tb], sub_keys)
    gt_arr, gt_spec0 = _row_operand(gate, seq_len, tb)
    gt_spec = pl.BlockSpec(gt_spec0.block_shape, lambda i, c: gt_spec0.index_map(i))
    rspec = pl.BlockSpec((PEER_HEADS, PEER_TOPK, tb), lambda i, c: (0, 0, 0))
    nbytes = (2 * (tb * d * 2 + 3 * PEER_PAIRS * tb * 4 + 2 * d * ec * 2 + 3 * tb * d * 4 + tb * PEER_KEY_DIM * 4)
              + (tb + GATE_ROW_PAD) * PEER_NKEYS * PEER_NKEYS * 4 + tb * d * 4 + 9 * tb * PEER_PAIRS * 4
              + 8 * tb * EXPERT_SUB * 4 + 8 * PEER_NKEYS * tb * 4)
    return pl.pallas_call(
        _expert_kernel,
        grid=(nt, PEER_HEADS),
        in_specs=[pl.BlockSpec((tb, d), lambda i, c: (i, 0)),
                  pl.BlockSpec((tb, PEER_KEY_DIM), lambda i, c: (jnp.minimum(i + 1, nt - 1), c)),
                  pl.BlockSpec((1, 2, PEER_NKEYS, PEER_HALF), lambda i, c: (c, 0, 0, 0)),
                  rspec, rspec, rspec,
                  pl.BlockSpec((ec // wb, d, wb), lambda i, c: (c, 0, 0)),
                  pl.BlockSpec((d // wb, ec, wb), lambda i, c: (0, c, 0)),
                  pl.BlockSpec((tb, d), lambda i, c: (i, 0)),
                  gt_spec],
        out_specs=pl.BlockSpec((tb, d), lambda i, c: (i, 0)),
        out_shape=jax.ShapeDtypeStruct((n, d), F32),
        scratch_shapes=[pltpu.VMEM(((tb + GATE_ROW_PAD) * PEER_NKEYS, PEER_NKEYS), F32),
                        pltpu.VMEM((tb, d), F32),
                        pltpu.VMEM((tb, PEER_PAIRS), jnp.int32),
                        pltpu.VMEM((tb, PEER_PAIRS), jnp.int32),
                        pltpu.VMEM((tb, PEER_PAIRS), F32),
                        pltpu.VMEM((2, PEER_PAIRS, tb), jnp.int32),
                        pltpu.VMEM((2, PEER_PAIRS, tb), jnp.int32),
                        pltpu.VMEM((2, PEER_PAIRS, tb), F32),
                        pltpu.VMEM((2, 2, PEER_NKEYS, tb), F32),
                        pltpu.VMEM((ec // wb, tb, wb), BF16)],
        compiler_params=_params(("arbitrary", "arbitrary"), nbytes),
        name="peer_experts",
    )(h_bf16, q, sub_keys, e1, e2, g, u_blocks, v_blocks, x, gt_arr)


def _swa_permute_in(w_in):
    d = w_in.shape[0]
    wq = w_in[:, :SWA_Q].reshape(d, SWA_KV_HEADS, SWA_GROUP, SWA_HEAD_DIM).transpose(0, 2, 1, 3).reshape(d, SWA_Q)
    return jnp.concatenate([wq, w_in[:, SWA_Q:]], axis=1)


def _swa_permute_out(w_out):
    d = w_out.shape[1]
    return w_out.reshape(SWA_KV_HEADS, SWA_GROUP, SWA_HEAD_DIM, d).transpose(1, 0, 2, 3).reshape(SWA_Q, d)


def _prepare_weights(p):
    w = {}
    w['gla_in'] = [jnp.pad(p['w_gla_in'][j], ((0, 0), (0, GLA_IN_PAD - p['w_gla_in'].shape[2]))).astype(BF16)
                   for j in range(p['w_gla_in'].shape[0])]
    w['gla_a2'] = [jnp.pad(p['w_gla_a2'][j], ((0, V7X_LANES - GLA_GATE_RANK), (0, 0)))
                   for j in range(p['w_gla_a2'].shape[0])]
    w['gla_out'] = [m.astype(BF16) for m in p['w_gla_out']]
    w['swa_in'] = [_swa_permute_in(m).astype(BF16) for m in p['w_swa_in']]
    w['swa_out'] = [_swa_permute_out(m).astype(BF16) for m in p['w_swa_out']]
    w['lru_in'] = [m.astype(BF16) for m in p['w_lru_in']]
    w['lru_out'] = [m.astype(BF16) for m in p['w_lru_out']]
    w['peer_q'] = [m.astype(BF16) for m in p['w_peer_q']]
    w['peer_ut'] = [_peer_u_blocks(m) for m in p['peer_u']]
    w['peer_v'] = [_peer_v_blocks(m) for m in p['peer_v']]
    return w


def _trunk(x3, mod, states, p, w):
    batch, seq_len, d = x3.shape
    x = x3.reshape(batch * seq_len, d)
    new_gla, new_k, new_v, new_conv, new_h = [], [], [], [], []
    gla_stack = None
    for i in range(DEPTH):
        kind, j = i % N_MIXERS, i // N_MIXERS
        sh_m, sc_m, gt_m, sh_f, sc_f, gt_f = [mod[i][:, k * d:(k + 1) * d] for k in range(6)]
        if kind == 0:
            proj = _norm_proj(x, p['g_ln_mix'][i], sc_m, sh_m, w['gla_in'][j], seq_len)
            if states is None:
                mix, s_new = _gla_prompt(proj, w['gla_a2'][j], p['b_gla_a'][j], p['g_gla_norm'][j], batch, seq_len)
                new_gla.append(s_new)
            else:
                mix, gla_stack = _gla_sample(proj, states[0], j, gla_stack, w['gla_a2'][j], p['b_gla_a'][j],
                                             p['g_gla_norm'][j], batch, seq_len)
            w_out = w['gla_out'][j]
        elif kind == 1:
            proj = _norm_proj(x, p['g_ln_mix'][i], sc_m, sh_m, w['swa_in'][j], seq_len)
            if states is None:
                mix, k_n, v_n = _swa_prompt(proj, p['g_swa_q'][j], p['g_swa_k'][j], p['swa_sinks'][j], batch, seq_len)
            else:
                kc = states[1][j].reshape(batch, WINDOW, SWA_KV)
                vc = states[2][j].reshape(batch, WINDOW, SWA_KV)
                mix, k_n, v_n = _swa_sample(proj, kc, vc, p['g_swa_q'][j], p['g_swa_k'][j], p['swa_sinks'][j],
                                            batch, seq_len)
            new_k.append(k_n.reshape(batch, WINDOW, SWA_KV_HEADS, SWA_HEAD_DIM))
            new_v.append(v_n.reshape(batch, WINDOW, SWA_KV_HEADS, SWA_HEAD_DIM))
            w_out = w['swa_out'][j]
        else:
            proj = _norm_proj(x, p['g_ln_mix'][i], sc_m, sh_m, w['lru_in'][j], seq_len)
            lru_args = (p['lru_conv_w'][j], p['lru_conv_b'][j], p['w_lru_ga'][j], p['b_lru_ga'][j],
                        p['w_lru_gx'][j], p['b_lru_gx'][j], p['lru_lam'][j], batch, seq_len)
            assert seq_len >= CONV_WIDTH - 1
            if states is None:
                mix, h_n = _lru_prompt(proj, *lru_args)
            else:
                mix, hs = _lru_sample(proj, states[3][j], states[4][j], *lru_args)
                h_n = hs.reshape(batch, seq_len, D_RNN)[:, -1]
            new_conv.append(proj[:, D_RNN:].reshape(batch, seq_len, D_RNN)[:, seq_len - (CONV_WIDTH - 1):])
            new_h.append(h_n)
            w_out = w['lru_out'][j]
        x = _proj_residual(mix, w_out, x, gt_m, seq_len)
        q, hb = _norm_proj(x, p['g_ln_ffn'][i], sc_f, sh_f, w['peer_q'][i], seq_len, with_h=True)
        x = _peer_experts(hb, q, p['peer_sub_keys'][i], w['peer_ut'][i], w['peer_v'][i], x, gt_f, seq_len)
    y = x.reshape(batch, seq_len, d)
    gla_out = jnp.stack(new_gla) if states is None else gla_stack
    return y, (gla_out, jnp.stack(new_k), jnp.stack(new_v), jnp.stack(new_conv), jnp.stack(new_h))


def kernel(x_prompt, x_sample, state_gla, cache_swa_k, cache_swa_v, state_lru_conv, state_lru_h,
           c_prompt, c_sample, g_ln_mix, g_ln_ffn, w_mod, b_mod,
           w_gla_in, w_gla_a2, b_gla_a, g_gla_norm, w_gla_out,
           w_swa_in, g_swa_q, g_swa_k, swa_sinks, w_swa_out,
           w_lru_in, lru_conv_w, lru_conv_b, w_lru_ga, b_lru_ga, w_lru_gx, b_lru_gx, lru_lam, w_lru_out,
           w_peer_q, peer_sub_keys, peer_u, peer_v):
    p = {'g_ln_mix': g_ln_mix, 'g_ln_ffn': g_ln_ffn,
         'w_gla_in': w_gla_in, 'w_gla_a2': w_gla_a2, 'b_gla_a': b_gla_a, 'g_gla_norm': g_gla_norm,
         'w_gla_out': w_gla_out,
         'w_swa_in': w_swa_in, 'g_swa_q': g_swa_q, 'g_swa_k': g_swa_k, 'swa_sinks': swa_sinks,
         'w_swa_out': w_swa_out,
         'w_lru_in': w_lru_in, 'lru_conv_w': lru_conv_w, 'lru_conv_b': lru_conv_b,
         'w_lru_ga': w_lru_ga, 'b_lru_ga': b_lru_ga, 'w_lru_gx': w_lru_gx, 'b_lru_gx': b_lru_gx,
         'lru_lam': lru_lam, 'w_lru_out': w_lru_out,
         'w_peer_q': w_peer_q, 'peer_sub_keys': peer_sub_keys, 'peer_u': peer_u, 'peer_v': peer_v}
    w = _prepare_weights(p)
    nb_p, nb_s = c_prompt.shape[0], c_sample.shape[0]
    rows = -(-(nb_p + nb_s) // V7X_SUBLANES) * V7X_SUBLANES
    c_all = jnp.pad(jnp.concatenate([c_prompt, c_sample], axis=0), ((0, rows - nb_p - nb_s), (0, 0)))
    mod = _modulation(c_all, w_mod, b_mod)
    y_p, (gla_p, k_p, v_p, conv_p, h_p) = _trunk(x_prompt, mod[:, :nb_p], None, p, w)
    y_s, (gla_s, k_s, v_s, conv_s, h_s) = _trunk(
        x_sample, mod[:, nb_p:nb_p + nb_s],
        (state_gla, cache_swa_k, cache_swa_v, state_lru_conv, state_lru_h), p, w)
    return (y_p, y_s, gla_p, gla_s, k_p, k_s, v_p, v_s, conv_p, conv_s, h_p, h_s)
```

```python
import functools
import math

import jax
import jax.numpy as jnp
from jax import lax
from jax.experimental import pallas as pl
from jax.experimental.pallas import tpu as pltpu

F32 = jnp.float32
BF16 = jnp.bfloat16

D_MODEL = 1024
DEPTH = 4
N_MIXERS = 3
RMS_EPS = 1e-6

GLA_HEADS = 4
GLA_QK = D_MODEL // 2
GLA_V = D_MODEL
GLA_DK = GLA_QK // GLA_HEADS
GLA_DV = GLA_V // GLA_HEADS
GLA_GATE_RANK = 16
GLA_TAU = 16.0
GLA_SUB = 16
GLA_CHUNK = 128
GLA_IN_PAD = 2 * GLA_QK + 2 * GLA_V + 128

SWA_HEAD_DIM = 64
SWA_Q_HEADS = D_MODEL // SWA_HEAD_DIM
SWA_KV_HEADS = 4
SWA_GROUP = SWA_Q_HEADS // SWA_KV_HEADS
SWA_Q = SWA_Q_HEADS * SWA_HEAD_DIM
SWA_KV = SWA_KV_HEADS * SWA_HEAD_DIM
WINDOW = 128

D_RNN = D_MODEL
LRU_BLOCKS = 4
LRU_BLOCK = D_RNN // LRU_BLOCKS
CONV_WIDTH = 4
LRU_C = 8.0

PEER_HEADS = 8
PEER_NKEYS = 128
PEER_EXPERTS = PEER_NKEYS * PEER_NKEYS
PEER_KEY_DIM = 256
PEER_HALF = PEER_KEY_DIM // 2
PEER_TOPK = 16
PEER_PAIRS = PEER_HEADS * PEER_TOPK

V7X_LANES = 128
V7X_SUBLANES = 8
V7X_VMEM_BYTES = 64 * 1024 * 1024

TOKEN_TILE = 256
NEG_BIG = -1e30


def _vmem_limit(nbytes):
    return int(min(max(nbytes * 3 // 2, 16 * 1024 * 1024), V7X_VMEM_BYTES - 8 * 1024 * 1024))


def _params(semantics, nbytes):
    return pltpu.CompilerParams(dimension_semantics=semantics, vmem_limit_bytes=_vmem_limit(nbytes))


def _rms(x, g):
    return x * lax.rsqrt(jnp.mean(x * x, axis=-1, keepdims=True) + RMS_EPS) * g


def _gelu_tanh(x):
    return 0.5 * x * (1.0 + jnp.tanh(math.sqrt(2.0 / math.pi) * (x + 0.044715 * (x * x * x))))


def _gelu_times_half_gate(x, half_gate):
    c1 = math.sqrt(2.0 / math.pi)
    inner = x * (c1 + (c1 * 0.044715) * (x * x))
    return (x * half_gate) * (1.0 + jnp.tanh(inner))


def _sigmoid(x):
    return 1.0 / (1.0 + jnp.exp(-x))


def _softplus(x):
    return jnp.maximum(x, 0.0) + jnp.log1p(jnp.exp(-jnp.abs(x)))


def _row_operand(vec, seq_len, tile):
    b, d = vec.shape
    if seq_len % tile == 0:
        per_seq = seq_len // tile
        return vec.reshape(b, 1, d), pl.BlockSpec((1, 1, d), lambda i: (i // per_seq, 0, 0))
    assert tile % seq_len == 0
    rep = jnp.repeat(vec, seq_len, axis=0).reshape(b * seq_len // tile, tile, d)
    return rep, pl.BlockSpec((1, tile, d), lambda i: (i, 0, 0))


def _mod_kernel(c_ref, w_ref, b_ref, o_ref):
    c = c_ref[...]
    sc = c * _sigmoid(c)
    o_ref[0] = jnp.dot(sc, w_ref[0], preferred_element_type=F32) + b_ref[0]


def _modulation(c, w_mod, b_mod):
    bp, d = c.shape
    tn = 1024
    nt = 6 * d // tn
    return pl.pallas_call(
        _mod_kernel,
        grid=(DEPTH, nt),
        in_specs=[pl.BlockSpec((bp, d), lambda l, j: (0, 0)),
                  pl.BlockSpec((1, d, tn), lambda l, j: (l, 0, j)),
                  pl.BlockSpec((1, 1, tn), lambda l, j: (l, 0, j))],
        out_specs=pl.BlockSpec((1, bp, tn), lambda l, j: (l, 0, j)),
        out_shape=jax.ShapeDtypeStruct((DEPTH, bp, 6 * d), F32),
        compiler_params=_params(("parallel", "parallel"), 2 * (d * tn * 4 + 2 * bp * tn * 4)),
        name="modulation",
    )(c, w_mod, b_mod.reshape(DEPTH, 1, 6 * d))


def _norm_proj_kernel(x_ref, g_ref, sc_ref, sh_ref, w_ref, o_ref, *h_ref):
    h = _rms(x_ref[...], g_ref[...]) * (1.0 + sc_ref[0]) + sh_ref[0]
    hb = h.astype(BF16)
    o_ref[...] = jnp.dot(hb, w_ref[...], preferred_element_type=F32).astype(o_ref.dtype)
    if h_ref:
        h_ref[0][...] = hb


def _norm_proj(x, g, scale, shift, w_bf16, seq_len, with_h=False):
    n, d = x.shape
    nout = w_bf16.shape[1]
    tm = TOKEN_TILE
    sc_arr, sc_spec = _row_operand(scale, seq_len, tm)
    sh_arr, sh_spec = _row_operand(shift, seq_len, tm)
    out_shape = [jax.ShapeDtypeStruct((n, nout), BF16 if with_h else F32)]
    out_specs = [pl.BlockSpec((tm, nout), lambda i: (i, 0))]
    if with_h:
        out_shape.append(jax.ShapeDtypeStruct((n, d), BF16))
        out_specs.append(pl.BlockSpec((tm, d), lambda i: (i, 0)))
    nbytes = 2 * (tm * d * 4 + d * nout * 2 + tm * nout * 4 + 3 * tm * d * 4)
    res = pl.pallas_call(
        _norm_proj_kernel,
        grid=(n // tm,),
        in_specs=[pl.BlockSpec((tm, d), lambda i: (i, 0)),
                  pl.BlockSpec((1, d), lambda i: (0, 0)),
                  sc_spec, sh_spec,
                  pl.BlockSpec((d, nout), lambda i: (0, 0))],
        out_specs=out_specs,
        out_shape=out_shape,
        compiler_params=_params(("parallel",), nbytes),
        name="norm_proj",
    )(x, g.reshape(1, d), sc_arr, sh_arr, w_bf16)
    return res if with_h else res[0]


def _proj_residual_kernel(a_ref, w_ref, x_ref, gt_ref, o_ref):
    y = jnp.dot(a_ref[...].astype(BF16), w_ref[...], preferred_element_type=F32)
    o_ref[...] = x_ref[...] + gt_ref[0] * y


def _proj_residual(a, w_bf16, x, gate, seq_len):
    n, k = a.shape
    d = x.shape[1]
    tm = TOKEN_TILE
    gt_arr, gt_spec = _row_operand(gate, seq_len, tm)
    nbytes = 2 * (tm * k * 4 + k * d * 2 + 3 * tm * d * 4)
    return pl.pallas_call(
        _proj_residual_kernel,
        grid=(n // tm,),
        in_specs=[pl.BlockSpec((tm, k), lambda i: (i, 0)),
                  pl.BlockSpec((k, d), lambda i: (0, 0)),
                  pl.BlockSpec((tm, d), lambda i: (i, 0)),
                  gt_spec],
        out_specs=pl.BlockSpec((tm, d), lambda i: (i, 0)),
        out_shape=jax.ShapeDtypeStruct((n, d), F32),
        compiler_params=_params(("parallel",), nbytes),
        name="proj_residual",
    )(a, w_bf16, x, gt_arr)


def _log_decay(lr, wa2, ba):
    z = jnp.dot(lr, wa2, preferred_element_type=F32) + ba
    return (jnp.minimum(z, 0.0) - jnp.log1p(jnp.exp(-jnp.abs(z)))) * (1.0 / GLA_TAU)


def _col_bcast(row):
    return jnp.transpose(jnp.broadcast_to(row, (V7X_LANES, V7X_LANES)))


def _head_out(o, gate, gn):
    return _rms(o, gn) * (gate * _sigmoid(gate))


def _gla_prompt_kernel(p_ref, wa2_ref, ba_ref, gn_ref, o_ref, s_ref):
    c = pl.program_id(1)
    C = GLA_CHUNK
    nsub = C // GLA_SUB

    @pl.when(c == 0)
    def _():
        s_ref[...] = jnp.zeros_like(s_ref)

    row = lax.broadcasted_iota(jnp.int32, (C, C), 0)
    col = lax.broadcasted_iota(jnp.int32, (C, C), 1)
    tri = (col <= row).astype(F32)
    later = ((col > row) & (col // GLA_SUB == row // GLA_SUB)).astype(F32)
    sums = jnp.concatenate([tri, later], axis=0)
    causal = col <= row
    rsub = lax.broadcasted_iota(jnp.int32, (C, GLA_DK), 0) // GLA_SUB

    lr = p_ref[0, :, 2 * GLA_QK + 2 * GLA_V:]
    la_all = _log_decay(lr, wa2_ref[...], ba_ref[...])
    cs_all = jnp.dot(sums, la_all, preferred_element_type=F32, precision=lax.Precision.HIGHEST)
    for h in range(GLA_HEADS):
        q = p_ref[0, :, h * GLA_DK:(h + 1) * GLA_DK] * (GLA_DK ** -0.5)
        k = p_ref[0, :, GLA_QK + h * GLA_DK:GLA_QK + (h + 1) * GLA_DK]
        v = p_ref[0, :, 2 * GLA_QK + h * GLA_DV:2 * GLA_QK + (h + 1) * GLA_DV]
        gate = p_ref[0, :, 2 * GLA_QK + GLA_V + h * GLA_DV:2 * GLA_QK + GLA_V + (h + 1) * GLA_DV]
        cs = cs_all[:, h * GLA_DK:(h + 1) * GLA_DK]
        b = cs[:C]
        to_sub_end = cs[C:]
        b_last = b[C - 1:C]
        k_sub = k * jnp.exp(to_sub_end)
        q_parts, k_parts = [], []
        for m in range(nsub):
            ref_row = b[m * GLA_SUB + GLA_SUB - 1:m * GLA_SUB + GLA_SUB]
            e = jnp.where(rsub >= m, b - ref_row, NEG_BIG)
            q_parts.append((q * jnp.exp(e)).astype(BF16))
            k_parts.append(jnp.where(rsub == m, k_sub, 0.0).astype(BF16))
        qcat = jnp.concatenate(q_parts, axis=1)
        kcat = jnp.concatenate(k_parts, axis=1)
        att = lax.dot_general(qcat, kcat, (((1,), (1,)), ((), ())), preferred_element_type=F32)
        att = jnp.where(causal, att, 0.0)
        s_old = s_ref[0, h]
        o = jnp.dot(att.astype(BF16), v.astype(BF16), preferred_element_type=F32)
        o = o + jnp.dot((q * jnp.exp(b)).astype(BF16), s_old.astype(BF16), preferred_element_type=F32)
        k_end = (k * jnp.exp(b_last - b)).astype(BF16)
        upd = lax.dot_general(k_end, v.astype(BF16), (((0,), (0,)), ((), ())), preferred_element_type=F32)
        decay = _col_bcast(jnp.exp(b_last))
        s_ref[0, h] = jnp.concatenate([decay] * (GLA_DV // V7X_LANES), axis=1) * s_old + upd
        o_ref[0, :, h * GLA_DV:(h + 1) * GLA_DV] = _head_out(o, gate, gn_ref[...])


def _gla_prompt(proj, wa2_pad, ba, gn, batch, seq_len):
    C = GLA_CHUNK
    p3 = proj.reshape(batch, seq_len, GLA_IN_PAD)
    nbytes = 2 * (C * GLA_IN_PAD * 4 + C * GLA_V * 4 + GLA_HEADS * GLA_DK * GLA_DV * 4) + 64 * C * C * 4
    o, s = pl.pallas_call(
        _gla_prompt_kernel,
        grid=(batch, seq_len // C),
        in_specs=[pl.BlockSpec((1, C, GLA_IN_PAD), lambda b, c: (b, c, 0)),
                  pl.BlockSpec((V7X_LANES, GLA_QK), lambda b, c: (0, 0)),
                  pl.BlockSpec((1, GLA_QK), lambda b, c: (0, 0)),
                  pl.BlockSpec((1, GLA_DV), lambda b, c: (0, 0))],
        out_specs=[pl.BlockSpec((1, C, GLA_V), lambda b, c: (b, c, 0)),
                   pl.BlockSpec((1, GLA_HEADS, GLA_DK, GLA_DV), lambda b, c: (b, 0, 0, 0))],
        out_shape=[jax.ShapeDtypeStruct((batch, seq_len, GLA_V), F32),
                   jax.ShapeDtypeStruct((batch, GLA_HEADS, GLA_DK, GLA_DV), F32)],
        compiler_params=_params(("parallel", "arbitrary"), nbytes),
        name="gla_prompt",
    )(p3, wa2_pad, ba.reshape(1, GLA_QK), gn.reshape(1, GLA_DV))
    return o.reshape(batch * seq_len, GLA_V), s


GLA_SAMPLE_BATCH = 8


def _gla_sample_kernel(p_ref, s0_ref, wa2_ref, ba_ref, gn_ref, *refs):
    o_ref, s_ref = refs[-2:]
    nseq = s0_ref.shape[1]
    rows = p_ref.shape[0]
    T = rows // nseq
    row = lax.broadcasted_iota(jnp.int32, (rows, rows), 0)
    col = lax.broadcasted_iota(jnp.int32, (rows, rows), 1)
    same_seq = (row // T) == (col // T)
    causal = same_seq & (col <= row)
    sums = jnp.concatenate([causal.astype(F32), same_seq.astype(F32)], axis=0)
    lr = p_ref[:, 2 * GLA_QK + 2 * GLA_V:]
    la = _log_decay(lr, wa2_ref[...], ba_ref[...])
    cs = jnp.dot(sums, la, preferred_element_type=F32, precision=lax.Precision.HIGHEST)
    b_all, b_last_all = cs[:rows], cs[rows:]
    for h in range(GLA_HEADS):
        hk = slice(h * GLA_DK, (h + 1) * GLA_DK)
        q = p_ref[:, h * GLA_DK:(h + 1) * GLA_DK] * (GLA_DK ** -0.5)
        k = p_ref[:, GLA_QK + h * GLA_DK:GLA_QK + (h + 1) * GLA_DK]
        v = p_ref[:, 2 * GLA_QK + h * GLA_DV:2 * GLA_QK + (h + 1) * GLA_DV]
        gate = p_ref[:, 2 * GLA_QK + GLA_V + h * GLA_DV:2 * GLA_QK + GLA_V + (h + 1) * GLA_DV]
        b, b_last = b_all[:, hk], b_last_all[:, hk]
        k_end = k * jnp.exp(b_last - b)
        q_rel = (q * jnp.exp(b - b_last)).astype(BF16)
        q_dec = q * jnp.exp(b)
        att = lax.dot_general(q_rel, k_end.astype(BF16), (((1,), (1,)), ((), ())), preferred_element_type=F32)
        att = jnp.where(causal, att, 0.0)
        o_intra = jnp.dot(att.astype(BF16), v.astype(BF16), preferred_element_type=F32)
        o_inter = []
        for j in range(nseq):
            rj = slice(j * T, (j + 1) * T)
            s_old = s0_ref[0, j, h]
            o_inter.append(jnp.dot(q_dec[rj].astype(BF16), s_old.astype(BF16), preferred_element_type=F32))
            upd = lax.dot_general(k_end[rj].astype(BF16), v[rj].astype(BF16), (((0,), (0,)), ((), ())),
                                  preferred_element_type=F32)
            decay = _col_bcast(jnp.exp(b_last[j * T:j * T + 1]))
            s_ref[0, j, h] = jnp.concatenate([decay] * (GLA_DV // V7X_LANES), axis=1) * s_old + upd
        o = o_intra + jnp.concatenate(o_inter, axis=0)
        o_ref[:, h * GLA_DV:(h + 1) * GLA_DV] = _head_out(o, gate, gn_ref[...])


def _gla_sample(proj, states, layer, new_states, wa2_pad, ba, gn, batch, seq_len):
    assert seq_len <= GLA_SUB
    nb = GLA_SAMPLE_BATCH
    rows = nb * seq_len
    state_spec = pl.BlockSpec((1, nb, GLA_HEADS, GLA_DK, GLA_DV), lambda b: (layer, b, 0, 0, 0))
    state_block = nb * GLA_HEADS * GLA_DK * GLA_DV * 4
    nbytes = 2 * (rows * (GLA_IN_PAD + GLA_V) * 4 + 2 * state_block) + 16 * rows * GLA_IN_PAD * 4
    in_specs = [pl.BlockSpec((rows, GLA_IN_PAD), lambda b: (b, 0)),
                state_spec,
                pl.BlockSpec((V7X_LANES, GLA_QK), lambda b: (0, 0)),
                pl.BlockSpec((1, GLA_QK), lambda b: (0, 0)),
                pl.BlockSpec((1, GLA_DV), lambda b: (0, 0))]
    operands = [proj, states, wa2_pad, ba.reshape(1, GLA_QK), gn.reshape(1, GLA_DV)]
    aliases = {}
    if new_states is not None:
        in_specs.append(pl.BlockSpec(memory_space=pl.ANY))
        operands.append(new_states)
        aliases = {len(operands) - 1: 1}
    o, s = pl.pallas_call(
        _gla_sample_kernel,
        grid=(batch // nb,),
        in_specs=in_specs,
        out_specs=[pl.BlockSpec((rows, GLA_V), lambda b: (b, 0)), state_spec],
        out_shape=[jax.ShapeDtypeStruct((batch * seq_len, GLA_V), F32),
                   jax.ShapeDtypeStruct(states.shape, F32)],
        input_output_aliases=aliases,
        compiler_params=_params(("parallel",), nbytes),
        name="gla_sample",
    )(*operands)
    return o, s


def _head_group_norm(x, gain, gsum):
    sq = x * x
    hi = sq.astype(BF16)
    lo = (sq - hi.astype(F32)).astype(BF16)
    ms = (jnp.dot(hi, gsum, preferred_element_type=F32) + jnp.dot(lo, gsum, preferred_element_type=F32))
    return x * lax.rsqrt(ms * (1.0 / SWA_HEAD_DIM) + RMS_EPS) * gain


def _swa_attend(q_groups, k_all, v_all, mask, sink_ref):
    tq = q_groups[0].shape[0]
    lane_head = lax.broadcasted_iota(jnp.int32, (tq, SWA_KV), 1) // SWA_HEAD_DIM
    mask_rows = jnp.concatenate([mask] * SWA_GROUP, axis=0)
    out = [jnp.zeros((tq, SWA_KV), F32) for _ in range(SWA_GROUP)]
    for kv in range(SWA_KV_HEADS):
        in_head = lane_head == kv
        qs = jnp.concatenate([jnp.where(in_head, qg, 0.0) for qg in q_groups], axis=0).astype(BF16)
        s = lax.dot_general(qs, k_all, (((1,), (1,)), ((), ())), preferred_element_type=F32)
        s = jnp.where(mask_rows, s, -jnp.inf)
        sink = jnp.concatenate(
            [jnp.full((tq, 1), sink_ref[kv * SWA_GROUP + g], F32) for g in range(SWA_GROUP)], axis=0)
        m = jnp.maximum(jnp.max(s, axis=1, keepdims=True), sink)
        p = jnp.exp(s - m)
        denom = jnp.sum(p, axis=1, keepdims=True) + jnp.exp(sink - m)
        pv = jnp.dot(p.astype(BF16), v_all, preferred_element_type=F32) / denom
        for g in range(SWA_GROUP):
            out[g] = jnp.where(in_head, pv[g * tq:(g + 1) * tq], out[g])
    return out


def _swa_prompt_kernel(sink_ref, cur_ref, prev_ref, gq_ref, gk_ref, gsum_ref, o_ref, k_ref, v_ref):
    n = pl.program_id(1)
    gsum = gsum_ref[...]
    k_cur = _head_group_norm(cur_ref[0, :, SWA_Q:SWA_Q + SWA_KV], gk_ref[...], gsum)
    k_prev = _head_group_norm(prev_ref[0, :, SWA_Q:SWA_Q + SWA_KV], gk_ref[...], gsum)
    v_cur = cur_ref[0, :, SWA_Q + SWA_KV:]
    v_prev = prev_ref[0, :, SWA_Q + SWA_KV:]
    k_all = jnp.concatenate([k_prev, k_cur], axis=0).astype(BF16)
    v_all = jnp.concatenate([v_prev, v_cur], axis=0).astype(BF16)
    t = lax.broadcasted_iota(jnp.int32, (WINDOW, 2 * WINDOW), 0)
    s = lax.broadcasted_iota(jnp.int32, (WINDOW, 2 * WINDOW), 1)
    mask = (s >= t) & (s <= t + WINDOW) & ((s >= WINDOW) | (n > 0))
    q_groups = [_head_group_norm(cur_ref[0, :, g * SWA_KV:(g + 1) * SWA_KV], gq_ref[...], gsum)
                * (SWA_HEAD_DIM ** -0.5) for g in range(SWA_GROUP)]
    out = _swa_attend(q_groups, k_all, v_all, mask, sink_ref)
    o_ref[0] = jnp.concatenate(out, axis=1)
    k_ref[0] = k_cur
    v_ref[0] = v_cur


def _swa_gsum():
    head = jnp.arange(SWA_KV) // SWA_HEAD_DIM
    return (head[:, None] == head[None, :]).astype(BF16)


def _swa_prompt(proj, gq, gk, sinks, batch, seq_len):
    W = WINDOW
    width = SWA_Q + 2 * SWA_KV
    p3 = proj.reshape(batch, seq_len, width)
    nbytes = 2 * (2 * W * width * 4 + W * SWA_Q * 4 + 2 * W * SWA_KV * 4) + 48 * W * 2 * W * 4
    o, k, v = pl.pallas_call(
        _swa_prompt_kernel,
        grid=(batch, seq_len // W),
        in_specs=[pl.BlockSpec(memory_space=pltpu.SMEM),
                  pl.BlockSpec((1, W, width), lambda b, n: (b, n, 0)),
                  pl.BlockSpec((1, W, width), lambda b, n: (b, jnp.maximum(n - 1, 0), 0)),
                  pl.BlockSpec((1, SWA_KV), lambda b, n: (0, 0)),
                  pl.BlockSpec((1, SWA_KV), lambda b, n: (0, 0)),
                  pl.BlockSpec((SWA_KV, SWA_KV), lambda b, n: (0, 0))],
        out_specs=[pl.BlockSpec((1, W, SWA_Q), lambda b, n: (b, n, 0)),
                   pl.BlockSpec((1, W, SWA_KV), lambda b, n: (b, 0, 0)),
                   pl.BlockSpec((1, W, SWA_KV), lambda b, n: (b, 0, 0))],
        out_shape=[jax.ShapeDtypeStruct((batch, seq_len, SWA_Q), F32),
                   jax.ShapeDtypeStruct((batch, W, SWA_KV), F32),
                   jax.ShapeDtypeStruct((batch, W, SWA_KV), F32)],
        compiler_params=_params(("parallel", "arbitrary"), nbytes),
        name="swa_prompt",
    )(sinks, p3, p3, jnp.tile(gq, SWA_KV_HEADS).reshape(1, SWA_KV), jnp.tile(gk, SWA_KV_HEADS).reshape(1, SWA_KV),
      _swa_gsum())
    return o.reshape(batch * seq_len, SWA_Q), k, v


SWA_SAMPLE_BATCH = 8


def _swa_sample_kernel(sink_ref, p_ref, kc_ref, vc_ref, gq_ref, gk_ref, gsum_ref, o_ref, k_ref, v_ref):
    T = p_ref.shape[1]
    gsum = gsum_ref[...]
    t = lax.broadcasted_iota(jnp.int32, (T, WINDOW + T), 0)
    s = lax.broadcasted_iota(jnp.int32, (T, WINDOW + T), 1)
    mask = (s >= t) & (s <= t + WINDOW)
    for j in range(p_ref.shape[0]):
        k_new = _head_group_norm(p_ref[j, :, SWA_Q:SWA_Q + SWA_KV], gk_ref[...], gsum)
        v_new = p_ref[j, :, SWA_Q + SWA_KV:]
        k_all = jnp.concatenate([kc_ref[j], k_new], axis=0)
        v_all = jnp.concatenate([vc_ref[j], v_new], axis=0)
        q_groups = [_head_group_norm(p_ref[j, :, g * SWA_KV:(g + 1) * SWA_KV], gq_ref[...], gsum)
                    * (SWA_HEAD_DIM ** -0.5) for g in range(SWA_GROUP)]
        out = _swa_attend(q_groups, k_all.astype(BF16), v_all.astype(BF16), mask, sink_ref)
        o_ref[j] = jnp.concatenate(out, axis=1)
        k_ref[j] = k_all[T:]
        v_ref[j] = v_all[T:]


def _swa_sample(proj, k_cache, v_cache, gq, gk, sinks, batch, seq_len):
    nb = SWA_SAMPLE_BATCH
    W = WINDOW
    width = SWA_Q + 2 * SWA_KV
    p3 = proj.reshape(batch, seq_len, width)
    nbytes = 2 * (nb * seq_len * (width + SWA_Q) * 4 + 4 * nb * W * SWA_KV * 4)
    o, k, v = pl.pallas_call(
        _swa_sample_kernel,
        grid=(batch // nb,),
        in_specs=[pl.BlockSpec(memory_space=pltpu.SMEM),
                  pl.BlockSpec((nb, seq_len, width), lambda b: (b, 0, 0)),
                  pl.BlockSpec((nb, W, SWA_KV), lambda b: (b, 0, 0)),
                  pl.BlockSpec((nb, W, SWA_KV), lambda b: (b, 0, 0)),
                  pl.BlockSpec((1, SWA_KV), lambda b: (0, 0)),
                  pl.BlockSpec((1, SWA_KV), lambda b: (0, 0)),
                  pl.BlockSpec((SWA_KV, SWA_KV), lambda b: (0, 0))],
        out_specs=[pl.BlockSpec((nb, seq_len, SWA_Q), lambda b: (b, 0, 0)),
                   pl.BlockSpec((nb, W, SWA_KV), lambda b: (b, 0, 0)),
                   pl.BlockSpec((nb, W, SWA_KV), lambda b: (b, 0, 0))],
        out_shape=[jax.ShapeDtypeStruct((batch, seq_len, SWA_Q), F32),
                   jax.ShapeDtypeStruct((batch, W, SWA_KV), F32),
                   jax.ShapeDtypeStruct((batch, W, SWA_KV), F32)],
        compiler_params=_params(("parallel",), nbytes),
        name="swa_sample",
    )(sinks, p3, k_cache, v_cache, jnp.tile(gq, SWA_KV_HEADS).reshape(1, SWA_KV),
      jnp.tile(gk, SWA_KV_HEADS).reshape(1, SWA_KV), _swa_gsum())
    return o.reshape(batch * seq_len, SWA_Q), k, v


def _lru_conv(x, shifted, cw_ref, cb_ref):
    y = cb_ref[...] + cw_ref[CONV_WIDTH - 1:CONV_WIDTH] * x
    for s in range(1, CONV_WIDTH):
        y = y + cw_ref[CONV_WIDTH - 1 - s:CONV_WIDTH - s] * shifted[s - 1]
    return y


def _block_diag_dot(x, w_ref):
    xb = x.astype(BF16)
    return jnp.concatenate(
        [jnp.dot(xb[:, n * LRU_BLOCK:(n + 1) * LRU_BLOCK], w_ref[n], preferred_element_type=F32)
         for n in range(LRU_BLOCKS)], axis=1)


def _lru_terms(xc, wga_ref, bga_ref, wgx_ref, bgx_ref, lam_ref):
    r = _sigmoid(_block_diag_dot(xc, wga_ref) + bga_ref[...])
    i = _sigmoid(_block_diag_dot(xc, wgx_ref) + bgx_ref[...])
    log_a = (-LRU_C) * r * _softplus(-lam_ref[...])
    a = jnp.exp(log_a)
    y2 = 2.0 * log_a
    u = a * a
    em1 = jnp.where(u == 1.0, y2, jnp.where(u == 0.0, -1.0, (u - 1.0) * y2 / jnp.log(u)))
    mult = jnp.sqrt(-em1)
    return a, mult * i * xc


def _scan_rows(a, b, group):
    rows = a.shape[0]
    pos = lax.broadcasted_iota(jnp.int32, a.shape, 0) % group
    d = 1
    while d < group:
        keep = pos >= d
        b = jnp.where(keep, a * pltpu.roll(b, d, 0) + b, b)
        a = jnp.where(keep, a * pltpu.roll(a, d, 0), a)
        d *= 2
    return a, b


def _lru_prompt_kernel(p_ref, cw_ref, cb_ref, wga_ref, bga_ref, wgx_ref, bgx_ref, lam_ref,
                       o_ref, h_ref, tail_ref, hc_ref):
    n = pl.program_id(1)
    rows = p_ref.shape[1]

    @pl.when(n == 0)
    def _():
        tail_ref[...] = jnp.zeros_like(tail_ref)
        hc_ref[...] = jnp.zeros_like(hc_ref)

    y = p_ref[0, :, :D_RNN]
    x = p_ref[0, :, D_RNN:]
    tail = tail_ref[...]
    r8 = lax.broadcasted_iota(jnp.int32, (V7X_SUBLANES, D_RNN), 0)
    shifted = []
    for s in range(1, CONV_WIDTH):
        xs = pltpu.roll(x, s, 0)
        head = jnp.where(r8 < s, pltpu.roll(tail, s, 0), xs[:V7X_SUBLANES])
        shifted.append(jnp.concatenate([head, xs[V7X_SUBLANES:]], axis=0))
    xc = _lru_conv(x, shifted, cw_ref, cb_ref)
    a, bterm = _lru_terms(xc, wga_ref, bga_ref, wgx_ref, bgx_ref, lam_ref)
    acum, hzero = _scan_rows(a, bterm, rows)
    hs = acum * hc_ref[0:1] + hzero
    o_ref[0] = _gelu_tanh(y) * hs
    last = hs[rows - 1:rows]
    h_ref[0] = last
    hc_ref[...] = jnp.broadcast_to(last, hc_ref.shape)
    tail_ref[...] = x[rows - V7X_SUBLANES:]


LRU_TILE = 256


def _lru_weight_specs(imap):
    return [pl.BlockSpec((CONV_WIDTH, D_RNN), imap(2)),
            pl.BlockSpec((1, D_RNN), imap(2)),
            pl.BlockSpec((LRU_BLOCKS, LRU_BLOCK, LRU_BLOCK), imap(3)),
            pl.BlockSpec((1, D_RNN), imap(2)),
            pl.BlockSpec((LRU_BLOCKS, LRU_BLOCK, LRU_BLOCK), imap(3)),
            pl.BlockSpec((1, D_RNN), imap(2)),
            pl.BlockSpec((1, D_RNN), imap(2))]


def _lru_prompt(proj, cw, cb, wga, bga, wgx, bgx, lam, batch, seq_len):
    R = LRU_TILE
    p3 = proj.reshape(batch, seq_len, 2 * D_RNN)
    nbytes = 2 * (R * 3 * D_RNN * 4 + 2 * LRU_BLOCKS * LRU_BLOCK * LRU_BLOCK * 2) + 24 * R * D_RNN * 4
    o, h = pl.pallas_call(
        _lru_prompt_kernel,
        grid=(batch, seq_len // R),
        in_specs=[pl.BlockSpec((1, R, 2 * D_RNN), lambda b, n: (b, n, 0))]
        + _lru_weight_specs(lambda nd: (lambda b, n: (0,) * nd)),
        out_specs=[pl.BlockSpec((1, R, D_RNN), lambda b, n: (b, n, 0)),
                   pl.BlockSpec((1, 1, D_RNN), lambda b, n: (b, 0, 0))],
        out_shape=[jax.ShapeDtypeStruct((batch, seq_len, D_RNN), F32),
                   jax.ShapeDtypeStruct((batch, 1, D_RNN), F32)],
        scratch_shapes=[pltpu.VMEM((V7X_SUBLANES, D_RNN), F32), pltpu.VMEM((V7X_SUBLANES, D_RNN), F32)],
        compiler_params=_params(("parallel", "arbitrary"), nbytes),
        name="lru_prompt",
    )(p3, cw, cb.reshape(1, D_RNN), wga.astype(BF16), bga.reshape(1, D_RNN), wgx.astype(BF16),
      bgx.reshape(1, D_RNN), lam.reshape(1, D_RNN))
    return o.reshape(batch * seq_len, D_RNN), h.reshape(batch, D_RNN)


def _lru_sample_kernel(p_ref, prev_ref, h0_ref, cw_ref, cb_ref, wga_ref, bga_ref, wgx_ref, bgx_ref, lam_ref,
                       o_ref, hs_ref, *, seq_len):
    rows = p_ref.shape[0]
    y = p_ref[:, :D_RNN]
    x = p_ref[:, D_RNN:]
    prev = prev_ref[...]
    pos = lax.broadcasted_iota(jnp.int32, (rows, D_RNN), 0) % seq_len
    shifted = [jnp.where(pos < s, pltpu.roll(prev, rows - seq_len + s, 0), pltpu.roll(x, s, 0))
               for s in range(1, CONV_WIDTH)]
    xc = _lru_conv(x, shifted, cw_ref, cb_ref)
    a, bterm = _lru_terms(xc, wga_ref, bga_ref, wgx_ref, bgx_ref, lam_ref)
    acum, hzero = _scan_rows(a, bterm, seq_len)
    hs = acum * h0_ref[...] + hzero
    o_ref[...] = _gelu_tanh(y) * hs
    hs_ref[...] = hs


def _lru_sample(proj, conv_state, h0, cw, cb, wga, bga, wgx, bgx, lam, batch, seq_len):
    assert seq_len == V7X_SUBLANES
    n = batch * seq_len
    R = LRU_TILE
    prev = jnp.pad(conv_state, ((0, 0), (seq_len - (CONV_WIDTH - 1), 0), (0, 0))).reshape(n, D_RNN)
    h0_rows = jnp.repeat(h0, seq_len, axis=0)
    nbytes = 2 * (R * 6 * D_RNN * 4 + 2 * LRU_BLOCKS * LRU_BLOCK * LRU_BLOCK * 2) + 24 * R * D_RNN * 4
    return pl.pallas_call(
        functools.partial(_lru_sample_kernel, seq_len=seq_len),
        grid=(n // R,),
        in_specs=[pl.BlockSpec((R, 2 * D_RNN), lambda i: (i, 0)),
                  pl.BlockSpec((R, D_RNN), lambda i: (i, 0)),
                  pl.BlockSpec((R, D_RNN), lambda i: (i, 0))]
        + _lru_weight_specs(lambda nd: (lambda i: (0,) * nd)),
        out_specs=[pl.BlockSpec((R, D_RNN), lambda i: (i, 0)),
                   pl.BlockSpec((R, D_RNN), lambda i: (i, 0))],
        out_shape=[jax.ShapeDtypeStruct((n, D_RNN), F32), jax.ShapeDtypeStruct((n, D_RNN), F32)],
        compiler_params=_params(("parallel",), nbytes),
        name="lru_sample",
    )(proj, prev, h0_rows, cw, cb.reshape(1, D_RNN), wga.astype(BF16), bga.reshape(1, D_RNN), wgx.astype(BF16),
      bgx.reshape(1, D_RNN), lam.reshape(1, D_RNN))


def _run(gen):
    while True:
        try:
            next(gen)
        except StopIteration as stop:
            return stop.value


def _lockstep(gens):
    results = [None] * len(gens)
    live = list(range(len(gens)))
    anchor = None
    while live:
        for idx in list(live):
            try:
                gens[idx].send(anchor)
            except StopIteration as stop:
                results[idx] = stop.value
                live.remove(idx)
        if live:
            anchor = yield
    return results


def _topk_rows_steps(s, k):
    n = s.shape[0]
    rid = lax.broadcasted_iota(jnp.int32, s.shape, 0).astype(F32)
    vals, ids = [], []
    for _ in range(k):
        m = jnp.max(s, axis=0, keepdims=True)
        ix = jnp.min(jnp.where(s == m, rid, float(n)), axis=0, keepdims=True)
        vals.append(m)
        ids.append(ix)
        s = jnp.where(rid == ix, -jnp.inf, s)
        anchor = yield
        if anchor is not None:
            s = s + anchor
    return jnp.concatenate(vals, axis=0), jnp.concatenate(ids, axis=0).astype(jnp.int32)


def _zero_from(parts):
    acc = None
    for x in parts:
        bits = pltpu.bitcast(x, jnp.uint32)
        bits = bits.reshape(bits.shape[0] // V7X_SUBLANES, V7X_SUBLANES, bits.shape[1])
        folded = bits[0]
        for r in range(1, bits.shape[0]):
            folded = folded | bits[r]
        acc = folded if acc is None else acc | folded
    cols = [acc[:, t * V7X_LANES:(t + 1) * V7X_LANES] for t in range(acc.shape[1] // V7X_LANES)]
    one = cols[0]
    for t in cols[1:]:
        one = one | t
    zero = lax.shift_right_logical(lax.shift_right_logical(one, jnp.uint32(16)), jnp.uint32(16))
    return pltpu.bitcast(zero, F32)[0:1, 0:1]


def _staircase_candidates(s1, s2):
    K = s1.shape[0]
    sub = V7X_SUBLANES
    first_single = next(a for a in range(K) if K // (a + 1) == 1)
    assert (K - first_single) % sub == 0
    pieces, starts, at = [], [], 0
    for a in range(first_single):
        nb = K // (a + 1)
        rows = -(-nb // sub) * sub
        piece = s1[a:a + 1] + s2[:rows]
        if rows != nb:
            piece = jnp.where(lax.broadcasted_iota(jnp.int32, piece.shape, 0) < nb, piece, -jnp.inf)
        pieces.append(piece)
        starts.append(at)
        at += rows
    pieces.append(s1[first_single:] + s2[0:1])
    return jnp.concatenate(pieces, axis=0), starts, at


ROUTE_STEPS = 2 + 2 * PEER_TOPK


def _route_head_steps(q_ref, sk_ref, stage_ref):
    K = PEER_TOPK
    par = pl.program_id(1) % 2
    for p in range(2):
        qh = q_ref[:, p * PEER_HALF:(p + 1) * PEER_HALF].astype(BF16)
        stage_ref[par, p] = lax.dot_general(sk_ref[0, p].astype(BF16), qh, (((1,), (1,)), ((), ())),
                                            preferred_element_type=F32)
    anchor = yield
    scores = [stage_ref[par, 0], stage_ref[par, 1]]
    if anchor is not None:
        scores = [st + anchor for st in scores]
    (s1, i1), (s2, i2) = yield from _lockstep([_topk_rows_steps(st, K) for st in scores])
    anchor = yield
    cand, starts, single_start = _staircase_candidates(s1, s2)
    if anchor is not None:
        cand = cand + anchor
    top, ci = yield from _topk_rows_steps(cand, K)
    a_id = jnp.zeros_like(ci)
    group_start = jnp.zeros_like(ci)
    for a in range(1, len(starts)):
        a_id = jnp.where(ci >= starts[a], a, a_id)
        group_start = jnp.where(ci >= starts[a], starts[a], group_start)
    single = ci >= single_start
    a_id = jnp.where(single, len(starts) + ci - single_start, a_id)
    b_id = jnp.where(single, 0, ci - group_start)
    e1 = jnp.zeros_like(ci)
    e2 = jnp.zeros_like(ci)
    for a in range(K):
        e1 = jnp.where(a_id == a, i1[a:a + 1], e1)
        e2 = jnp.where(b_id == a, i2[a:a + 1], e2)
    e = jnp.exp(top - top[0:1])
    return e1, e2, e / jnp.sum(e, axis=0, keepdims=True)


def _route_kernel(q_ref, sk_ref, e1_ref, e2_ref, g_ref, stage_ref):
    e1_ref[0], e2_ref[0], g_ref[0] = _run(_route_head_steps(q_ref, sk_ref, stage_ref))


EXPERT_TILE = 256


def _peer_route(q, sub_keys):
    n = q.shape[0]
    tb = EXPERT_TILE
    spec = pl.BlockSpec((1, PEER_TOPK, tb), lambda i, h: (h, 0, i))
    nbytes = 2 * (tb * PEER_KEY_DIM * 4 + 2 * PEER_NKEYS * PEER_HALF * 4) + 16 * 2 * PEER_NKEYS * tb * 4
    return pl.pallas_call(
        _route_kernel,
        grid=(n // tb, PEER_HEADS),
        in_specs=[pl.BlockSpec((tb, PEER_KEY_DIM), lambda i, h: (i, h)),
                  pl.BlockSpec((1, 2, PEER_NKEYS, PEER_HALF), lambda i, h: (h, 0, 0, 0))],
        out_specs=[spec, spec, spec],
        out_shape=[jax.ShapeDtypeStruct((PEER_HEADS, PEER_TOPK, n), jnp.int32),
                   jax.ShapeDtypeStruct((PEER_HEADS, PEER_TOPK, n), jnp.int32),
                   jax.ShapeDtypeStruct((PEER_HEADS, PEER_TOPK, n), F32)],
        scratch_shapes=[pltpu.VMEM((2, 2, PEER_NKEYS, tb), F32)],
        compiler_params=_params(("parallel", "parallel"), nbytes),
        name="peer_route",
    )(q, sub_keys)


EXPERT_CHUNK = PEER_EXPERTS // PEER_HEADS
EXPERT_SUB = 256
WEIGHT_BLOCK = 512
GATE_ROW_PAD = V7X_SUBLANES
GATE_UNROLL = 64


def _expert_kernel(h_ref, q_ref, sk_ref, e1_ref, e2_ref, g_ref, ut_ref, v0_ref, v1_ref, x_ref, gt_ref, o_ref,
                   gate_ref, acc_ref, e1t_ref, e2t_ref, gtt_ref, re1_ref, re2_ref, rg_ref, stage_ref, w_ref):
    i = pl.program_id(0)
    c = pl.program_id(1)
    tb = h_ref.shape[0]
    NK = PEER_NKEYS
    stride = tb + GATE_ROW_PAD
    slot = i % 2

    @pl.when((i == 0) & (c == 0))
    def _():
        re1_ref[0] = e1_ref[...].reshape(PEER_PAIRS, tb)
        re2_ref[0] = e2_ref[...].reshape(PEER_PAIRS, tb)
        rg_ref[0] = g_ref[...].reshape(PEER_PAIRS, tb)

    @pl.when(c == 0)
    def _():
        acc_ref[...] = jnp.zeros_like(acc_ref)
        e1t_ref[...] = jnp.transpose(re1_ref[slot])
        e2t_ref[...] = jnp.transpose(re2_ref[slot])
        gtt_ref[...] = jnp.transpose(rg_ref[slot])
        kid = lax.broadcasted_iota(jnp.int32, (NK, PEER_PAIRS), 0)

        def per_token(n, carry):
            i1 = e1t_ref[pl.ds(n, 1), :]
            i2 = e2t_ref[pl.ds(n, 1), :]
            gg = gtt_ref[pl.ds(n, 1), :]
            a_t = jnp.where(kid == i1, 1.0, 0.0).astype(BF16)
            b_t = jnp.where(kid == i2, 0.5 * gg, 0.0).astype(BF16)
            gn = lax.dot_general(a_t, b_t, (((1,), (1,)), ((), ())), preferred_element_type=F32)
            gate_ref[pl.ds(n, NK, stride=stride), :] = gn
            return carry

        lax.fori_loop(0, tb, per_token, 0, unroll=GATE_UNROLL)

    route = _route_head_steps(q_ref, sk_ref, stage_ref)
    routed = []
    n_sub = EXPERT_CHUNK // EXPERT_SUB

    def advance_route(rounds, anchor=None):
        for _ in range(rounds):
            if routed:
                return
            try:
                route.send(anchor)
            except StopIteration as stop:
                routed.append(stop.value)
            anchor = None

    h = h_ref[...]
    keys_per_chunk = EXPERT_CHUNK // NK
    advance_route(1)
    WB = WEIGHT_BLOCK
    w_parts = []
    for j in range(n_sub):
        lo = j * EXPERT_SUB
        blk, off = divmod(lo, WB)
        s = jnp.dot(h, ut_ref[0, blk, :, off:off + EXPERT_SUB], preferred_element_type=F32)
        gsel = jnp.concatenate(
            [gate_ref[pl.ds(pl.multiple_of((c * keys_per_chunk + lo // NK + t) * stride, V7X_SUBLANES), tb), :]
             for t in range(EXPERT_SUB // NK)], axis=1)
        wb = _gelu_times_half_gate(s, gsel).astype(BF16)
        w_parts.append(wb)
        w_ref[blk, :, off:off + EXPERT_SUB] = wb
    anchor = _zero_from(w_parts)
    for n, vn_ref in enumerate((v0_ref, v1_ref)):
        part = None
        for k in range(EXPERT_CHUNK // WB):
            d = jnp.dot(w_ref[k], vn_ref[0, k * WB:(k + 1) * WB, :], preferred_element_type=F32)
            part = d if part is None else part + d
        acc_ref[:, n * WB:(n + 1) * WB] += part
    advance_route(ROUTE_STEPS + 1, anchor)
    n1, n2, ng = routed[0]
    rows = pl.ds(pl.multiple_of(c * PEER_TOPK, PEER_TOPK), PEER_TOPK)
    re1_ref[1 - slot, rows, :] = n1
    re2_ref[1 - slot, rows, :] = n2
    rg_ref[1 - slot, rows, :] = ng

    @pl.when(c == pl.num_programs(1) - 1)
    def _():
        o_ref[...] = x_ref[...] + gt_ref[0] * acc_ref[...]


def _peer_u_blocks(u):
    nl, e, d = u.shape
    return u.reshape(nl, e // WEIGHT_BLOCK, WEIGHT_BLOCK, d).transpose(0, 1, 3, 2).astype(BF16)


def _peer_experts(h_bf16, q, sub_keys, u_blocks, v_bf16, layer, x, gate, seq_len):
    n, d = x.shape
    tb = EXPERT_TILE
    ec = EXPERT_CHUNK
    wb = WEIGHT_BLOCK
    assert d == 2 * wb
    nt = n // tb
    e1, e2, g = _peer_route(q[:tb], sub_keys)
    gt_arr, gt_spec0 = _row_operand(gate, seq_len, tb)
    gt_spec = pl.BlockSpec(gt_spec0.block_shape, lambda i, c: gt_spec0.index_map(i))
    rspec = pl.BlockSpec((PEER_HEADS, PEER_TOPK, tb), lambda i, c: (0, 0, 0))
    nbytes = (2 * (tb * d * 2 + 3 * PEER_PAIRS * tb * 4 + 2 * d * ec * 2 + 3 * tb * d * 4 + tb * PEER_KEY_DIM * 4)
              + (tb + GATE_ROW_PAD) * PEER_NKEYS * PEER_NKEYS * 4 + tb * d * 4 + 9 * tb * PEER_PAIRS * 4
              + 8 * tb * EXPERT_SUB * 4 + 8 * PEER_NKEYS * tb * 4)
    return pl.pallas_call(
        _expert_kernel,
        grid=(nt, PEER_HEADS),
        in_specs=[pl.BlockSpec((tb, d), lambda i, c: (i, 0)),
                  pl.BlockSpec((tb, PEER_KEY_DIM), lambda i, c: (jnp.minimum(i + 1, nt - 1), c)),
                  pl.BlockSpec((1, 2, PEER_NKEYS, PEER_HALF), lambda i, c: (c, 0, 0, 0)),
                  rspec, rspec, rspec,
                  pl.BlockSpec((1, ec // wb, d, wb), lambda i, c: (layer, c, 0, 0)),
                  pl.BlockSpec((1, ec, wb), lambda i, c: (layer, c, 0)),
                  pl.BlockSpec((1, ec, wb), lambda i, c: (layer, c, 1)),
                  pl.BlockSpec((tb, d), lambda i, c: (i, 0)),
                  gt_spec],
        out_specs=pl.BlockSpec((tb, d), lambda i, c: (i, 0)),
        out_shape=jax.ShapeDtypeStruct((n, d), F32),
        scratch_shapes=[pltpu.VMEM(((tb + GATE_ROW_PAD) * PEER_NKEYS, PEER_NKEYS), F32),
                        pltpu.VMEM((tb, d), F32),
                        pltpu.VMEM((tb, PEER_PAIRS), jnp.int32),
                        pltpu.VMEM((tb, PEER_PAIRS), jnp.int32),
                        pltpu.VMEM((tb, PEER_PAIRS), F32),
                        pltpu.VMEM((2, PEER_PAIRS, tb), jnp.int32),
                        pltpu.VMEM((2, PEER_PAIRS, tb), jnp.int32),
                        pltpu.VMEM((2, PEER_PAIRS, tb), F32),
                        pltpu.VMEM((2, 2, PEER_NKEYS, tb), F32),
                        pltpu.VMEM((ec // wb, tb, wb), BF16)],
        compiler_params=_params(("arbitrary", "arbitrary"), nbytes),
        name="peer_experts",
    )(h_bf16, q, sub_keys, e1, e2, g, u_blocks, v_bf16, v_bf16, x, gt_arr)


def _swa_permute_in(w_in):
    d = w_in.shape[0]
    wq = w_in[:, :SWA_Q].reshape(d, SWA_KV_HEADS, SWA_GROUP, SWA_HEAD_DIM).transpose(0, 2, 1, 3).reshape(d, SWA_Q)
    return jnp.concatenate([wq, w_in[:, SWA_Q:]], axis=1)


def _swa_permute_out(w_out):
    d = w_out.shape[1]
    return w_out.reshape(SWA_KV_HEADS, SWA_GROUP, SWA_HEAD_DIM, d).transpose(1, 0, 2, 3).reshape(SWA_Q, d)


def _prepare_weights(p):
    w = {}
    w['gla_in'] = [jnp.pad(p['w_gla_in'][j], ((0, 0), (0, GLA_IN_PAD - p['w_gla_in'].shape[2]))).astype(BF16)
                   for j in range(p['w_gla_in'].shape[0])]
    w['gla_a2'] = [jnp.pad(p['w_gla_a2'][j], ((0, V7X_LANES - GLA_GATE_RANK), (0, 0)))
                   for j in range(p['w_gla_a2'].shape[0])]
    w['gla_out'] = [m.astype(BF16) for m in p['w_gla_out']]
    w['swa_in'] = [_swa_permute_in(m).astype(BF16) for m in p['w_swa_in']]
    w['swa_out'] = [_swa_permute_out(m).astype(BF16) for m in p['w_swa_out']]
    w['lru_in'] = [m.astype(BF16) for m in p['w_lru_in']]
    w['lru_out'] = [m.astype(BF16) for m in p['w_lru_out']]
    w['peer_q'] = [m.astype(BF16) for m in p['w_peer_q']]
    w['peer_ut'] = _peer_u_blocks(p['peer_u'])
    w['peer_v'] = p['peer_v'].astype(BF16)
    return w


def _trunk(x3, mod, states, p, w):
    batch, seq_len, d = x3.shape
    x = x3.reshape(batch * seq_len, d)
    new_gla, new_k, new_v, new_conv, new_h = [], [], [], [], []
    gla_stack = None
    for i in range(DEPTH):
        kind, j = i % N_MIXERS, i // N_MIXERS
        sh_m, sc_m, gt_m, sh_f, sc_f, gt_f = [mod[i][:, k * d:(k + 1) * d] for k in range(6)]
        if kind == 0:
            proj = _norm_proj(x, p['g_ln_mix'][i], sc_m, sh_m, w['gla_in'][j], seq_len)
            if states is None:
                mix, s_new = _gla_prompt(proj, w['gla_a2'][j], p['b_gla_a'][j], p['g_gla_norm'][j], batch, seq_len)
                new_gla.append(s_new)
            else:
                mix, gla_stack = _gla_sample(proj, states[0], j, gla_stack, w['gla_a2'][j], p['b_gla_a'][j],
                                             p['g_gla_norm'][j], batch, seq_len)
            w_out = w['gla_out'][j]
        elif kind == 1:
            proj = _norm_proj(x, p['g_ln_mix'][i], sc_m, sh_m, w['swa_in'][j], seq_len)
            if states is None:
                mix, k_n, v_n = _swa_prompt(proj, p['g_swa_q'][j], p['g_swa_k'][j], p['swa_sinks'][j], batch, seq_len)
            else:
                kc = states[1][j].reshape(batch, WINDOW, SWA_KV)
                vc = states[2][j].reshape(batch, WINDOW, SWA_KV)
                mix, k_n, v_n = _swa_sample(proj, kc, vc, p['g_swa_q'][j], p['g_swa_k'][j], p['swa_sinks'][j],
                                            batch, seq_len)
            new_k.append(k_n.reshape(batch, WINDOW, SWA_KV_HEADS, SWA_HEAD_DIM))
            new_v.append(v_n.reshape(batch, WINDOW, SWA_KV_HEADS, SWA_HEAD_DIM))
            w_out = w['swa_out'][j]
        else:
            proj = _norm_proj(x, p['g_ln_mix'][i], sc_m, sh_m, w['lru_in'][j], seq_len)
            lru_args = (p['lru_conv_w'][j], p['lru_conv_b'][j], p['w_lru_ga'][j], p['b_lru_ga'][j],
                        p['w_lru_gx'][j], p['b_lru_gx'][j], p['lru_lam'][j], batch, seq_len)
            assert seq_len >= CONV_WIDTH - 1
            if states is None:
                mix, h_n = _lru_prompt(proj, *lru_args)
            else:
                mix, hs = _lru_sample(proj, states[3][j], states[4][j], *lru_args)
                h_n = hs.reshape(batch, seq_len, D_RNN)[:, -1]
            new_conv.append(proj[:, D_RNN:].reshape(batch, seq_len, D_RNN)[:, seq_len - (CONV_WIDTH - 1):])
            new_h.append(h_n)
            w_out = w['lru_out'][j]
        x = _proj_residual(mix, w_out, x, gt_m, seq_len)
        q, hb = _norm_proj(x, p['g_ln_ffn'][i], sc_f, sh_f, w['peer_q'][i], seq_len, with_h=True)
        x = _peer_experts(hb, q, p['peer_sub_keys'][i], w['peer_ut'], w['peer_v'], i, x, gt_f, seq_len)
    y = x.reshape(batch, seq_len, d)
    gla_out = jnp.stack(new_gla) if states is None else gla_stack
    return y, (gla_out, jnp.stack(new_k), jnp.stack(new_v), jnp.stack(new_conv), jnp.stack(new_h))


def kernel(x_prompt, x_sample, state_gla, cache_swa_k, cache_swa_v, state_lru_conv, state_lru_h,
           c_prompt, c_sample, g_ln_mix, g_ln_ffn, w_mod, b_mod,
           w_gla_in, w_gla_a2, b_gla_a, g_gla_norm, w_gla_out,
           w_swa_in, g_swa_q, g_swa_k, swa_sinks, w_swa_out,
           w_lru_in, lru_conv_w, lru_conv_b, w_lru_ga, b_lru_ga, w_lru_gx, b_lru_gx, lru_lam, w_lru_out,
           w_peer_q, peer_sub_keys, peer_u, peer_v):
    p = {'g_ln_mix': g_ln_mix, 'g_ln_ffn': g_ln_ffn,
         'w_gla_in': w_gla_in, 'w_gla_a2': w_gla_a2, 'b_gla_a': b_gla_a, 'g_gla_norm': g_gla_norm,
         'w_gla_out': w_gla_out,
         'w_swa_in': w_swa_in, 'g_swa_q': g_swa_q, 'g_swa_k': g_swa_k, 'swa_sinks': swa_sinks,
         'w_swa_out': w_swa_out,
         'w_lru_in': w_lru_in, 'lru_conv_w': lru_conv_w, 'lru_conv_b': lru_conv_b,
         'w_lru_ga': w_lru_ga, 'b_lru_ga': b_lru_ga, 'w_lru_gx': w_lru_gx, 'b_lru_gx': b_lru_gx,
         'lru_lam': lru_lam, 'w_lru_out': w_lru_out,
         'w_peer_q': w_peer_q, 'peer_sub_keys': peer_sub_keys, 'peer_u': peer_u, 'peer_v': peer_v}
    w = _prepare_weights(p)
    nb_p, nb_s = c_prompt.shape[0], c_sample.shape[0]
    rows = -(-(nb_p + nb_s) // V7X_SUBLANES) * V7X_SUBLANES
    c_all = jnp.pad(jnp.concatenate([c_prompt, c_sample], axis=0), ((0, rows - nb_p - nb_s), (0, 0)))
    mod = _modulation(c_all, w_mod, b_mod)
    y_p, (gla_p, k_p, v_p, conv_p, h_p) = _trunk(x_prompt, mod[:, :nb_p], None, p, w)
    y_s, (gla_s, k_s, v_s, conv_s, h_s) = _trunk(
        x_sample, mod[:, nb_p:nb_p + nb_s],
        (state_gla, cache_swa_k, cache_swa_v, state_lru_conv, state_lru_h), p, w)
    return (y_p, y_s, gla_p, gla_s, k_p, k_s, v_p, v_s, conv_p, conv_s, h_p, h_s)
```

```python
import functools
import math

import jax
import jax.numpy as jnp
from jax import lax
from jax.experimental import pallas as pl
from jax.experimental.pallas import tpu as pltpu

F32 = jnp.float32
BF16 = jnp.bfloat16

D_MODEL = 1024
DEPTH = 4
N_MIXERS = 3
RMS_EPS = 1e-6

GLA_HEADS = 4
GLA_QK = D_MODEL // 2
GLA_V = D_MODEL
GLA_DK = GLA_QK // GLA_HEADS
GLA_DV = GLA_V // GLA_HEADS
GLA_GATE_RANK = 16
GLA_TAU = 16.0
GLA_SUB = 16
GLA_CHUNK = 128
GLA_IN_PAD = 2 * GLA_QK + 2 * GLA_V + 128

SWA_HEAD_DIM = 64
SWA_Q_HEADS = D_MODEL // SWA_HEAD_DIM
SWA_KV_HEADS = 4
SWA_GROUP = SWA_Q_HEADS // SWA_KV_HEADS
SWA_Q = SWA_Q_HEADS * SWA_HEAD_DIM
SWA_KV = SWA_KV_HEADS * SWA_HEAD_DIM
WINDOW = 128

D_RNN = D_MODEL
LRU_BLOCKS = 4
LRU_BLOCK = D_RNN // LRU_BLOCKS
CONV_WIDTH = 4
LRU_C = 8.0

PEER_HEADS = 8
PEER_NKEYS = 128
PEER_EXPERTS = PEER_NKEYS * PEER_NKEYS
PEER_KEY_DIM = 256
PEER_HALF = PEER_KEY_DIM // 2
PEER_TOPK = 16
PEER_PAIRS = PEER_HEADS * PEER_TOPK

V7X_LANES = 128
V7X_SUBLANES = 8
V7X_VMEM_BYTES = 64 * 1024 * 1024

TOKEN_TILE = 256
NEG_BIG = -1e30


def _vmem_limit(nbytes):
    return int(min(max(nbytes * 3 // 2, 16 * 1024 * 1024), V7X_VMEM_BYTES - 8 * 1024 * 1024))


def _params(semantics, nbytes):
    return pltpu.CompilerParams(dimension_semantics=semantics, vmem_limit_bytes=_vmem_limit(nbytes))


def _rms(x, g):
    return x * lax.rsqrt(jnp.mean(x * x, axis=-1, keepdims=True) + RMS_EPS) * g


def _gelu_tanh(x):
    return 0.5 * x * (1.0 + jnp.tanh(math.sqrt(2.0 / math.pi) * (x + 0.044715 * (x * x * x))))


def _gelu_times_half_gate(x, half_gate):
    c1 = math.sqrt(2.0 / math.pi)
    inner = x * (c1 + (c1 * 0.044715) * (x * x))
    return (x * half_gate) * (1.0 + jnp.tanh(inner))


def _sigmoid(x):
    return 1.0 / (1.0 + jnp.exp(-x))


def _softplus(x):
    return jnp.maximum(x, 0.0) + jnp.log1p(jnp.exp(-jnp.abs(x)))


def _row_operand(vec, seq_len, tile):
    b, d = vec.shape
    if seq_len % tile == 0:
        per_seq = seq_len // tile
        return vec.reshape(b, 1, d), pl.BlockSpec((1, 1, d), lambda i: (i // per_seq, 0, 0))
    assert tile % seq_len == 0
    rep = jnp.repeat(vec, seq_len, axis=0).reshape(b * seq_len // tile, tile, d)
    return rep, pl.BlockSpec((1, tile, d), lambda i: (i, 0, 0))


def _mod_kernel(c_ref, w_ref, b_ref, o_ref):
    c = c_ref[...]
    sc = c * _sigmoid(c)
    o_ref[0] = jnp.dot(sc, w_ref[0], preferred_element_type=F32) + b_ref[0]


def _modulation(c, w_mod, b_mod):
    bp, d = c.shape
    tn = 1024
    nt = 6 * d // tn
    return pl.pallas_call(
        _mod_kernel,
        grid=(DEPTH, nt),
        in_specs=[pl.BlockSpec((bp, d), lambda l, j: (0, 0)),
                  pl.BlockSpec((1, d, tn), lambda l, j: (l, 0, j)),
                  pl.BlockSpec((1, 1, tn), lambda l, j: (l, 0, j))],
        out_specs=pl.BlockSpec((1, bp, tn), lambda l, j: (l, 0, j)),
        out_shape=jax.ShapeDtypeStruct((DEPTH, bp, 6 * d), F32),
        compiler_params=_params(("parallel", "parallel"), 2 * (d * tn * 4 + 2 * bp * tn * 4)),
        name="modulation",
    )(c, w_mod, b_mod.reshape(DEPTH, 1, 6 * d))


def _norm_proj_kernel(x_ref, g_ref, sc_ref, sh_ref, w_ref, o_ref, *h_ref):
    h = _rms(x_ref[...], g_ref[...]) * (1.0 + sc_ref[0]) + sh_ref[0]
    hb = h.astype(BF16)
    o_ref[...] = jnp.dot(hb, w_ref[...], preferred_element_type=F32).astype(o_ref.dtype)
    if h_ref:
        h_ref[0][...] = hb


def _norm_proj(x, g, scale, shift, w_bf16, seq_len, with_h=False):
    n, d = x.shape
    nout = w_bf16.shape[1]
    tm = TOKEN_TILE
    sc_arr, sc_spec = _row_operand(scale, seq_len, tm)
    sh_arr, sh_spec = _row_operand(shift, seq_len, tm)
    out_shape = [jax.ShapeDtypeStruct((n, nout), BF16 if with_h else F32)]
    out_specs = [pl.BlockSpec((tm, nout), lambda i: (i, 0))]
    if with_h:
        out_shape.append(jax.ShapeDtypeStruct((n, d), BF16))
        out_specs.append(pl.BlockSpec((tm, d), lambda i: (i, 0)))
    nbytes = 2 * (tm * d * 4 + d * nout * 2 + tm * nout * 4 + 3 * tm * d * 4)
    res = pl.pallas_call(
        _norm_proj_kernel,
        grid=(n // tm,),
        in_specs=[pl.BlockSpec((tm, d), lambda i: (i, 0)),
                  pl.BlockSpec((1, d), lambda i: (0, 0)),
                  sc_spec, sh_spec,
                  pl.BlockSpec((d, nout), lambda i: (0, 0))],
        out_specs=out_specs,
        out_shape=out_shape,
        compiler_params=_params(("parallel",), nbytes),
        name="norm_proj",
    )(x, g.reshape(1, d), sc_arr, sh_arr, w_bf16)
    return res if with_h else res[0]


def _proj_residual_kernel(a_ref, w_ref, x_ref, gt_ref, o_ref):
    y = jnp.dot(a_ref[...].astype(BF16), w_ref[...], preferred_element_type=F32)
    o_ref[...] = x_ref[...] + gt_ref[0] * y


def _proj_residual(a, w_bf16, x, gate, seq_len):
    n, k = a.shape
    d = x.shape[1]
    tm = TOKEN_TILE
    gt_arr, gt_spec = _row_operand(gate, seq_len, tm)
    nbytes = 2 * (tm * k * 4 + k * d * 2 + 3 * tm * d * 4)
    return pl.pallas_call(
        _proj_residual_kernel,
        grid=(n // tm,),
        in_specs=[pl.BlockSpec((tm, k), lambda i: (i, 0)),
                  pl.BlockSpec((k, d), lambda i: (0, 0)),
                  pl.BlockSpec((tm, d), lambda i: (i, 0)),
                  gt_spec],
        out_specs=pl.BlockSpec((tm, d), lambda i: (i, 0)),
        out_shape=jax.ShapeDtypeStruct((n, d), F32),
        compiler_params=_params(("parallel",), nbytes),
        name="proj_residual",
    )(a, w_bf16, x, gt_arr)


def _log_decay(lr, wa2, ba):
    z = jnp.dot(lr, wa2, preferred_element_type=F32) + ba
    return (jnp.minimum(z, 0.0) - jnp.log1p(jnp.exp(-jnp.abs(z)))) * (1.0 / GLA_TAU)


def _col_bcast(row):
    return jnp.transpose(jnp.broadcast_to(row, (V7X_LANES, V7X_LANES)))


def _head_out(o, gate, gn):
    return _rms(o, gn) * (gate * _sigmoid(gate))


def _gla_prompt_kernel(p_ref, wa2_ref, ba_ref, gn_ref, o_ref, s_ref):
    c = pl.program_id(1)
    C = GLA_CHUNK
    nsub = C // GLA_SUB

    @pl.when(c == 0)
    def _():
        s_ref[...] = jnp.zeros_like(s_ref)

    row = lax.broadcasted_iota(jnp.int32, (C, C), 0)
    col = lax.broadcasted_iota(jnp.int32, (C, C), 1)
    tri = (col <= row).astype(F32)
    later = ((col > row) & (col // GLA_SUB == row // GLA_SUB)).astype(F32)
    sums = jnp.concatenate([tri, later], axis=0)
    causal = col <= row
    rsub = lax.broadcasted_iota(jnp.int32, (C, GLA_DK), 0) // GLA_SUB

    lr = p_ref[0, :, 2 * GLA_QK + 2 * GLA_V:]
    la_all = _log_decay(lr, wa2_ref[...], ba_ref[...])
    cs_all = jnp.dot(sums, la_all, preferred_element_type=F32, precision=lax.Precision.HIGHEST)
    for h in range(GLA_HEADS):
        q = p_ref[0, :, h * GLA_DK:(h + 1) * GLA_DK] * (GLA_DK ** -0.5)
        k = p_ref[0, :, GLA_QK + h * GLA_DK:GLA_QK + (h + 1) * GLA_DK]
        v = p_ref[0, :, 2 * GLA_QK + h * GLA_DV:2 * GLA_QK + (h + 1) * GLA_DV]
        gate = p_ref[0, :, 2 * GLA_QK + GLA_V + h * GLA_DV:2 * GLA_QK + GLA_V + (h + 1) * GLA_DV]
        cs = cs_all[:, h * GLA_DK:(h + 1) * GLA_DK]
        b = cs[:C]
        to_sub_end = cs[C:]
        b_last = b[C - 1:C]
        k_sub = k * jnp.exp(to_sub_end)
        q_parts, k_parts = [], []
        for m in range(nsub):
            ref_row = b[m * GLA_SUB + GLA_SUB - 1:m * GLA_SUB + GLA_SUB]
            e = jnp.where(rsub >= m, b - ref_row, NEG_BIG)
            q_parts.append((q * jnp.exp(e)).astype(BF16))
            k_parts.append(jnp.where(rsub == m, k_sub, 0.0).astype(BF16))
        qcat = jnp.concatenate(q_parts, axis=1)
        kcat = jnp.concatenate(k_parts, axis=1)
        att = lax.dot_general(qcat, kcat, (((1,), (1,)), ((), ())), preferred_element_type=F32)
        att = jnp.where(causal, att, 0.0)
        s_old = s_ref[0, h]
        o = jnp.dot(att.astype(BF16), v.astype(BF16), preferred_element_type=F32)
        o = o + jnp.dot((q * jnp.exp(b)).astype(BF16), s_old.astype(BF16), preferred_element_type=F32)
        k_end = (k * jnp.exp(b_last - b)).astype(BF16)
        upd = lax.dot_general(k_end, v.astype(BF16), (((0,), (0,)), ((), ())), preferred_element_type=F32)
        decay = _col_bcast(jnp.exp(b_last))
        s_ref[0, h] = jnp.concatenate([decay] * (GLA_DV // V7X_LANES), axis=1) * s_old + upd
        o_ref[0, :, h * GLA_DV:(h + 1) * GLA_DV] = _head_out(o, gate, gn_ref[...])


def _gla_prompt(proj, wa2_pad, ba, gn, batch, seq_len):
    C = GLA_CHUNK
    p3 = proj.reshape(batch, seq_len, GLA_IN_PAD)
    nbytes = 2 * (C * GLA_IN_PAD * 4 + C * GLA_V * 4 + GLA_HEADS * GLA_DK * GLA_DV * 4) + 64 * C * C * 4
    o, s = pl.pallas_call(
        _gla_prompt_kernel,
        grid=(batch, seq_len // C),
        in_specs=[pl.BlockSpec((1, C, GLA_IN_PAD), lambda b, c: (b, c, 0)),
                  pl.BlockSpec((V7X_LANES, GLA_QK), lambda b, c: (0, 0)),
                  pl.BlockSpec((1, GLA_QK), lambda b, c: (0, 0)),
                  pl.BlockSpec((1, GLA_DV), lambda b, c: (0, 0))],
        out_specs=[pl.BlockSpec((1, C, GLA_V), lambda b, c: (b, c, 0)),
                   pl.BlockSpec((1, GLA_HEADS, GLA_DK, GLA_DV), lambda b, c: (b, 0, 0, 0))],
        out_shape=[jax.ShapeDtypeStruct((batch, seq_len, GLA_V), F32),
                   jax.ShapeDtypeStruct((batch, GLA_HEADS, GLA_DK, GLA_DV), F32)],
        compiler_params=_params(("parallel", "arbitrary"), nbytes),
        name="gla_prompt",
    )(p3, wa2_pad, ba.reshape(1, GLA_QK), gn.reshape(1, GLA_DV))
    return o.reshape(batch * seq_len, GLA_V), s


GLA_SAMPLE_BATCH = 8


def _gla_sample_kernel(p_ref, s0_ref, wa2_ref, ba_ref, gn_ref, *refs):
    o_ref, s_ref = refs[-2:]
    nseq = s0_ref.shape[1]
    rows = p_ref.shape[0]
    T = rows // nseq
    row = lax.broadcasted_iota(jnp.int32, (rows, rows), 0)
    col = lax.broadcasted_iota(jnp.int32, (rows, rows), 1)
    same_seq = (row // T) == (col // T)
    causal = same_seq & (col <= row)
    sums = jnp.concatenate([causal.astype(F32), same_seq.astype(F32)], axis=0)
    lr = p_ref[:, 2 * GLA_QK + 2 * GLA_V:]
    la = _log_decay(lr, wa2_ref[...], ba_ref[...])
    cs = jnp.dot(sums, la, preferred_element_type=F32, precision=lax.Precision.HIGHEST)
    b_all, b_last_all = cs[:rows], cs[rows:]
    for h in range(GLA_HEADS):
        hk = slice(h * GLA_DK, (h + 1) * GLA_DK)
        q = p_ref[:, h * GLA_DK:(h + 1) * GLA_DK] * (GLA_DK ** -0.5)
        k = p_ref[:, GLA_QK + h * GLA_DK:GLA_QK + (h + 1) * GLA_DK]
        v = p_ref[:, 2 * GLA_QK + h * GLA_DV:2 * GLA_QK + (h + 1) * GLA_DV]
        gate = p_ref[:, 2 * GLA_QK + GLA_V + h * GLA_DV:2 * GLA_QK + GLA_V + (h + 1) * GLA_DV]
        b, b_last = b_all[:, hk], b_last_all[:, hk]
        k_end = k * jnp.exp(b_last - b)
        q_rel = (q * jnp.exp(b - b_last)).astype(BF16)
        q_dec = q * jnp.exp(b)
        att = lax.dot_general(q_rel, k_end.astype(BF16), (((1,), (1,)), ((), ())), preferred_element_type=F32)
        att = jnp.where(causal, att, 0.0)
        o_intra = jnp.dot(att.astype(BF16), v.astype(BF16), preferred_element_type=F32)
        o_inter = []
        for j in range(nseq):
            rj = slice(j * T, (j + 1) * T)
            s_old = s0_ref[0, j, h]
            o_inter.append(jnp.dot(q_dec[rj].astype(BF16), s_old.astype(BF16), preferred_element_type=F32))
            upd = lax.dot_general(k_end[rj].astype(BF16), v[rj].astype(BF16), (((0,), (0,)), ((), ())),
                                  preferred_element_type=F32)
            decay = _col_bcast(jnp.exp(b_last[j * T:j * T + 1]))
            s_ref[0, j, h] = jnp.concatenate([decay] * (GLA_DV // V7X_LANES), axis=1) * s_old + upd
        o = o_intra + jnp.concatenate(o_inter, axis=0)
        o_ref[:, h * GLA_DV:(h + 1) * GLA_DV] = _head_out(o, gate, gn_ref[...])


def _gla_sample(proj, states, layer, new_states, wa2_pad, ba, gn, batch, seq_len):
    assert seq_len <= GLA_SUB
    nb = GLA_SAMPLE_BATCH
    rows = nb * seq_len
    state_spec = pl.BlockSpec((1, nb, GLA_HEADS, GLA_DK, GLA_DV), lambda b: (layer, b, 0, 0, 0))
    state_block = nb * GLA_HEADS * GLA_DK * GLA_DV * 4
    nbytes = 2 * (rows * (GLA_IN_PAD + GLA_V) * 4 + 2 * state_block) + 16 * rows * GLA_IN_PAD * 4
    in_specs = [pl.BlockSpec((rows, GLA_IN_PAD), lambda b: (b, 0)),
                state_spec,
                pl.BlockSpec((V7X_LANES, GLA_QK), lambda b: (0, 0)),
                pl.BlockSpec((1, GLA_QK), lambda b: (0, 0)),
                pl.BlockSpec((1, GLA_DV), lambda b: (0, 0))]
    operands = [proj, states, wa2_pad, ba.reshape(1, GLA_QK), gn.reshape(1, GLA_DV)]
    aliases = {}
    if new_states is not None:
        in_specs.append(pl.BlockSpec(memory_space=pl.ANY))
        operands.append(new_states)
        aliases = {len(operands) - 1: 1}
    o, s = pl.pallas_call(
        _gla_sample_kernel,
        grid=(batch // nb,),
        in_specs=in_specs,
        out_specs=[pl.BlockSpec((rows, GLA_V), lambda b: (b, 0)), state_spec],
        out_shape=[jax.ShapeDtypeStruct((batch * seq_len, GLA_V), F32),
                   jax.ShapeDtypeStruct(states.shape, F32)],
        input_output_aliases=aliases,
        compiler_params=_params(("parallel",), nbytes),
        name="gla_sample",
    )(*operands)
    return o, s


def _head_group_norm(x, gain, gsum):
    sq = x * x
    hi = sq.astype(BF16)
    lo = (sq - hi.astype(F32)).astype(BF16)
    ms = (jnp.dot(hi, gsum, preferred_element_type=F32) + jnp.dot(lo, gsum, preferred_element_type=F32))
    return x * lax.rsqrt(ms * (1.0 / SWA_HEAD_DIM) + RMS_EPS) * gain


def _swa_attend(q_groups, k_all, v_all, mask, sink_ref):
    tq = q_groups[0].shape[0]
    lane_head = lax.broadcasted_iota(jnp.int32, (tq, SWA_KV), 1) // SWA_HEAD_DIM
    mask_rows = jnp.concatenate([mask] * SWA_GROUP, axis=0)
    out = [jnp.zeros((tq, SWA_KV), F32) for _ in range(SWA_GROUP)]
    for kv in range(SWA_KV_HEADS):
        in_head = lane_head == kv
        qs = jnp.concatenate([jnp.where(in_head, qg, 0.0) for qg in q_groups], axis=0).astype(BF16)
        s = lax.dot_general(qs, k_all, (((1,), (1,)), ((), ())), preferred_element_type=F32)
        s = jnp.where(mask_rows, s, -jnp.inf)
        sink = jnp.concatenate(
            [jnp.full((tq, 1), sink_ref[kv * SWA_GROUP + g], F32) for g in range(SWA_GROUP)], axis=0)
        m = jnp.maximum(jnp.max(s, axis=1, keepdims=True), sink)
        p = jnp.exp(s - m)
        denom = jnp.sum(p, axis=1, keepdims=True) + jnp.exp(sink - m)
        pv = jnp.dot(p.astype(BF16), v_all, preferred_element_type=F32) / denom
        for g in range(SWA_GROUP):
            out[g] = jnp.where(in_head, pv[g * tq:(g + 1) * tq], out[g])
    return out


SWA_STEP_WINDOWS = 4


def _swa_prompt_kernel(sink_ref, cur_ref, prev_ref, gq_ref, gk_ref, gsum_ref, o_ref, k_ref, v_ref):
    n = pl.program_id(1)
    W = WINDOW
    gsum = gsum_ref[...]
    k_cur = _head_group_norm(cur_ref[0, :, SWA_Q:SWA_Q + SWA_KV], gk_ref[...], gsum)
    k_prev = _head_group_norm(prev_ref[0, :, SWA_Q:SWA_Q + SWA_KV], gk_ref[...], gsum)
    k_rows = jnp.concatenate([k_prev, k_cur], axis=0).astype(BF16)
    v_rows = jnp.concatenate([prev_ref[0, :, SWA_Q + SWA_KV:], cur_ref[0, :, SWA_Q + SWA_KV:]], axis=0).astype(BF16)
    t = lax.broadcasted_iota(jnp.int32, (W, 2 * W), 0)
    s = lax.broadcasted_iota(jnp.int32, (W, 2 * W), 1)
    band = (s >= t) & (s <= t + W)
    for w in range(SWA_STEP_WINDOWS):
        rows = slice(w * W, (w + 1) * W)
        mask = band if w > 0 else band & ((s >= W) | (n > 0))
        q_groups = [_head_group_norm(cur_ref[0, rows, g * SWA_KV:(g + 1) * SWA_KV], gq_ref[...], gsum)
                    * (SWA_HEAD_DIM ** -0.5) for g in range(SWA_GROUP)]
        out = _swa_attend(q_groups, k_rows[w * W:(w + 2) * W], v_rows[w * W:(w + 2) * W], mask, sink_ref)
        o_ref[0, rows, :] = jnp.concatenate(out, axis=1)
    last = slice((SWA_STEP_WINDOWS - 1) * W, SWA_STEP_WINDOWS * W)
    k_ref[0] = k_cur[last]
    v_ref[0] = cur_ref[0, last, SWA_Q + SWA_KV:]


def _swa_gsum():
    head = jnp.arange(SWA_KV) // SWA_HEAD_DIM
    return (head[:, None] == head[None, :]).astype(BF16)


def _swa_prompt(proj, gq, gk, sinks, batch, seq_len):
    W = WINDOW
    nw = SWA_STEP_WINDOWS
    width = SWA_Q + 2 * SWA_KV
    p3 = proj.reshape(batch, seq_len, width)
    nbytes = 2 * ((nw + 1) * W * width * 4 + nw * W * SWA_Q * 4 + 2 * W * SWA_KV * 4) + nw * 48 * W * 2 * W * 4
    o, k, v = pl.pallas_call(
        _swa_prompt_kernel,
        grid=(batch, seq_len // (nw * W)),
        in_specs=[pl.BlockSpec(memory_space=pltpu.SMEM),
                  pl.BlockSpec((1, nw * W, width), lambda b, n: (b, n, 0)),
                  pl.BlockSpec((1, W, width), lambda b, n: (b, jnp.maximum(nw * n - 1, 0), 0)),
                  pl.BlockSpec((1, SWA_KV), lambda b, n: (0, 0)),
                  pl.BlockSpec((1, SWA_KV), lambda b, n: (0, 0)),
                  pl.BlockSpec((SWA_KV, SWA_KV), lambda b, n: (0, 0))],
        out_specs=[pl.BlockSpec((1, nw * W, SWA_Q), lambda b, n: (b, n, 0)),
                   pl.BlockSpec((1, W, SWA_KV), lambda b, n: (b, 0, 0)),
                   pl.BlockSpec((1, W, SWA_KV), lambda b, n: (b, 0, 0))],
        out_shape=[jax.ShapeDtypeStruct((batch, seq_len, SWA_Q), F32),
                   jax.ShapeDtypeStruct((batch, W, SWA_KV), F32),
                   jax.ShapeDtypeStruct((batch, W, SWA_KV), F32)],
        compiler_params=_params(("parallel", "arbitrary"), nbytes),
        name="swa_prompt",
    )(sinks, p3, p3, jnp.tile(gq, SWA_KV_HEADS).reshape(1, SWA_KV), jnp.tile(gk, SWA_KV_HEADS).reshape(1, SWA_KV),
      _swa_gsum())
    return o.reshape(batch * seq_len, SWA_Q), k, v


SWA_SAMPLE_BATCH = 8


def _swa_sample_kernel(sink_ref, p_ref, kc_ref, vc_ref, gq_ref, gk_ref, gsum_ref, o_ref, k_ref, v_ref):
    T = p_ref.shape[1]
    gsum = gsum_ref[...]
    t = lax.broadcasted_iota(jnp.int32, (T, WINDOW + T), 0)
    s = lax.broadcasted_iota(jnp.int32, (T, WINDOW + T), 1)
    mask = (s >= t) & (s <= t + WINDOW)
    for j in range(p_ref.shape[0]):
        k_new = _head_group_norm(p_ref[j, :, SWA_Q:SWA_Q + SWA_KV], gk_ref[...], gsum)
        v_new = p_ref[j, :, SWA_Q + SWA_KV:]
        k_all = jnp.concatenate([kc_ref[j], k_new], axis=0)
        v_all = jnp.concatenate([vc_ref[j], v_new], axis=0)
        q_groups = [_head_group_norm(p_ref[j, :, g * SWA_KV:(g + 1) * SWA_KV], gq_ref[...], gsum)
                    * (SWA_HEAD_DIM ** -0.5) for g in range(SWA_GROUP)]
        out = _swa_attend(q_groups, k_all.astype(BF16), v_all.astype(BF16), mask, sink_ref)
        o_ref[j] = jnp.concatenate(out, axis=1)
        k_ref[j] = k_all[T:]
        v_ref[j] = v_all[T:]


def _swa_sample(proj, k_cache, v_cache, gq, gk, sinks, batch, seq_len):
    nb = SWA_SAMPLE_BATCH
    W = WINDOW
    width = SWA_Q + 2 * SWA_KV
    p3 = proj.reshape(batch, seq_len, width)
    nbytes = 2 * (nb * seq_len * (width + SWA_Q) * 4 + 4 * nb * W * SWA_KV * 4)
    o, k, v = pl.pallas_call(
        _swa_sample_kernel,
        grid=(batch // nb,),
        in_specs=[pl.BlockSpec(memory_space=pltpu.SMEM),
                  pl.BlockSpec((nb, seq_len, width), lambda b: (b, 0, 0)),
                  pl.BlockSpec((nb, W, SWA_KV), lambda b: (b, 0, 0)),
                  pl.BlockSpec((nb, W, SWA_KV), lambda b: (b, 0, 0)),
                  pl.BlockSpec((1, SWA_KV), lambda b: (0, 0)),
                  pl.BlockSpec((1, SWA_KV), lambda b: (0, 0)),
                  pl.BlockSpec((SWA_KV, SWA_KV), lambda b: (0, 0))],
        out_specs=[pl.BlockSpec((nb, seq_len, SWA_Q), lambda b: (b, 0, 0)),
                   pl.BlockSpec((nb, W, SWA_KV), lambda b: (b, 0, 0)),
                   pl.BlockSpec((nb, W, SWA_KV), lambda b: (b, 0, 0))],
        out_shape=[jax.ShapeDtypeStruct((batch, seq_len, SWA_Q), F32),
                   jax.ShapeDtypeStruct((batch, W, SWA_KV), F32),
                   jax.ShapeDtypeStruct((batch, W, SWA_KV), F32)],
        compiler_params=_params(("parallel",), nbytes),
        name="swa_sample",
    )(sinks, p3, k_cache, v_cache, jnp.tile(gq, SWA_KV_HEADS).reshape(1, SWA_KV),
      jnp.tile(gk, SWA_KV_HEADS).reshape(1, SWA_KV), _swa_gsum())
    return o.reshape(batch * seq_len, SWA_Q), k, v


def _lru_conv(x, shifted, cw_ref, cb_ref):
    y = cb_ref[...] + cw_ref[CONV_WIDTH - 1:CONV_WIDTH] * x
    for s in range(1, CONV_WIDTH):
        y = y + cw_ref[CONV_WIDTH - 1 - s:CONV_WIDTH - s] * shifted[s - 1]
    return y


def _block_diag_dot(x, w_ref):
    xb = x.astype(BF16)
    return jnp.concatenate(
        [jnp.dot(xb[:, n * LRU_BLOCK:(n + 1) * LRU_BLOCK], w_ref[n], preferred_element_type=F32)
         for n in range(LRU_BLOCKS)], axis=1)


def _lru_terms(xc, wga_ref, bga_ref, wgx_ref, bgx_ref, lam_ref):
    r = _sigmoid(_block_diag_dot(xc, wga_ref) + bga_ref[...])
    i = _sigmoid(_block_diag_dot(xc, wgx_ref) + bgx_ref[...])
    log_a = (-LRU_C) * r * _softplus(-lam_ref[...])
    a = jnp.exp(log_a)
    y2 = 2.0 * log_a
    u = a * a
    em1 = jnp.where(u == 1.0, y2, jnp.where(u == 0.0, -1.0, (u - 1.0) * y2 / jnp.log(u)))
    mult = jnp.sqrt(-em1)
    return a, mult * i * xc


def _scan_rows(a, b, group):
    rows = a.shape[0]
    pos = lax.broadcasted_iota(jnp.int32, a.shape, 0) % group
    d = 1
    while d < group:
        keep = pos >= d
        b = jnp.where(keep, a * pltpu.roll(b, d, 0) + b, b)
        a = jnp.where(keep, a * pltpu.roll(a, d, 0), a)
        d *= 2
    return a, b


def _lru_prompt_kernel(p_ref, cw_ref, cb_ref, wga_ref, bga_ref, wgx_ref, bgx_ref, lam_ref,
                       o_ref, h_ref, tail_ref, hc_ref):
    n = pl.program_id(1)
    rows = p_ref.shape[1]

    @pl.when(n == 0)
    def _():
        tail_ref[...] = jnp.zeros_like(tail_ref)
        hc_ref[...] = jnp.zeros_like(hc_ref)

    y = p_ref[0, :, :D_RNN]
    x = p_ref[0, :, D_RNN:]
    tail = tail_ref[...]
    r8 = lax.broadcasted_iota(jnp.int32, (V7X_SUBLANES, D_RNN), 0)
    shifted = []
    for s in range(1, CONV_WIDTH):
        xs = pltpu.roll(x, s, 0)
        head = jnp.where(r8 < s, pltpu.roll(tail, s, 0), xs[:V7X_SUBLANES])
        shifted.append(jnp.concatenate([head, xs[V7X_SUBLANES:]], axis=0))
    xc = _lru_conv(x, shifted, cw_ref, cb_ref)
    a, bterm = _lru_terms(xc, wga_ref, bga_ref, wgx_ref, bgx_ref, lam_ref)
    acum, hzero = _scan_rows(a, bterm, rows)
    hs = acum * hc_ref[0:1] + hzero
    o_ref[0] = _gelu_tanh(y) * hs
    last = hs[rows - 1:rows]
    h_ref[0] = last
    hc_ref[...] = jnp.broadcast_to(last, hc_ref.shape)
    tail_ref[...] = x[rows - V7X_SUBLANES:]


LRU_TILE = 256


def _lru_weight_specs(imap):
    return [pl.BlockSpec((CONV_WIDTH, D_RNN), imap(2)),
            pl.BlockSpec((1, D_RNN), imap(2)),
            pl.BlockSpec((LRU_BLOCKS, LRU_BLOCK, LRU_BLOCK), imap(3)),
            pl.BlockSpec((1, D_RNN), imap(2)),
            pl.BlockSpec((LRU_BLOCKS, LRU_BLOCK, LRU_BLOCK), imap(3)),
            pl.BlockSpec((1, D_RNN), imap(2)),
            pl.BlockSpec((1, D_RNN), imap(2))]


def _lru_prompt(proj, cw, cb, wga, bga, wgx, bgx, lam, batch, seq_len):
    R = LRU_TILE
    p3 = proj.reshape(batch, seq_len, 2 * D_RNN)
    nbytes = 2 * (R * 3 * D_RNN * 4 + 2 * LRU_BLOCKS * LRU_BLOCK * LRU_BLOCK * 2) + 24 * R * D_RNN * 4
    o, h = pl.pallas_call(
        _lru_prompt_kernel,
        grid=(batch, seq_len // R),
        in_specs=[pl.BlockSpec((1, R, 2 * D_RNN), lambda b, n: (b, n, 0))]
        + _lru_weight_specs(lambda nd: (lambda b, n: (0,) * nd)),
        out_specs=[pl.BlockSpec((1, R, D_RNN), lambda b, n: (b, n, 0)),
                   pl.BlockSpec((1, 1, D_RNN), lambda b, n: (b, 0, 0))],
        out_shape=[jax.ShapeDtypeStruct((batch, seq_len, D_RNN), F32),
                   jax.ShapeDtypeStruct((batch, 1, D_RNN), F32)],
        scratch_shapes=[pltpu.VMEM((V7X_SUBLANES, D_RNN), F32), pltpu.VMEM((V7X_SUBLANES, D_RNN), F32)],
        compiler_params=_params(("parallel", "arbitrary"), nbytes),
        name="lru_prompt",
    )(p3, cw, cb.reshape(1, D_RNN), wga.astype(BF16), bga.reshape(1, D_RNN), wgx.astype(BF16),
      bgx.reshape(1, D_RNN), lam.reshape(1, D_RNN))
    return o.reshape(batch * seq_len, D_RNN), h.reshape(batch, D_RNN)


def _lru_sample_kernel(p_ref, prev_ref, h0_ref, cw_ref, cb_ref, wga_ref, bga_ref, wgx_ref, bgx_ref, lam_ref,
                       o_ref, hs_ref, *, seq_len):
    rows = p_ref.shape[0]
    y = p_ref[:, :D_RNN]
    x = p_ref[:, D_RNN:]
    prev = prev_ref[...]
    pos = lax.broadcasted_iota(jnp.int32, (rows, D_RNN), 0) % seq_len
    shifted = [jnp.where(pos < s, pltpu.roll(prev, rows - seq_len + s, 0), pltpu.roll(x, s, 0))
               for s in range(1, CONV_WIDTH)]
    xc = _lru_conv(x, shifted, cw_ref, cb_ref)
    a, bterm = _lru_terms(xc, wga_ref, bga_ref, wgx_ref, bgx_ref, lam_ref)
    acum, hzero = _scan_rows(a, bterm, seq_len)
    hs = acum * h0_ref[...] + hzero
    o_ref[...] = _gelu_tanh(y) * hs
    hs_ref[...] = hs


def _lru_sample(proj, conv_state, h0, cw, cb, wga, bga, wgx, bgx, lam, batch, seq_len):
    assert seq_len == V7X_SUBLANES
    n = batch * seq_len
    R = LRU_TILE
    prev = jnp.pad(conv_state, ((0, 0), (seq_len - (CONV_WIDTH - 1), 0), (0, 0))).reshape(n, D_RNN)
    h0_rows = jnp.repeat(h0, seq_len, axis=0)
    nbytes = 2 * (R * 6 * D_RNN * 4 + 2 * LRU_BLOCKS * LRU_BLOCK * LRU_BLOCK * 2) + 24 * R * D_RNN * 4
    return pl.pallas_call(
        functools.partial(_lru_sample_kernel, seq_len=seq_len),
        grid=(n // R,),
        in_specs=[pl.BlockSpec((R, 2 * D_RNN), lambda i: (i, 0)),
                  pl.BlockSpec((R, D_RNN), lambda i: (i, 0)),
                  pl.BlockSpec((R, D_RNN), lambda i: (i, 0))]
        + _lru_weight_specs(lambda nd: (lambda i: (0,) * nd)),
        out_specs=[pl.BlockSpec((R, D_RNN), lambda i: (i, 0)),
                   pl.BlockSpec((R, D_RNN), lambda i: (i, 0))],
        out_shape=[jax.ShapeDtypeStruct((n, D_RNN), F32), jax.ShapeDtypeStruct((n, D_RNN), F32)],
        compiler_params=_params(("parallel",), nbytes),
        name="lru_sample",
    )(proj, prev, h0_rows, cw, cb.reshape(1, D_RNN), wga.astype(BF16), bga.reshape(1, D_RNN), wgx.astype(BF16),
      bgx.reshape(1, D_RNN), lam.reshape(1, D_RNN))


def _run(gen):
    while True:
        try:
            next(gen)
        except StopIteration as stop:
            return stop.value


def _lockstep(gens):
    results = [None] * len(gens)
    live = list(range(len(gens)))
    anchor = None
    while live:
        for idx in list(live):
            try:
                gens[idx].send(anchor)
            except StopIteration as stop:
                results[idx] = stop.value
                live.remove(idx)
        if live:
            anchor = yield
    return results


def _topk_rows_steps(s, k):
    n = s.shape[0]
    rid = lax.broadcasted_iota(jnp.int32, s.shape, 0).astype(F32)
    vals, ids = [], []
    for _ in range(k):
        m = jnp.max(s, axis=0, keepdims=True)
        ix = jnp.min(jnp.where(s == m, rid, float(n)), axis=0, keepdims=True)
        vals.append(m)
        ids.append(ix)
        s = jnp.where(rid == ix, -jnp.inf, s)
        anchor = yield
        if anchor is not None:
            s = s + anchor
    return jnp.concatenate(vals, axis=0), jnp.concatenate(ids, axis=0).astype(jnp.int32)


def _zero_from(parts):
    acc = None
    for x in parts:
        bits = pltpu.bitcast(x, jnp.uint32)
        bits = bits.reshape(bits.shape[0] // V7X_SUBLANES, V7X_SUBLANES, bits.shape[1])
        folded = bits[0]
        for r in range(1, bits.shape[0]):
            folded = folded | bits[r]
        acc = folded if acc is None else acc | folded
    cols = [acc[:, t * V7X_LANES:(t + 1) * V7X_LANES] for t in range(acc.shape[1] // V7X_LANES)]
    one = cols[0]
    for t in cols[1:]:
        one = one | t
    zero = lax.shift_right_logical(lax.shift_right_logical(one, jnp.uint32(16)), jnp.uint32(16))
    return pltpu.bitcast(zero, F32)[0:1, 0:1]


def _staircase_candidates(s1, s2):
    K = s1.shape[0]
    sub = V7X_SUBLANES
    first_single = next(a for a in range(K) if K // (a + 1) == 1)
    assert (K - first_single) % sub == 0
    pieces, starts, at = [], [], 0
    for a in range(first_single):
        nb = K // (a + 1)
        rows = -(-nb // sub) * sub
        piece = s1[a:a + 1] + s2[:rows]
        if rows != nb:
            piece = jnp.where(lax.broadcasted_iota(jnp.int32, piece.shape, 0) < nb, piece, -jnp.inf)
        pieces.append(piece)
        starts.append(at)
        at += rows
    pieces.append(s1[first_single:] + s2[0:1])
    return jnp.concatenate(pieces, axis=0), starts, at


ROUTE_STEPS = 2 + 2 * PEER_TOPK


def _route_head_steps(q_ref, sk_ref, stage_ref):
    K = PEER_TOPK
    par = pl.program_id(1) % 2
    for p in range(2):
        qh = q_ref[:, p * PEER_HALF:(p + 1) * PEER_HALF].astype(BF16)
        stage_ref[par, p] = lax.dot_general(sk_ref[0, p].astype(BF16), qh, (((1,), (1,)), ((), ())),
                                            preferred_element_type=F32)
    anchor = yield
    scores = [stage_ref[par, 0], stage_ref[par, 1]]
    if anchor is not None:
        scores = [st + anchor for st in scores]
    (s1, i1), (s2, i2) = yield from _lockstep([_topk_rows_steps(st, K) for st in scores])
    anchor = yield
    cand, starts, single_start = _staircase_candidates(s1, s2)
    if anchor is not None:
        cand = cand + anchor
    top, ci = yield from _topk_rows_steps(cand, K)
    a_id = jnp.zeros_like(ci)
    group_start = jnp.zeros_like(ci)
    for a in range(1, len(starts)):
        a_id = jnp.where(ci >= starts[a], a, a_id)
        group_start = jnp.where(ci >= starts[a], starts[a], group_start)
    single = ci >= single_start
    a_id = jnp.where(single, len(starts) + ci - single_start, a_id)
    b_id = jnp.where(single, 0, ci - group_start)
    e1 = jnp.zeros_like(ci)
    e2 = jnp.zeros_like(ci)
    for a in range(K):
        e1 = jnp.where(a_id == a, i1[a:a + 1], e1)
        e2 = jnp.where(b_id == a, i2[a:a + 1], e2)
    e = jnp.exp(top - top[0:1])
    return e1, e2, e / jnp.sum(e, axis=0, keepdims=True)


def _route_kernel(q_ref, sk_ref, e1_ref, e2_ref, g_ref, stage_ref):
    e1_ref[0], e2_ref[0], g_ref[0] = _run(_route_head_steps(q_ref, sk_ref, stage_ref))


EXPERT_TILE = 256


def _peer_route(q, sub_keys):
    n = q.shape[0]
    tb = EXPERT_TILE
    spec = pl.BlockSpec((1, PEER_TOPK, tb), lambda i, h: (h, 0, i))
    nbytes = 2 * (tb * PEER_KEY_DIM * 4 + 2 * PEER_NKEYS * PEER_HALF * 4) + 16 * 2 * PEER_NKEYS * tb * 4
    return pl.pallas_call(
        _route_kernel,
        grid=(n // tb, PEER_HEADS),
        in_specs=[pl.BlockSpec((tb, PEER_KEY_DIM), lambda i, h: (i, h)),
                  pl.BlockSpec((1, 2, PEER_NKEYS, PEER_HALF), lambda i, h: (h, 0, 0, 0))],
        out_specs=[spec, spec, spec],
        out_shape=[jax.ShapeDtypeStruct((PEER_HEADS, PEER_TOPK, n), jnp.int32),
                   jax.ShapeDtypeStruct((PEER_HEADS, PEER_TOPK, n), jnp.int32),
                   jax.ShapeDtypeStruct((PEER_HEADS, PEER_TOPK, n), F32)],
        scratch_shapes=[pltpu.VMEM((2, 2, PEER_NKEYS, tb), F32)],
        compiler_params=_params(("parallel", "parallel"), nbytes),
        name="peer_route",
    )(q, sub_keys)


EXPERT_CHUNK = PEER_EXPERTS // PEER_HEADS
EXPERT_SUB = 256
WEIGHT_BLOCK = 512
GATE_ROW_PAD = V7X_SUBLANES
GATE_UNROLL = 64


def _expert_kernel(h_ref, q_ref, sk_ref, e1_ref, e2_ref, g_ref, ut_ref, v0_ref, v1_ref, x_ref, gt_ref, o_ref,
                   gate_ref, acc_ref, e1t_ref, e2t_ref, gtt_ref, re1_ref, re2_ref, rg_ref, stage_ref, w_ref):
    i = pl.program_id(0)
    c = pl.program_id(1)
    tb = h_ref.shape[0]
    NK = PEER_NKEYS
    stride = tb + GATE_ROW_PAD
    slot = i % 2

    @pl.when((i == 0) & (c == 0))
    def _():
        re1_ref[0] = e1_ref[...].reshape(PEER_PAIRS, tb)
        re2_ref[0] = e2_ref[...].reshape(PEER_PAIRS, tb)
        rg_ref[0] = g_ref[...].reshape(PEER_PAIRS, tb)

    @pl.when(c == 0)
    def _():
        acc_ref[...] = jnp.zeros_like(acc_ref)
        e1t_ref[...] = jnp.transpose(re1_ref[slot])
        e2t_ref[...] = jnp.transpose(re2_ref[slot])
        gtt_ref[...] = jnp.transpose(rg_ref[slot])
        kid = lax.broadcasted_iota(jnp.int32, (NK, PEER_PAIRS), 0)

        def per_token(n, carry):
            i1 = e1t_ref[pl.ds(n, 1), :]
            i2 = e2t_ref[pl.ds(n, 1), :]
            gg = gtt_ref[pl.ds(n, 1), :]
            a_t = jnp.where(kid == i1, 1.0, 0.0).astype(BF16)
            b_t = jnp.where(kid == i2, 0.5 * gg, 0.0).astype(BF16)
            gn = lax.dot_general(a_t, b_t, (((1,), (1,)), ((), ())), preferred_element_type=F32)
            gate_ref[pl.ds(n, NK, stride=stride), :] = gn
            return carry

        lax.fori_loop(0, tb, per_token, 0, unroll=GATE_UNROLL)

    route = _route_head_steps(q_ref, sk_ref, stage_ref)
    routed = []
    n_sub = EXPERT_CHUNK // EXPERT_SUB

    def advance_route(rounds, anchor=None):
        for _ in range(rounds):
            if routed:
                return
            try:
                route.send(anchor)
            except StopIteration as stop:
                routed.append(stop.value)
            anchor = None

    h = h_ref[...]
    keys_per_chunk = EXPERT_CHUNK // NK
    advance_route(1)
    WB = WEIGHT_BLOCK
    w_parts = []
    for j in range(n_sub):
        lo = j * EXPERT_SUB
        blk, off = divmod(lo, WB)
        s = jnp.dot(h, ut_ref[0, blk, :, off:off + EXPERT_SUB], preferred_element_type=F32)
        gsel = jnp.concatenate(
            [gate_ref[pl.ds(pl.multiple_of((c * keys_per_chunk + lo // NK + t) * stride, V7X_SUBLANES), tb), :]
             for t in range(EXPERT_SUB // NK)], axis=1)
        wb = _gelu_times_half_gate(s, gsel).astype(BF16)
        w_parts.append(wb)
        w_ref[blk, :, off:off + EXPERT_SUB] = wb
    anchor = _zero_from(w_parts)
    for n, vn_ref in enumerate((v0_ref, v1_ref)):
        part = None
        for k in range(EXPERT_CHUNK // WB):
            d = jnp.dot(w_ref[k], vn_ref[0, k * WB:(k + 1) * WB, :], preferred_element_type=F32)
            part = d if part is None else part + d
        acc_ref[:, n * WB:(n + 1) * WB] += part
    advance_route(ROUTE_STEPS + 1, anchor)
    n1, n2, ng = routed[0]
    rows = pl.ds(pl.multiple_of(c * PEER_TOPK, PEER_TOPK), PEER_TOPK)
    re1_ref[1 - slot, rows, :] = n1
    re2_ref[1 - slot, rows, :] = n2
    rg_ref[1 - slot, rows, :] = ng

    @pl.when(c == pl.num_programs(1) - 1)
    def _():
        o_ref[...] = x_ref[...] + gt_ref[0] * acc_ref[...]


def _peer_u_blocks(u):
    nl, e, d = u.shape
    return u.reshape(nl, e // WEIGHT_BLOCK, WEIGHT_BLOCK, d).transpose(0, 1, 3, 2).astype(BF16)


def _peer_experts(h_bf16, q, sub_keys, u_blocks, v_bf16, layer, x, gate, seq_len):
    n, d = x.shape
    tb = EXPERT_TILE
    ec = EXPERT_CHUNK
    wb = WEIGHT_BLOCK
    assert d == 2 * wb
    nt = n // tb
    e1, e2, g = _peer_route(q[:tb], sub_keys)
    gt_arr, gt_spec0 = _row_operand(gate, seq_len, tb)
    gt_spec = pl.BlockSpec(gt_spec0.block_shape, lambda i, c: gt_spec0.index_map(i))
    rspec = pl.BlockSpec((PEER_HEADS, PEER_TOPK, tb), lambda i, c: (0, 0, 0))
    nbytes = (2 * (tb * d * 2 + 3 * PEER_PAIRS * tb * 4 + 2 * d * ec * 2 + 3 * tb * d * 4 + tb * PEER_KEY_DIM * 4)
              + (tb + GATE_ROW_PAD) * PEER_NKEYS * PEER_NKEYS * 4 + tb * d * 4 + 9 * tb * PEER_PAIRS * 4
              + 8 * tb * EXPERT_SUB * 4 + 8 * PEER_NKEYS * tb * 4)
    return pl.pallas_call(
        _expert_kernel,
        grid=(nt, PEER_HEADS),
        in_specs=[pl.BlockSpec((tb, d), lambda i, c: (i, 0)),
                  pl.BlockSpec((tb, PEER_KEY_DIM), lambda i, c: (jnp.minimum(i + 1, nt - 1), c)),
                  pl.BlockSpec((1, 2, PEER_NKEYS, PEER_HALF), lambda i, c: (c, 0, 0, 0)),
                  rspec, rspec, rspec,
                  pl.BlockSpec((1, ec // wb, d, wb), lambda i, c: (layer, c, 0, 0)),
                  pl.BlockSpec((1, ec, wb), lambda i, c: (layer, c, 0)),
                  pl.BlockSpec((1, ec, wb), lambda i, c: (layer, c, 1)),
                  pl.BlockSpec((tb, d), lambda i, c: (i, 0)),
                  gt_spec],
        out_specs=pl.BlockSpec((tb, d), lambda i, c: (i, 0)),
        out_shape=jax.ShapeDtypeStruct((n, d), F32),
        scratch_shapes=[pltpu.VMEM(((tb + GATE_ROW_PAD) * PEER_NKEYS, PEER_NKEYS), F32),
                        pltpu.VMEM((tb, d), F32),
                        pltpu.VMEM((tb, PEER_PAIRS), jnp.int32),
                        pltpu.VMEM((tb, PEER_PAIRS), jnp.int32),
                        pltpu.VMEM((tb, PEER_PAIRS), F32),
                        pltpu.VMEM((2, PEER_PAIRS, tb), jnp.int32),
                        pltpu.VMEM((2, PEER_PAIRS, tb), jnp.int32),
                        pltpu.VMEM((2, PEER_PAIRS, tb), F32),
                        pltpu.VMEM((2, 2, PEER_NKEYS, tb), F32),
                        pltpu.VMEM((ec // wb, tb, wb), BF16)],
        compiler_params=_params(("arbitrary", "arbitrary"), nbytes),
        name="peer_experts",
    )(h_bf16, q, sub_keys, e1, e2, g, u_blocks, v_bf16, v_bf16, x, gt_arr)


def _swa_permute_in(w_in):
    d = w_in.shape[0]
    wq = w_in[:, :SWA_Q].reshape(d, SWA_KV_HEADS, SWA_GROUP, SWA_HEAD_DIM).transpose(0, 2, 1, 3).reshape(d, SWA_Q)
    return jnp.concatenate([wq, w_in[:, SWA_Q:]], axis=1)


def _swa_permute_out(w_out):
    d = w_out.shape[1]
    return w_out.reshape(SWA_KV_HEADS, SWA_GROUP, SWA_HEAD_DIM, d).transpose(1, 0, 2, 3).reshape(SWA_Q, d)


def _prepare_weights(p):
    w = {}
    w['gla_in'] = [jnp.pad(p['w_gla_in'][j], ((0, 0), (0, GLA_IN_PAD - p['w_gla_in'].shape[2]))).astype(BF16)
                   for j in range(p['w_gla_in'].shape[0])]
    w['gla_a2'] = [jnp.pad(p['w_gla_a2'][j], ((0, V7X_LANES - GLA_GATE_RANK), (0, 0)))
                   for j in range(p['w_gla_a2'].shape[0])]
    w['gla_out'] = [m.astype(BF16) for m in p['w_gla_out']]
    w['swa_in'] = [_swa_permute_in(m).astype(BF16) for m in p['w_swa_in']]
    w['swa_out'] = [_swa_permute_out(m).astype(BF16) for m in p['w_swa_out']]
    w['lru_in'] = [m.astype(BF16) for m in p['w_lru_in']]
    w['lru_out'] = [m.astype(BF16) for m in p['w_lru_out']]
    w['peer_q'] = [m.astype(BF16) for m in p['w_peer_q']]
    w['peer_ut'] = _peer_u_blocks(p['peer_u'])
    w['peer_v'] = p['peer_v'].astype(BF16)
    return w


def _trunk(x3, mod, states, p, w):
    batch, seq_len, d = x3.shape
    x = x3.reshape(batch * seq_len, d)
    new_gla, new_k, new_v, new_conv, new_h = [], [], [], [], []
    gla_stack = None
    for i in range(DEPTH):
        kind, j = i % N_MIXERS, i // N_MIXERS
        sh_m, sc_m, gt_m, sh_f, sc_f, gt_f = [mod[i][:, k * d:(k + 1) * d] for k in range(6)]
        if kind == 0:
            proj = _norm_proj(x, p['g_ln_mix'][i], sc_m, sh_m, w['gla_in'][j], seq_len)
            if states is None:
                mix, s_new = _gla_prompt(proj, w['gla_a2'][j], p['b_gla_a'][j], p['g_gla_norm'][j], batch, seq_len)
                new_gla.append(s_new)
            else:
                mix, gla_stack = _gla_sample(proj, states[0], j, gla_stack, w['gla_a2'][j], p['b_gla_a'][j],
                                             p['g_gla_norm'][j], batch, seq_len)
            w_out = w['gla_out'][j]
        elif kind == 1:
            proj = _norm_proj(x, p['g_ln_mix'][i], sc_m, sh_m, w['swa_in'][j], seq_len)
            if states is None:
                mix, k_n, v_n = _swa_prompt(proj, p['g_swa_q'][j], p['g_swa_k'][j], p['swa_sinks'][j], batch, seq_len)
            else:
                kc = states[1][j].reshape(batch, WINDOW, SWA_KV)
                vc = states[2][j].reshape(batch, WINDOW, SWA_KV)
                mix, k_n, v_n = _swa_sample(proj, kc, vc, p['g_swa_q'][j], p['g_swa_k'][j], p['swa_sinks'][j],
                                            batch, seq_len)
            new_k.append(k_n.reshape(batch, WINDOW, SWA_KV_HEADS, SWA_HEAD_DIM))
            new_v.append(v_n.reshape(batch, WINDOW, SWA_KV_HEADS, SWA_HEAD_DIM))
            w_out = w['swa_out'][j]
        else:
            proj = _norm_proj(x, p['g_ln_mix'][i], sc_m, sh_m, w['lru_in'][j], seq_len)
            lru_args = (p['lru_conv_w'][j], p['lru_conv_b'][j], p['w_lru_ga'][j], p['b_lru_ga'][j],
                        p['w_lru_gx'][j], p['b_lru_gx'][j], p['lru_lam'][j], batch, seq_len)
            assert seq_len >= CONV_WIDTH - 1
            if states is None:
                mix, h_n = _lru_prompt(proj, *lru_args)
            else:
                mix, hs = _lru_sample(proj, states[3][j], states[4][j], *lru_args)
                h_n = hs.reshape(batch, seq_len, D_RNN)[:, -1]
            new_conv.append(proj[:, D_RNN:].reshape(batch, seq_len, D_RNN)[:, seq_len - (CONV_WIDTH - 1):])
            new_h.append(h_n)
            w_out = w['lru_out'][j]
        x = _proj_residual(mix, w_out, x, gt_m, seq_len)
        q, hb = _norm_proj(x, p['g_ln_ffn'][i], sc_f, sh_f, w['peer_q'][i], seq_len, with_h=True)
        x = _peer_experts(hb, q, p['peer_sub_keys'][i], w['peer_ut'], w['peer_v'], i, x, gt_f, seq_len)
    y = x.reshape(batch, seq_len, d)
    gla_out = jnp.stack(new_gla) if states is None else gla_stack
    return y, (gla_out, jnp.stack(new_k), jnp.stack(new_v), jnp.stack(new_conv), jnp.stack(new_h))


def kernel(x_prompt, x_sample, state_gla, cache_swa_k, cache_swa_v, state_lru_conv, state_lru_h,
           c_prompt, c_sample, g_ln_mix, g_ln_ffn, w_mod, b_mod,
           w_gla_in, w_gla_a2, b_gla_a, g_gla_norm, w_gla_out,
           w_swa_in, g_swa_q, g_swa_k, swa_sinks, w_swa_out,
           w_lru_in, lru_conv_w, lru_conv_b, w_lru_ga, b_lru_ga, w_lru_gx, b_lru_gx, lru_lam, w_lru_out,
           w_peer_q, peer_sub_keys, peer_u, peer_v):
    p = {'g_ln_mix': g_ln_mix, 'g_ln_ffn': g_ln_ffn,
         'w_gla_in': w_gla_in, 'w_gla_a2': w_gla_a2, 'b_gla_a': b_gla_a, 'g_gla_norm': g_gla_norm,
         'w_gla_out': w_gla_out,
         'w_swa_in': w_swa_in, 'g_swa_q': g_swa_q, 'g_swa_k': g_swa_k, 'swa_sinks': swa_sinks,
         'w_swa_out': w_swa_out,
         'w_lru_in': w_lru_in, 'lru_conv_w': lru_conv_w, 'lru_conv_b': lru_conv_b,
         'w_lru_ga': w_lru_ga, 'b_lru_ga': b_lru_ga, 'w_lru_gx': w_lru_gx, 'b_lru_gx': b_lru_gx,
         'lru_lam': lru_lam, 'w_lru_out': w_lru_out,
         'w_peer_q': w_peer_q, 'peer_sub_keys': peer_sub_keys, 'peer_u': peer_u, 'peer_v': peer_v}
    w = _prepare_weights(p)
    nb_p, nb_s = c_prompt.shape[0], c_sample.shape[0]
    rows = -(-(nb_p + nb_s) // V7X_SUBLANES) * V7X_SUBLANES
    c_all = jnp.pad(jnp.concatenate([c_prompt, c_sample], axis=0), ((0, rows - nb_p - nb_s), (0, 0)))
    mod = _modulation(c_all, w_mod, b_mod)
    y_p, (gla_p, k_p, v_p, conv_p, h_p) = _trunk(x_prompt, mod[:, :nb_p], None, p, w)
    y_s, (gla_s, k_s, v_s, conv_s, h_s) = _trunk(
        x_sample, mod[:, nb_p:nb_p + nb_s],
        (state_gla, cache_swa_k, cache_swa_v, state_lru_conv, state_lru_h), p, w)
    return (y_p, y_s, gla_p, gla_s, k_p, k_s, v_p, v_s, conv_p, conv_s, h_p, h_s)
```

```python
import functools
import math

import jax
import jax.numpy as jnp
from jax import lax
from jax.experimental import pallas as pl
from jax.experimental.pallas import tpu as pltpu

F32 = jnp.float32
BF16 = jnp.bfloat16

D_MODEL = 1024
DEPTH = 4
N_MIXERS = 3
RMS_EPS = 1e-6

GLA_HEADS = 4
GLA_QK = D_MODEL // 2
GLA_V = D_MODEL
GLA_DK = GLA_QK // GLA_HEADS
GLA_DV = GLA_V // GLA_HEADS
GLA_GATE_RANK = 16
GLA_TAU = 16.0
GLA_SUB = 16
GLA_CHUNK = 128
GLA_IN_PAD = 2 * GLA_QK + 2 * GLA_V + 128

SWA_HEAD_DIM = 64
SWA_Q_HEADS = D_MODEL // SWA_HEAD_DIM
SWA_KV_HEADS = 4
SWA_GROUP = SWA_Q_HEADS // SWA_KV_HEADS
SWA_Q = SWA_Q_HEADS * SWA_HEAD_DIM
SWA_KV = SWA_KV_HEADS * SWA_HEAD_DIM
WINDOW = 128

D_RNN = D_MODEL
LRU_BLOCKS = 4
LRU_BLOCK = D_RNN // LRU_BLOCKS
CONV_WIDTH = 4
LRU_C = 8.0

PEER_HEADS = 8
PEER_NKEYS = 128
PEER_EXPERTS = PEER_NKEYS * PEER_NKEYS
PEER_KEY_DIM = 256
PEER_HALF = PEER_KEY_DIM // 2
PEER_TOPK = 16
PEER_PAIRS = PEER_HEADS * PEER_TOPK

V7X_LANES = 128
V7X_SUBLANES = 8
V7X_VMEM_BYTES = 64 * 1024 * 1024

TOKEN_TILE = 512
NEG_BIG = -1e30


def _vmem_limit(nbytes):
    return int(min(max(nbytes * 3 // 2, 16 * 1024 * 1024), V7X_VMEM_BYTES - 8 * 1024 * 1024))


def _params(semantics, nbytes):
    return pltpu.CompilerParams(dimension_semantics=semantics, vmem_limit_bytes=_vmem_limit(nbytes))


def _rms(x, g):
    return x * lax.rsqrt(jnp.mean(x * x, axis=-1, keepdims=True) + RMS_EPS) * g


def _gelu_tanh(x):
    return 0.5 * x * (1.0 + jnp.tanh(math.sqrt(2.0 / math.pi) * (x + 0.044715 * (x * x * x))))


def _gelu_times_half_gate(x, half_gate):
    c1 = math.sqrt(2.0 / math.pi)
    inner = x * (c1 + (c1 * 0.044715) * (x * x))
    return (x * half_gate) * (1.0 + jnp.tanh(inner))


def _sigmoid(x):
    return 1.0 / (1.0 + jnp.exp(-x))


def _softplus(x):
    return jnp.maximum(x, 0.0) + jnp.log1p(jnp.exp(-jnp.abs(x)))


def _row_operand(vec, seq_len, tile):
    b, d = vec.shape
    if seq_len % tile == 0:
        per_seq = seq_len // tile
        return vec.reshape(b, 1, d), pl.BlockSpec((1, 1, d), lambda i: (i // per_seq, 0, 0))
    assert tile % seq_len == 0
    rep = jnp.repeat(vec, seq_len, axis=0).reshape(b * seq_len // tile, tile, d)
    return rep, pl.BlockSpec((1, tile, d), lambda i: (i, 0, 0))


def _mod_kernel(c_ref, w_ref, b_ref, o_ref):
    c = c_ref[...]
    sc = c * _sigmoid(c)
    o_ref[0] = jnp.dot(sc, w_ref[0], preferred_element_type=F32) + b_ref[0]


def _modulation(c, w_mod, b_mod):
    bp, d = c.shape
    tn = 1024
    nt = 6 * d // tn
    return pl.pallas_call(
        _mod_kernel,
        grid=(DEPTH, nt),
        in_specs=[pl.BlockSpec((bp, d), lambda l, j: (0, 0)),
                  pl.BlockSpec((1, d, tn), lambda l, j: (l, 0, j)),
                  pl.BlockSpec((1, 1, tn), lambda l, j: (l, 0, j))],
        out_specs=pl.BlockSpec((1, bp, tn), lambda l, j: (l, 0, j)),
        out_shape=jax.ShapeDtypeStruct((DEPTH, bp, 6 * d), F32),
        compiler_params=_params(("parallel", "parallel"), 2 * (d * tn * 4 + 2 * bp * tn * 4)),
        name="modulation",
    )(c, w_mod, b_mod.reshape(DEPTH, 1, 6 * d))


def _norm_proj_kernel(x_ref, g_ref, sc_ref, sh_ref, w_ref, o_ref, *h_ref):
    h = _rms(x_ref[...], g_ref[...]) * (1.0 + sc_ref[0]) + sh_ref[0]
    hb = h.astype(BF16)
    o_ref[...] = jnp.dot(hb, w_ref[...], preferred_element_type=F32).astype(o_ref.dtype)
    if h_ref:
        h_ref[0][...] = hb


def _norm_proj(x, g, scale, shift, w_bf16, seq_len, with_h=False):
    n, d = x.shape
    nout = w_bf16.shape[1]
    tm = TOKEN_TILE
    sc_arr, sc_spec = _row_operand(scale, seq_len, tm)
    sh_arr, sh_spec = _row_operand(shift, seq_len, tm)
    out_shape = [jax.ShapeDtypeStruct((n, nout), BF16 if with_h else F32)]
    out_specs = [pl.BlockSpec((tm, nout), lambda i: (i, 0))]
    if with_h:
        out_shape.append(jax.ShapeDtypeStruct((n, d), BF16))
        out_specs.append(pl.BlockSpec((tm, d), lambda i: (i, 0)))
    nbytes = 2 * (tm * d * 4 + d * nout * 2 + tm * nout * 4 + 3 * tm * d * 4)
    res = pl.pallas_call(
        _norm_proj_kernel,
        grid=(n // tm,),
        in_specs=[pl.BlockSpec((tm, d), lambda i: (i, 0)),
                  pl.BlockSpec((1, d), lambda i: (0, 0)),
                  sc_spec, sh_spec,
                  pl.BlockSpec((d, nout), lambda i: (0, 0))],
        out_specs=out_specs,
        out_shape=out_shape,
        compiler_params=_params(("parallel",), nbytes),
        name="norm_proj",
    )(x, g.reshape(1, d), sc_arr, sh_arr, w_bf16)
    return res if with_h else res[0]


def _proj_residual_kernel(a_ref, w_ref, x_ref, gt_ref, o_ref):
    y = jnp.dot(a_ref[...].astype(BF16), w_ref[...], preferred_element_type=F32)
    o_ref[...] = x_ref[...] + gt_ref[0] * y


def _proj_residual(a, w_bf16, x, gate, seq_len):
    n, k = a.shape
    d = x.shape[1]
    tm = TOKEN_TILE
    gt_arr, gt_spec = _row_operand(gate, seq_len, tm)
    nbytes = 2 * (tm * k * 4 + k * d * 2 + 3 * tm * d * 4)
    return pl.pallas_call(
        _proj_residual_kernel,
        grid=(n // tm,),
        in_specs=[pl.BlockSpec((tm, k), lambda i: (i, 0)),
                  pl.BlockSpec((k, d), lambda i: (0, 0)),
                  pl.BlockSpec((tm, d), lambda i: (i, 0)),
                  gt_spec],
        out_specs=pl.BlockSpec((tm, d), lambda i: (i, 0)),
        out_shape=jax.ShapeDtypeStruct((n, d), F32),
        compiler_params=_params(("parallel",), nbytes),
        name="proj_residual",
    )(a, w_bf16, x, gt_arr)


def _log_decay(lr, wa2, ba):
    z = jnp.dot(lr, wa2, preferred_element_type=F32) + ba
    return (jnp.minimum(z, 0.0) - jnp.log1p(jnp.exp(-jnp.abs(z)))) * (1.0 / GLA_TAU)


def _col_bcast(row):
    return jnp.transpose(jnp.broadcast_to(row, (V7X_LANES, V7X_LANES)))


def _head_out(o, gate, gn):
    return _rms(o, gn) * (gate * _sigmoid(gate))


def _gla_prompt_kernel(p_ref, wa2_ref, ba_ref, gn_ref, o_ref, s_ref):
    c = pl.program_id(0)
    C = GLA_CHUNK
    nsub = C // GLA_SUB

    @pl.when(c == 0)
    def _():
        s_ref[...] = jnp.zeros_like(s_ref)

    row = lax.broadcasted_iota(jnp.int32, (C, C), 0)
    col = lax.broadcasted_iota(jnp.int32, (C, C), 1)
    tri = (col <= row).astype(F32)
    later = ((col > row) & (col // GLA_SUB == row // GLA_SUB)).astype(F32)
    sums = jnp.concatenate([tri, later], axis=0)
    causal = col <= row
    rsub = lax.broadcasted_iota(jnp.int32, (C, GLA_DK), 0) // GLA_SUB

    for seq in range(p_ref.shape[0]):
        _gla_prompt_chunk(p_ref.at[seq], wa2_ref, ba_ref, gn_ref, o_ref.at[seq], s_ref.at[seq],
                          sums, causal, rsub)


def _gla_prompt_chunk(p_ref, wa2_ref, ba_ref, gn_ref, o_ref, s_ref, sums, causal, rsub):
    C = GLA_CHUNK
    nsub = C // GLA_SUB
    lr = p_ref[:, 2 * GLA_QK + 2 * GLA_V:]
    la_all = _log_decay(lr, wa2_ref[...], ba_ref[...])
    cs_all = jnp.dot(sums, la_all, preferred_element_type=F32, precision=lax.Precision.HIGHEST)
    for h in range(GLA_HEADS):
        q = p_ref[:, h * GLA_DK:(h + 1) * GLA_DK] * (GLA_DK ** -0.5)
        k = p_ref[:, GLA_QK + h * GLA_DK:GLA_QK + (h + 1) * GLA_DK]
        v = p_ref[:, 2 * GLA_QK + h * GLA_DV:2 * GLA_QK + (h + 1) * GLA_DV]
        gate = p_ref[:, 2 * GLA_QK + GLA_V + h * GLA_DV:2 * GLA_QK + GLA_V + (h + 1) * GLA_DV]
        cs = cs_all[:, h * GLA_DK:(h + 1) * GLA_DK]
        b = cs[:C]
        to_sub_end = cs[C:]
        b_last = b[C - 1:C]
        k_sub = k * jnp.exp(to_sub_end)
        q_parts, k_parts = [], []
        for m in range(nsub):
            ref_row = b[m * GLA_SUB + GLA_SUB - 1:m * GLA_SUB + GLA_SUB]
            e = jnp.where(rsub >= m, b - ref_row, NEG_BIG)
            q_parts.append((q * jnp.exp(e)).astype(BF16))
            k_parts.append(jnp.where(rsub == m, k_sub, 0.0).astype(BF16))
        qcat = jnp.concatenate(q_parts, axis=1)
        kcat = jnp.concatenate(k_parts, axis=1)
        att = lax.dot_general(qcat, kcat, (((1,), (1,)), ((), ())), preferred_element_type=F32)
        att = jnp.where(causal, att, 0.0)
        s_old = s_ref[h]
        o = jnp.dot(att.astype(BF16), v.astype(BF16), preferred_element_type=F32)
        o = o + jnp.dot((q * jnp.exp(b)).astype(BF16), s_old.astype(BF16), preferred_element_type=F32)
        k_end = (k * jnp.exp(b_last - b)).astype(BF16)
        upd = lax.dot_general(k_end, v.astype(BF16), (((0,), (0,)), ((), ())), preferred_element_type=F32)
        decay = _col_bcast(jnp.exp(b_last))
        s_ref[h] = jnp.concatenate([decay] * (GLA_DV // V7X_LANES), axis=1) * s_old + upd
        o_ref[:, h * GLA_DV:(h + 1) * GLA_DV] = _head_out(o, gate, gn_ref[...])


def _gla_prompt(proj, wa2_pad, ba, gn, batch, seq_len):
    C = GLA_CHUNK
    p3 = proj.reshape(batch, seq_len, GLA_IN_PAD)
    nbytes = batch * (2 * (C * GLA_IN_PAD * 4 + C * GLA_V * 4 + GLA_HEADS * GLA_DK * GLA_DV * 4) + 64 * C * C * 4)
    o, s = pl.pallas_call(
        _gla_prompt_kernel,
        grid=(seq_len // C,),
        in_specs=[pl.BlockSpec((batch, C, GLA_IN_PAD), lambda c: (0, c, 0)),
                  pl.BlockSpec((V7X_LANES, GLA_QK), lambda c: (0, 0)),
                  pl.BlockSpec((1, GLA_QK), lambda c: (0, 0)),
                  pl.BlockSpec((1, GLA_DV), lambda c: (0, 0))],
        out_specs=[pl.BlockSpec((batch, C, GLA_V), lambda c: (0, c, 0)),
                   pl.BlockSpec((batch, GLA_HEADS, GLA_DK, GLA_DV), lambda c: (0, 0, 0, 0))],
        out_shape=[jax.ShapeDtypeStruct((batch, seq_len, GLA_V), F32),
                   jax.ShapeDtypeStruct((batch, GLA_HEADS, GLA_DK, GLA_DV), F32)],
        compiler_params=_params(("arbitrary",), nbytes),
        name="gla_prompt",
    )(p3, wa2_pad, ba.reshape(1, GLA_QK), gn.reshape(1, GLA_DV))
    return o.reshape(batch * seq_len, GLA_V), s


GLA_SAMPLE_BATCH = 8


def _gla_sample_kernel(p_ref, s0_ref, wa2_ref, ba_ref, gn_ref, *refs):
    o_ref, s_ref = refs[-2:]
    nseq = s0_ref.shape[1]
    rows = p_ref.shape[0]
    T = rows // nseq
    row = lax.broadcasted_iota(jnp.int32, (rows, rows), 0)
    col = lax.broadcasted_iota(jnp.int32, (rows, rows), 1)
    same_seq = (row // T) == (col // T)
    causal = same_seq & (col <= row)
    sums = jnp.concatenate([causal.astype(F32), same_seq.astype(F32)], axis=0)
    lr = p_ref[:, 2 * GLA_QK + 2 * GLA_V:]
    la = _log_decay(lr, wa2_ref[...], ba_ref[...])
    cs = jnp.dot(sums, la, preferred_element_type=F32, precision=lax.Precision.HIGHEST)
    b_all, b_last_all = cs[:rows], cs[rows:]
    for h in range(GLA_HEADS):
        hk = slice(h * GLA_DK, (h + 1) * GLA_DK)
        q = p_ref[:, h * GLA_DK:(h + 1) * GLA_DK] * (GLA_DK ** -0.5)
        k = p_ref[:, GLA_QK + h * GLA_DK:GLA_QK + (h + 1) * GLA_DK]
        v = p_ref[:, 2 * GLA_QK + h * GLA_DV:2 * GLA_QK + (h + 1) * GLA_DV]
        gate = p_ref[:, 2 * GLA_QK + GLA_V + h * GLA_DV:2 * GLA_QK + GLA_V + (h + 1) * GLA_DV]
        b, b_last = b_all[:, hk], b_last_all[:, hk]
        k_end = k * jnp.exp(b_last - b)
        q_rel = (q * jnp.exp(b - b_last)).astype(BF16)
        q_dec = q * jnp.exp(b)
        att = lax.dot_general(q_rel, k_end.astype(BF16), (((1,), (1,)), ((), ())), preferred_element_type=F32)
        att = jnp.where(causal, att, 0.0)
        o_intra = jnp.dot(att.astype(BF16), v.astype(BF16), preferred_element_type=F32)
        o_inter = []
        for j in range(nseq):
            rj = slice(j * T, (j + 1) * T)
            s_old = s0_ref[0, j, h]
            o_inter.append(jnp.dot(q_dec[rj].astype(BF16), s_old.astype(BF16), preferred_element_type=F32))
            upd = lax.dot_general(k_end[rj].astype(BF16), v[rj].astype(BF16), (((0,), (0,)), ((), ())),
                                  preferred_element_type=F32)
            decay = _col_bcast(jnp.exp(b_last[j * T:j * T + 1]))
            s_ref[0, j, h] = jnp.concatenate([decay] * (GLA_DV // V7X_LANES), axis=1) * s_old + upd
        o = o_intra + jnp.concatenate(o_inter, axis=0)
        o_ref[:, h * GLA_DV:(h + 1) * GLA_DV] = _head_out(o, gate, gn_ref[...])


def _gla_sample(proj, states, layer, new_states, wa2_pad, ba, gn, batch, seq_len):
    assert seq_len <= GLA_SUB
    nb = GLA_SAMPLE_BATCH
    rows = nb * seq_len
    state_spec = pl.BlockSpec((1, nb, GLA_HEADS, GLA_DK, GLA_DV), lambda b: (layer, b, 0, 0, 0))
    state_block = nb * GLA_HEADS * GLA_DK * GLA_DV * 4
    nbytes = 2 * (rows * (GLA_IN_PAD + GLA_V) * 4 + 2 * state_block) + 16 * rows * GLA_IN_PAD * 4
    in_specs = [pl.BlockSpec((rows, GLA_IN_PAD), lambda b: (b, 0)),
                state_spec,
                pl.BlockSpec((V7X_LANES, GLA_QK), lambda b: (0, 0)),
                pl.BlockSpec((1, GLA_QK), lambda b: (0, 0)),
                pl.BlockSpec((1, GLA_DV), lambda b: (0, 0))]
    operands = [proj, states, wa2_pad, ba.reshape(1, GLA_QK), gn.reshape(1, GLA_DV)]
    aliases = {}
    if new_states is not None:
        in_specs.append(pl.BlockSpec(memory_space=pl.ANY))
        operands.append(new_states)
        aliases = {len(operands) - 1: 1}
    o, s = pl.pallas_call(
        _gla_sample_kernel,
        grid=(batch // nb,),
        in_specs=in_specs,
        out_specs=[pl.BlockSpec((rows, GLA_V), lambda b: (b, 0)), state_spec],
        out_shape=[jax.ShapeDtypeStruct((batch * seq_len, GLA_V), F32),
                   jax.ShapeDtypeStruct(states.shape, F32)],
        input_output_aliases=aliases,
        compiler_params=_params(("parallel",), nbytes),
        name="gla_sample",
    )(*operands)
    return o, s


def _head_group_norm(x, gain, gsum):
    sq = x * x
    hi = sq.astype(BF16)
    lo = (sq - hi.astype(F32)).astype(BF16)
    ms = (jnp.dot(hi, gsum, preferred_element_type=F32) + jnp.dot(lo, gsum, preferred_element_type=F32))
    return x * lax.rsqrt(ms * (1.0 / SWA_HEAD_DIM) + RMS_EPS) * gain


def _swa_attend(q_groups, k_all, v_all, mask, sink_ref):
    tq = q_groups[0].shape[0]
    lane_head = lax.broadcasted_iota(jnp.int32, (tq, SWA_KV), 1) // SWA_HEAD_DIM
    mask_rows = jnp.concatenate([mask] * SWA_GROUP, axis=0)
    out = [jnp.zeros((tq, SWA_KV), F32) for _ in range(SWA_GROUP)]
    for kv in range(SWA_KV_HEADS):
        in_head = lane_head == kv
        qs = jnp.concatenate([jnp.where(in_head, qg, 0.0) for qg in q_groups], axis=0).astype(BF16)
        s = lax.dot_general(qs, k_all, (((1,), (1,)), ((), ())), preferred_element_type=F32)
        s = jnp.where(mask_rows, s, -jnp.inf)
        sink = jnp.concatenate(
            [jnp.full((tq, 1), sink_ref[kv * SWA_GROUP + g], F32) for g in range(SWA_GROUP)], axis=0)
        m = jnp.maximum(jnp.max(s, axis=1, keepdims=True), sink)
        p = jnp.exp(s - m)
        denom = jnp.sum(p, axis=1, keepdims=True) + jnp.exp(sink - m)
        pv = jnp.dot(p.astype(BF16), v_all, preferred_element_type=F32) / denom
        for g in range(SWA_GROUP):
            out[g] = jnp.where(in_head, pv[g * tq:(g + 1) * tq], out[g])
    return out


SWA_STEP_WINDOWS = 4


def _swa_prompt_kernel(sink_ref, cur_ref, prev_ref, gq_ref, gk_ref, gsum_ref, o_ref, k_ref, v_ref):
    n = pl.program_id(1)
    W = WINDOW
    gsum = gsum_ref[...]
    k_cur = _head_group_norm(cur_ref[0, :, SWA_Q:SWA_Q + SWA_KV], gk_ref[...], gsum)
    k_prev = _head_group_norm(prev_ref[0, :, SWA_Q:SWA_Q + SWA_KV], gk_ref[...], gsum)
    k_rows = jnp.concatenate([k_prev, k_cur], axis=0).astype(BF16)
    v_rows = jnp.concatenate([prev_ref[0, :, SWA_Q + SWA_KV:], cur_ref[0, :, SWA_Q + SWA_KV:]], axis=0).astype(BF16)
    t = lax.broadcasted_iota(jnp.int32, (W, 2 * W), 0)
    s = lax.broadcasted_iota(jnp.int32, (W, 2 * W), 1)
    band = (s >= t) & (s <= t + W)
    for w in range(SWA_STEP_WINDOWS):
        rows = slice(w * W, (w + 1) * W)
        mask = band if w > 0 else band & ((s >= W) | (n > 0))
        q_groups = [_head_group_norm(cur_ref[0, rows, g * SWA_KV:(g + 1) * SWA_KV], gq_ref[...], gsum)
                    * (SWA_HEAD_DIM ** -0.5) for g in range(SWA_GROUP)]
        out = _swa_attend(q_groups, k_rows[w * W:(w + 2) * W], v_rows[w * W:(w + 2) * W], mask, sink_ref)
        o_ref[0, rows, :] = jnp.concatenate(out, axis=1)
    last = slice((SWA_STEP_WINDOWS - 1) * W, SWA_STEP_WINDOWS * W)
    k_ref[0] = k_cur[last]
    v_ref[0] = cur_ref[0, last, SWA_Q + SWA_KV:]


def _swa_gsum():
    head = jnp.arange(SWA_KV) // SWA_HEAD_DIM
    return (head[:, None] == head[None, :]).astype(BF16)


def _swa_prompt(proj, gq, gk, sinks, batch, seq_len):
    W = WINDOW
    nw = SWA_STEP_WINDOWS
    width = SWA_Q + 2 * SWA_KV
    p3 = proj.reshape(batch, seq_len, width)
    nbytes = 2 * ((nw + 1) * W * width * 4 + nw * W * SWA_Q * 4 + 2 * W * SWA_KV * 4) + nw * 48 * W * 2 * W * 4
    o, k, v = pl.pallas_call(
        _swa_prompt_kernel,
        grid=(batch, seq_len // (nw * W)),
        in_specs=[pl.BlockSpec(memory_space=pltpu.SMEM),
                  pl.BlockSpec((1, nw * W, width), lambda b, n: (b, n, 0)),
                  pl.BlockSpec((1, W, width), lambda b, n: (b, jnp.maximum(nw * n - 1, 0), 0)),
                  pl.BlockSpec((1, SWA_KV), lambda b, n: (0, 0)),
                  pl.BlockSpec((1, SWA_KV), lambda b, n: (0, 0)),
                  pl.BlockSpec((SWA_KV, SWA_KV), lambda b, n: (0, 0))],
        out_specs=[pl.BlockSpec((1, nw * W, SWA_Q), lambda b, n: (b, n, 0)),
                   pl.BlockSpec((1, W, SWA_KV), lambda b, n: (b, 0, 0)),
                   pl.BlockSpec((1, W, SWA_KV), lambda b, n: (b, 0, 0))],
        out_shape=[jax.ShapeDtypeStruct((batch, seq_len, SWA_Q), F32),
                   jax.ShapeDtypeStruct((batch, W, SWA_KV), F32),
                   jax.ShapeDtypeStruct((batch, W, SWA_KV), F32)],
        compiler_params=_params(("parallel", "arbitrary"), nbytes),
        name="swa_prompt",
    )(sinks, p3, p3, jnp.tile(gq, SWA_KV_HEADS).reshape(1, SWA_KV), jnp.tile(gk, SWA_KV_HEADS).reshape(1, SWA_KV),
      _swa_gsum())
    return o.reshape(batch * seq_len, SWA_Q), k, v


SWA_SAMPLE_BATCH = 8


def _swa_sample_kernel(sink_ref, p_ref, kc_ref, vc_ref, gq_ref, gk_ref, gsum_ref, o_ref, k_ref, v_ref):
    T = p_ref.shape[1]
    gsum = gsum_ref[...]
    t = lax.broadcasted_iota(jnp.int32, (T, WINDOW + T), 0)
    s = lax.broadcasted_iota(jnp.int32, (T, WINDOW + T), 1)
    mask = (s >= t) & (s <= t + WINDOW)
    for j in range(p_ref.shape[0]):
        k_new = _head_group_norm(p_ref[j, :, SWA_Q:SWA_Q + SWA_KV], gk_ref[...], gsum)
        v_new = p_ref[j, :, SWA_Q + SWA_KV:]
        k_all = jnp.concatenate([kc_ref[j], k_new], axis=0)
        v_all = jnp.concatenate([vc_ref[j], v_new], axis=0)
        q_groups = [_head_group_norm(p_ref[j, :, g * SWA_KV:(g + 1) * SWA_KV], gq_ref[...], gsum)
                    * (SWA_HEAD_DIM ** -0.5) for g in range(SWA_GROUP)]
        out = _swa_attend(q_groups, k_all.astype(BF16), v_all.astype(BF16), mask, sink_ref)
        o_ref[j] = jnp.concatenate(out, axis=1)
        k_ref[j] = k_all[T:]
        v_ref[j] = v_all[T:]


def _swa_sample(proj, k_cache, v_cache, gq, gk, sinks, batch, seq_len):
    nb = SWA_SAMPLE_BATCH
    W = WINDOW
    width = SWA_Q + 2 * SWA_KV
    p3 = proj.reshape(batch, seq_len, width)
    nbytes = 2 * (nb * seq_len * (width + SWA_Q) * 4 + 4 * nb * W * SWA_KV * 4)
    o, k, v = pl.pallas_call(
        _swa_sample_kernel,
        grid=(batch // nb,),
        in_specs=[pl.BlockSpec(memory_space=pltpu.SMEM),
                  pl.BlockSpec((nb, seq_len, width), lambda b: (b, 0, 0)),
                  pl.BlockSpec((nb, W, SWA_KV), lambda b: (b, 0, 0)),
                  pl.BlockSpec((nb, W, SWA_KV), lambda b: (b, 0, 0)),
                  pl.BlockSpec((1, SWA_KV), lambda b: (0, 0)),
                  pl.BlockSpec((1, SWA_KV), lambda b: (0, 0)),
                  pl.BlockSpec((SWA_KV, SWA_KV), lambda b: (0, 0))],
        out_specs=[pl.BlockSpec((nb, seq_len, SWA_Q), lambda b: (b, 0, 0)),
                   pl.BlockSpec((nb, W, SWA_KV), lambda b: (b, 0, 0)),
                   pl.BlockSpec((nb, W, SWA_KV), lambda b: (b, 0, 0))],
        out_shape=[jax.ShapeDtypeStruct((batch, seq_len, SWA_Q), F32),
                   jax.ShapeDtypeStruct((batch, W, SWA_KV), F32),
                   jax.ShapeDtypeStruct((batch, W, SWA_KV), F32)],
        compiler_params=_params(("parallel",), nbytes),
        name="swa_sample",
    )(sinks, p3, k_cache, v_cache, jnp.tile(gq, SWA_KV_HEADS).reshape(1, SWA_KV),
      jnp.tile(gk, SWA_KV_HEADS).reshape(1, SWA_KV), _swa_gsum())
    return o.reshape(batch * seq_len, SWA_Q), k, v


def _lru_conv(x, shifted, cw_ref, cb_ref):
    y = cb_ref[...] + cw_ref[CONV_WIDTH - 1:CONV_WIDTH] * x
    for s in range(1, CONV_WIDTH):
        y = y + cw_ref[CONV_WIDTH - 1 - s:CONV_WIDTH - s] * shifted[s - 1]
    return y


def _block_diag_dot(x, w_ref):
    xb = x.astype(BF16)
    return jnp.concatenate(
        [jnp.dot(xb[:, n * LRU_BLOCK:(n + 1) * LRU_BLOCK], w_ref[n], preferred_element_type=F32)
         for n in range(LRU_BLOCKS)], axis=1)


def _lru_terms(xc, wga_ref, bga_ref, wgx_ref, bgx_ref, lam_ref):
    r = _sigmoid(_block_diag_dot(xc, wga_ref) + bga_ref[...])
    i = _sigmoid(_block_diag_dot(xc, wgx_ref) + bgx_ref[...])
    log_a = (-LRU_C) * r * _softplus(-lam_ref[...])
    a = jnp.exp(log_a)
    y2 = 2.0 * log_a
    u = a * a
    em1 = jnp.where(u == 1.0, y2, jnp.where(u == 0.0, -1.0, (u - 1.0) * y2 / jnp.log(u)))
    mult = jnp.sqrt(-em1)
    return a, mult * i * xc


def _scan_rows(a, b, group):
    rows = a.shape[0]
    pos = lax.broadcasted_iota(jnp.int32, a.shape, 0) % group
    d = 1
    while d < group:
        keep = pos >= d
        b = jnp.where(keep, a * pltpu.roll(b, d, 0) + b, b)
        a = jnp.where(keep, a * pltpu.roll(a, d, 0), a)
        d *= 2
    return a, b


def _lru_prompt_kernel(p_ref, cw_ref, cb_ref, wga_ref, bga_ref, wgx_ref, bgx_ref, lam_ref,
                       o_ref, h_ref, tail_ref, hc_ref):
    n = pl.program_id(1)
    rows = p_ref.shape[1]

    @pl.when(n == 0)
    def _():
        tail_ref[...] = jnp.zeros_like(tail_ref)
        hc_ref[...] = jnp.zeros_like(hc_ref)

    y = p_ref[0, :, :D_RNN]
    x = p_ref[0, :, D_RNN:]
    tail = tail_ref[...]
    r8 = lax.broadcasted_iota(jnp.int32, (V7X_SUBLANES, D_RNN), 0)
    shifted = []
    for s in range(1, CONV_WIDTH):
        xs = pltpu.roll(x, s, 0)
        head = jnp.where(r8 < s, pltpu.roll(tail, s, 0), xs[:V7X_SUBLANES])
        shifted.append(jnp.concatenate([head, xs[V7X_SUBLANES:]], axis=0))
    xc = _lru_conv(x, shifted, cw_ref, cb_ref)
    a, bterm = _lru_terms(xc, wga_ref, bga_ref, wgx_ref, bgx_ref, lam_ref)
    acum, hzero = _scan_rows(a, bterm, rows)
    hs = acum * hc_ref[0:1] + hzero
    o_ref[0] = _gelu_tanh(y) * hs
    last = hs[rows - 1:rows]
    h_ref[0] = last
    hc_ref[...] = jnp.broadcast_to(last, hc_ref.shape)
    tail_ref[...] = x[rows - V7X_SUBLANES:]


LRU_TILE = 256


def _lru_weight_specs(imap):
    return [pl.BlockSpec((CONV_WIDTH, D_RNN), imap(2)),
            pl.BlockSpec((1, D_RNN), imap(2)),
            pl.BlockSpec((LRU_BLOCKS, LRU_BLOCK, LRU_BLOCK), imap(3)),
            pl.BlockSpec((1, D_RNN), imap(2)),
            pl.BlockSpec((LRU_BLOCKS, LRU_BLOCK, LRU_BLOCK), imap(3)),
            pl.BlockSpec((1, D_RNN), imap(2)),
            pl.BlockSpec((1, D_RNN), imap(2))]


def _lru_prompt(proj, cw, cb, wga, bga, wgx, bgx, lam, batch, seq_len):
    R = LRU_TILE
    p3 = proj.reshape(batch, seq_len, 2 * D_RNN)
    nbytes = 2 * (R * 3 * D_RNN * 4 + 2 * LRU_BLOCKS * LRU_BLOCK * LRU_BLOCK * 2) + 24 * R * D_RNN * 4
    o, h = pl.pallas_call(
        _lru_prompt_kernel,
        grid=(batch, seq_len // R),
        in_specs=[pl.BlockSpec((1, R, 2 * D_RNN), lambda b, n: (b, n, 0))]
        + _lru_weight_specs(lambda nd: (lambda b, n: (0,) * nd)),
        out_specs=[pl.BlockSpec((1, R, D_RNN), lambda b, n: (b, n, 0)),
                   pl.BlockSpec((1, 1, D_RNN), lambda b, n: (b, 0, 0))],
        out_shape=[jax.ShapeDtypeStruct((batch, seq_len, D_RNN), F32),
                   jax.ShapeDtypeStruct((batch, 1, D_RNN), F32)],
        scratch_shapes=[pltpu.VMEM((V7X_SUBLANES, D_RNN), F32), pltpu.VMEM((V7X_SUBLANES, D_RNN), F32)],
        compiler_params=_params(("parallel", "arbitrary"), nbytes),
        name="lru_prompt",
    )(p3, cw, cb.reshape(1, D_RNN), wga.astype(BF16), bga.reshape(1, D_RNN), wgx.astype(BF16),
      bgx.reshape(1, D_RNN), lam.reshape(1, D_RNN))
    return o.reshape(batch * seq_len, D_RNN), h.reshape(batch, D_RNN)


def _lru_sample_kernel(p_ref, prev_ref, h0_ref, cw_ref, cb_ref, wga_ref, bga_ref, wgx_ref, bgx_ref, lam_ref,
                       o_ref, hs_ref, *, seq_len):
    rows = p_ref.shape[0]
    y = p_ref[:, :D_RNN]
    x = p_ref[:, D_RNN:]
    prev = prev_ref[...]
    pos = lax.broadcasted_iota(jnp.int32, (rows, D_RNN), 0) % seq_len
    shifted = [jnp.where(pos < s, pltpu.roll(prev, rows - seq_len + s, 0), pltpu.roll(x, s, 0))
               for s in range(1, CONV_WIDTH)]
    xc = _lru_conv(x, shifted, cw_ref, cb_ref)
    a, bterm = _lru_terms(xc, wga_ref, bga_ref, wgx_ref, bgx_ref, lam_ref)
    acum, hzero = _scan_rows(a, bterm, seq_len)
    hs = acum * h0_ref[...] + hzero
    o_ref[...] = _gelu_tanh(y) * hs
    hs_ref[...] = hs


def _lru_sample(proj, conv_state, h0, cw, cb, wga, bga, wgx, bgx, lam, batch, seq_len):
    assert seq_len == V7X_SUBLANES
    n = batch * seq_len
    R = LRU_TILE
    prev = jnp.pad(conv_state, ((0, 0), (seq_len - (CONV_WIDTH - 1), 0), (0, 0))).reshape(n, D_RNN)
    h0_rows = jnp.repeat(h0, seq_len, axis=0)
    nbytes = 2 * (R * 6 * D_RNN * 4 + 2 * LRU_BLOCKS * LRU_BLOCK * LRU_BLOCK * 2) + 24 * R * D_RNN * 4
    return pl.pallas_call(
        functools.partial(_lru_sample_kernel, seq_len=seq_len),
        grid=(n // R,),
        in_specs=[pl.BlockSpec((R, 2 * D_RNN), lambda i: (i, 0)),
                  pl.BlockSpec((R, D_RNN), lambda i: (i, 0)),
                  pl.BlockSpec((R, D_RNN), lambda i: (i, 0))]
        + _lru_weight_specs(lambda nd: (lambda i: (0,) * nd)),
        out_specs=[pl.BlockSpec((R, D_RNN), lambda i: (i, 0)),
                   pl.BlockSpec((R, D_RNN), lambda i: (i, 0))],
        out_shape=[jax.ShapeDtypeStruct((n, D_RNN), F32), jax.ShapeDtypeStruct((n, D_RNN), F32)],
        compiler_params=_params(("parallel",), nbytes),
        name="lru_sample",
    )(proj, prev, h0_rows, cw, cb.reshape(1, D_RNN), wga.astype(BF16), bga.reshape(1, D_RNN), wgx.astype(BF16),
      bgx.reshape(1, D_RNN), lam.reshape(1, D_RNN))


def _run(gen):
    while True:
        try:
            next(gen)
        except StopIteration as stop:
            return stop.value


def _lockstep(gens):
    results = [None] * len(gens)
    live = list(range(len(gens)))
    anchor = None
    while live:
        for idx in list(live):
            try:
                gens[idx].send(anchor)
            except StopIteration as stop:
                results[idx] = stop.value
                live.remove(idx)
        if live:
            anchor = yield
    return results


def _topk_rows_steps(s, k):
    n = s.shape[0]
    rid = lax.broadcasted_iota(jnp.int32, s.shape, 0).astype(F32)
    vals, ids = [], []
    for _ in range(k):
        m = jnp.max(s, axis=0, keepdims=True)
        ix = jnp.min(jnp.where(s == m, rid, float(n)), axis=0, keepdims=True)
        vals.append(m)
        ids.append(ix)
        s = jnp.where(rid == ix, -jnp.inf, s)
        anchor = yield
        if anchor is not None:
            s = s + anchor
    return jnp.concatenate(vals, axis=0), jnp.concatenate(ids, axis=0).astype(jnp.int32)


def _zero_from(parts):
    acc = None
    for x in parts:
        bits = pltpu.bitcast(x, jnp.uint32)
        bits = bits.reshape(bits.shape[0] // V7X_SUBLANES, V7X_SUBLANES, bits.shape[1])
        folded = bits[0]
        for r in range(1, bits.shape[0]):
            folded = folded | bits[r]
        acc = folded if acc is None else acc | folded
    cols = [acc[:, t * V7X_LANES:(t + 1) * V7X_LANES] for t in range(acc.shape[1] // V7X_LANES)]
    one = cols[0]
    for t in cols[1:]:
        one = one | t
    zero = lax.shift_right_logical(lax.shift_right_logical(one, jnp.uint32(16)), jnp.uint32(16))
    return pltpu.bitcast(zero, F32)[0:1, 0:1]


def _staircase_candidates(s1, s2):
    K = s1.shape[0]
    sub = V7X_SUBLANES
    first_single = next(a for a in range(K) if K // (a + 1) == 1)
    assert (K - first_single) % sub == 0
    pieces, starts, at = [], [], 0
    for a in range(first_single):
        nb = K // (a + 1)
        rows = -(-nb // sub) * sub
        piece = s1[a:a + 1] + s2[:rows]
        if rows != nb:
            piece = jnp.where(lax.broadcasted_iota(jnp.int32, piece.shape, 0) < nb, piece, -jnp.inf)
        pieces.append(piece)
        starts.append(at)
        at += rows
    pieces.append(s1[first_single:] + s2[0:1])
    return jnp.concatenate(pieces, axis=0), starts, at


ROUTE_STEPS = 2 + 2 * PEER_TOPK


def _route_head_steps(q_ref, sk_ref, stage_ref):
    K = PEER_TOPK
    par = pl.program_id(1) % 2
    for p in range(2):
        qh = q_ref[:, p * PEER_HALF:(p + 1) * PEER_HALF].astype(BF16)
        stage_ref[par, p] = lax.dot_general(sk_ref[0, p].astype(BF16), qh, (((1,), (1,)), ((), ())),
                                            preferred_element_type=F32)
    anchor = yield
    scores = [stage_ref[par, 0], stage_ref[par, 1]]
    if anchor is not None:
        scores = [st + anchor for st in scores]
    (s1, i1), (s2, i2) = yield from _lockstep([_topk_rows_steps(st, K) for st in scores])
    anchor = yield
    cand, starts, single_start = _staircase_candidates(s1, s2)
    if anchor is not None:
        cand = cand + anchor
    top, ci = yield from _topk_rows_steps(cand, K)
    a_id = jnp.zeros_like(ci)
    group_start = jnp.zeros_like(ci)
    for a in range(1, len(starts)):
        a_id = jnp.where(ci >= starts[a], a, a_id)
        group_start = jnp.where(ci >= starts[a], starts[a], group_start)
    single = ci >= single_start
    a_id = jnp.where(single, len(starts) + ci - single_start, a_id)
    b_id = jnp.where(single, 0, ci - group_start)
    e1 = jnp.zeros_like(ci)
    e2 = jnp.zeros_like(ci)
    for a in range(K):
        e1 = jnp.where(a_id == a, i1[a:a + 1], e1)
        e2 = jnp.where(b_id == a, i2[a:a + 1], e2)
    e = jnp.exp(top - top[0:1])
    return e1, e2, e / jnp.sum(e, axis=0, keepdims=True)


def _route_kernel(q_ref, sk_ref, e1_ref, e2_ref, g_ref, stage_ref):
    e1_ref[0], e2_ref[0], g_ref[0] = _run(_route_head_steps(q_ref, sk_ref, stage_ref))


EXPERT_TILE = 256


def _peer_route(q, sub_keys):
    n = q.shape[0]
    tb = EXPERT_TILE
    spec = pl.BlockSpec((1, PEER_TOPK, tb), lambda i, h: (h, 0, i))
    nbytes = 2 * (tb * PEER_KEY_DIM * 4 + 2 * PEER_NKEYS * PEER_HALF * 4) + 16 * 2 * PEER_NKEYS * tb * 4
    return pl.pallas_call(
        _route_kernel,
        grid=(n // tb, PEER_HEADS),
        in_specs=[pl.BlockSpec((tb, PEER_KEY_DIM), lambda i, h: (i, h)),
                  pl.BlockSpec((1, 2, PEER_NKEYS, PEER_HALF), lambda i, h: (h, 0, 0, 0))],
        out_specs=[spec, spec, spec],
        out_shape=[jax.ShapeDtypeStruct((PEER_HEADS, PEER_TOPK, n), jnp.int32),
                   jax.ShapeDtypeStruct((PEER_HEADS, PEER_TOPK, n), jnp.int32),
                   jax.ShapeDtypeStruct((PEER_HEADS, PEER_TOPK, n), F32)],
        scratch_shapes=[pltpu.VMEM((2, 2, PEER_NKEYS, tb), F32)],
        compiler_params=_params(("parallel", "parallel"), nbytes),
        name="peer_route",
    )(q, sub_keys)


EXPERT_CHUNK = PEER_EXPERTS // PEER_HEADS
EXPERT_SUB = 256
WEIGHT_BLOCK = 512
GATE_ROW_PAD = V7X_SUBLANES
GATE_UNROLL = 64


def _expert_kernel(h_ref, q_ref, sk_ref, e1_ref, e2_ref, g_ref, ut_ref, v0_ref, v1_ref, x_ref, gt_ref, o_ref,
                   gate_ref, acc_ref, e1t_ref, e2t_ref, gtt_ref, re1_ref, re2_ref, rg_ref, stage_ref, w_ref):
    i = pl.program_id(0)
    c = pl.program_id(1)
    tb = h_ref.shape[0]
    NK = PEER_NKEYS
    stride = tb + GATE_ROW_PAD
    slot = i % 2

    @pl.when((i == 0) & (c == 0))
    def _():
        re1_ref[0] = e1_ref[...].reshape(PEER_PAIRS, tb)
        re2_ref[0] = e2_ref[...].reshape(PEER_PAIRS, tb)
        rg_ref[0] = g_ref[...].reshape(PEER_PAIRS, tb)

    @pl.when(c == 0)
    def _():
        acc_ref[...] = jnp.zeros_like(acc_ref)
        e1t_ref[...] = jnp.transpose(re1_ref[slot])
        e2t_ref[...] = jnp.transpose(re2_ref[slot])
        gtt_ref[...] = jnp.transpose(rg_ref[slot])
        kid = lax.broadcasted_iota(jnp.int32, (NK, PEER_PAIRS), 0)

        def per_token(n, carry):
            i1 = e1t_ref[pl.ds(n, 1), :]
            i2 = e2t_ref[pl.ds(n, 1), :]
            gg = gtt_ref[pl.ds(n, 1), :]
            a_t = jnp.where(kid == i1, 1.0, 0.0).astype(BF16)
            b_t = jnp.where(kid == i2, 0.5 * gg, 0.0).astype(BF16)
            gn = lax.dot_general(a_t, b_t, (((1,), (1,)), ((), ())), preferred_element_type=F32)
            gate_ref[pl.ds(n, NK, stride=stride), :] = gn
            return carry

        lax.fori_loop(0, tb, per_token, 0, unroll=GATE_UNROLL)

    route = _route_head_steps(q_ref, sk_ref, stage_ref)
    routed = []
    n_sub = EXPERT_CHUNK // EXPERT_SUB

    def advance_route(rounds, anchor=None):
        for _ in range(rounds):
            if routed:
                return
            try:
                route.send(anchor)
            except StopIteration as stop:
                routed.append(stop.value)
            anchor = None

    h = h_ref[...]
    keys_per_chunk = EXPERT_CHUNK // NK
    advance_route(1)
    WB = WEIGHT_BLOCK
    w_parts = []
    for j in range(n_sub):
        lo = j * EXPERT_SUB
        blk, off = divmod(lo, WB)
        s = jnp.dot(h, ut_ref[0, blk, :, off:off + EXPERT_SUB], preferred_element_type=F32)
        gsel = jnp.concatenate(
            [gate_ref[pl.ds(pl.multiple_of((c * keys_per_chunk + lo // NK + t) * stride, V7X_SUBLANES), tb), :]
             for t in range(EXPERT_SUB // NK)], axis=1)
        wb = _gelu_times_half_gate(s, gsel).astype(BF16)
        w_parts.append(wb)
        w_ref[blk, :, off:off + EXPERT_SUB] = wb
    anchor = _zero_from(w_parts)
    for n, vn_ref in enumerate((v0_ref, v1_ref)):
        part = None
        for k in range(EXPERT_CHUNK // WB):
            d = jnp.dot(w_ref[k], vn_ref[0, k * WB:(k + 1) * WB, :], preferred_element_type=F32)
            part = d if part is None else part + d
        acc_ref[:, n * WB:(n + 1) * WB] += part
    advance_route(ROUTE_STEPS + 1, anchor)
    n1, n2, ng = routed[0]
    rows = pl.ds(pl.multiple_of(c * PEER_TOPK, PEER_TOPK), PEER_TOPK)
    re1_ref[1 - slot, rows, :] = n1
    re2_ref[1 - slot, rows, :] = n2
    rg_ref[1 - slot, rows, :] = ng

    @pl.when(c == pl.num_programs(1) - 1)
    def _():
        o_ref[...] = x_ref[...] + gt_ref[0] * acc_ref[...]


def _peer_u_blocks(u):
    nl, e, d = u.shape
    return u.reshape(nl, e // WEIGHT_BLOCK, WEIGHT_BLOCK, d).transpose(0, 1, 3, 2).astype(BF16)


def _peer_experts(h_bf16, q, sub_keys, u_blocks, v_bf16, layer, x, gate, seq_len):
    n, d = x.shape
    tb = EXPERT_TILE
    ec = EXPERT_CHUNK
    wb = WEIGHT_BLOCK
    assert d == 2 * wb
    nt = n // tb
    e1, e2, g = _peer_route(q[:tb], sub_keys)
    gt_arr, gt_spec0 = _row_operand(gate, seq_len, tb)
    gt_spec = pl.BlockSpec(gt_spec0.block_shape, lambda i, c: gt_spec0.index_map(i))
    rspec = pl.BlockSpec((PEER_HEADS, PEER_TOPK, tb), lambda i, c: (0, 0, 0))
    nbytes = (2 * (tb * d * 2 + 3 * PEER_PAIRS * tb * 4 + 2 * d * ec * 2 + 3 * tb * d * 4 + tb * PEER_KEY_DIM * 4)
              + (tb + GATE_ROW_PAD) * PEER_NKEYS * PEER_NKEYS * 4 + tb * d * 4 + 9 * tb * PEER_PAIRS * 4
              + 8 * tb * EXPERT_SUB * 4 + 8 * PEER_NKEYS * tb * 4)
    return pl.pallas_call(
        _expert_kernel,
        grid=(nt, PEER_HEADS),
        in_specs=[pl.BlockSpec((tb, d), lambda i, c: (i, 0)),
                  pl.BlockSpec((tb, PEER_KEY_DIM), lambda i, c: (jnp.minimum(i + 1, nt - 1), c)),
                  pl.BlockSpec((1, 2, PEER_NKEYS, PEER_HALF), lambda i, c: (c, 0, 0, 0)),
                  rspec, rspec, rspec,
                  pl.BlockSpec((1, ec // wb, d, wb), lambda i, c: (layer, c, 0, 0)),
                  pl.BlockSpec((1, ec, wb), lambda i, c: (layer, c, 0)),
                  pl.BlockSpec((1, ec, wb), lambda i, c: (layer, c, 1)),
                  pl.BlockSpec((tb, d), lambda i, c: (i, 0)),
                  gt_spec],
        out_specs=pl.BlockSpec((tb, d), lambda i, c: (i, 0)),
        out_shape=jax.ShapeDtypeStruct((n, d), F32),
        scratch_shapes=[pltpu.VMEM(((tb + GATE_ROW_PAD) * PEER_NKEYS, PEER_NKEYS), F32),
                        pltpu.VMEM((tb, d), F32),
                        pltpu.VMEM((tb, PEER_PAIRS), jnp.int32),
                        pltpu.VMEM((tb, PEER_PAIRS), jnp.int32),
                        pltpu.VMEM((tb, PEER_PAIRS), F32),
                        pltpu.VMEM((2, PEER_PAIRS, tb), jnp.int32),
                        pltpu.VMEM((2, PEER_PAIRS, tb), jnp.int32),
                        pltpu.VMEM((2, PEER_PAIRS, tb), F32),
                        pltpu.VMEM((2, 2, PEER_NKEYS, tb), F32),
                        pltpu.VMEM((ec // wb, tb, wb), BF16)],
        compiler_params=_params(("arbitrary", "arbitrary"), nbytes),
        name="peer_experts",
    )(h_bf16, q, sub_keys, e1, e2, g, u_blocks, v_bf16, v_bf16, x, gt_arr)


def _swa_permute_in(w_in):
    d = w_in.shape[0]
    wq = w_in[:, :SWA_Q].reshape(d, SWA_KV_HEADS, SWA_GROUP, SWA_HEAD_DIM).transpose(0, 2, 1, 3).reshape(d, SWA_Q)
    return jnp.concatenate([wq, w_in[:, SWA_Q:]], axis=1)


def _swa_permute_out(w_out):
    d = w_out.shape[1]
    return w_out.reshape(SWA_KV_HEADS, SWA_GROUP, SWA_HEAD_DIM, d).transpose(1, 0, 2, 3).reshape(SWA_Q, d)


def _prepare_weights(p):
    w = {}
    w['gla_in'] = [jnp.pad(p['w_gla_in'][j], ((0, 0), (0, GLA_IN_PAD - p['w_gla_in'].shape[2]))).astype(BF16)
                   for j in range(p['w_gla_in'].shape[0])]
    w['gla_a2'] = [jnp.pad(p['w_gla_a2'][j], ((0, V7X_LANES - GLA_GATE_RANK), (0, 0)))
                   for j in range(p['w_gla_a2'].shape[0])]
    w['gla_out'] = [m.astype(BF16) for m in p['w_gla_out']]
    w['swa_in'] = [_swa_permute_in(m).astype(BF16) for m in p['w_swa_in']]
    w['swa_out'] = [_swa_permute_out(m).astype(BF16) for m in p['w_swa_out']]
    w['lru_in'] = [m.astype(BF16) for m in p['w_lru_in']]
    w['lru_out'] = [m.astype(BF16) for m in p['w_lru_out']]
    w['peer_q'] = [m.astype(BF16) for m in p['w_peer_q']]
    w['peer_ut'] = _peer_u_blocks(p['peer_u'])
    w['peer_v'] = p['peer_v'].astype(BF16)
    return w


def _trunk(x3, mod, states, p, w):
    batch, seq_len, d = x3.shape
    x = x3.reshape(batch * seq_len, d)
    new_gla, new_k, new_v, new_conv, new_h = [], [], [], [], []
    gla_stack = None
    for i in range(DEPTH):
        kind, j = i % N_MIXERS, i // N_MIXERS
        sh_m, sc_m, gt_m, sh_f, sc_f, gt_f = [mod[i][:, k * d:(k + 1) * d] for k in range(6)]
        if kind == 0:
            proj = _norm_proj(x, p['g_ln_mix'][i], sc_m, sh_m, w['gla_in'][j], seq_len)
            if states is None:
                mix, s_new = _gla_prompt(proj, w['gla_a2'][j], p['b_gla_a'][j], p['g_gla_norm'][j], batch, seq_len)
                new_gla.append(s_new)
            else:
                mix, gla_stack = _gla_sample(proj, states[0], j, gla_stack, w['gla_a2'][j], p['b_gla_a'][j],
                                             p['g_gla_norm'][j], batch, seq_len)
            w_out = w['gla_out'][j]
        elif kind == 1:
            proj = _norm_proj(x, p['g_ln_mix'][i], sc_m, sh_m, w['swa_in'][j], seq_len)
            if states is None:
                mix, k_n, v_n = _swa_prompt(proj, p['g_swa_q'][j], p['g_swa_k'][j], p['swa_sinks'][j], batch, seq_len)
            else:
                kc = states[1][j].reshape(batch, WINDOW, SWA_KV)
                vc = states[2][j].reshape(batch, WINDOW, SWA_KV)
                mix, k_n, v_n = _swa_sample(proj, kc, vc, p['g_swa_q'][j], p['g_swa_k'][j], p['swa_sinks'][j],
                                            batch, seq_len)
            new_k.append(k_n.reshape(batch, WINDOW, SWA_KV_HEADS, SWA_HEAD_DIM))
            new_v.append(v_n.reshape(batch, WINDOW, SWA_KV_HEADS, SWA_HEAD_DIM))
            w_out = w['swa_out'][j]
        else:
            proj = _norm_proj(x, p['g_ln_mix'][i], sc_m, sh_m, w['lru_in'][j], seq_len)
            lru_args = (p['lru_conv_w'][j], p['lru_conv_b'][j], p['w_lru_ga'][j], p['b_lru_ga'][j],
                        p['w_lru_gx'][j], p['b_lru_gx'][j], p['lru_lam'][j], batch, seq_len)
            assert seq_len >= CONV_WIDTH - 1
            if states is None:
                mix, h_n = _lru_prompt(proj, *lru_args)
            else:
                mix, hs = _lru_sample(proj, states[3][j], states[4][j], *lru_args)
                h_n = hs.reshape(batch, seq_len, D_RNN)[:, -1]
            new_conv.append(proj[:, D_RNN:].reshape(batch, seq_len, D_RNN)[:, seq_len - (CONV_WIDTH - 1):])
            new_h.append(h_n)
            w_out = w['lru_out'][j]
        x = _proj_residual(mix, w_out, x, gt_m, seq_len)
        q, hb = _norm_proj(x, p['g_ln_ffn'][i], sc_f, sh_f, w['peer_q'][i], seq_len, with_h=True)
        x = _peer_experts(hb, q, p['peer_sub_keys'][i], w['peer_ut'], w['peer_v'], i, x, gt_f, seq_len)
    y = x.reshape(batch, seq_len, d)
    gla_out = jnp.stack(new_gla) if states is None else gla_stack
    return y, (gla_out, jnp.stack(new_k), jnp.stack(new_v), jnp.stack(new_conv), jnp.stack(new_h))


def kernel(x_prompt, x_sample, state_gla, cache_swa_k, cache_swa_v, state_lru_conv, state_lru_h,
           c_prompt, c_sample, g_ln_mix, g_ln_ffn, w_mod, b_mod,
           w_gla_in, w_gla_a2, b_gla_a, g_gla_norm, w_gla_out,
           w_swa_in, g_swa_q, g_swa_k, swa_sinks, w_swa_out,
           w_lru_in, lru_conv_w, lru_conv_b, w_lru_ga, b_lru_ga, w_lru_gx, b_lru_gx, lru_lam, w_lru_out,
           w_peer_q, peer_sub_keys, peer_u, peer_v):
    p = {'g_ln_mix': g_ln_mix, 'g_ln_ffn': g_ln_ffn,
         'w_gla_in': w_gla_in, 'w_gla_a2': w_gla_a2, 'b_gla_a': b_gla_a, 'g_gla_norm': g_gla_norm,
         'w_gla_out': w_gla_out,
         'w_swa_in': w_swa_in, 'g_swa_q': g_swa_q, 'g_swa_k': g_swa_k, 'swa_sinks': swa_sinks,
         'w_swa_out': w_swa_out,
         'w_lru_in': w_lru_in, 'lru_conv_w': lru_conv_w, 'lru_conv_b': lru_conv_b,
         'w_lru_ga': w_lru_ga, 'b_lru_ga': b_lru_ga, 'w_lru_gx': w_lru_gx, 'b_lru_gx': b_lru_gx,
         'lru_lam': lru_lam, 'w_lru_out': w_lru_out,
         'w_peer_q': w_peer_q, 'peer_sub_keys': peer_sub_keys, 'peer_u': peer_u, 'peer_v': peer_v}
    w = _prepare_weights(p)
    nb_p, nb_s = c_prompt.shape[0], c_sample.shape[0]
    rows = -(-(nb_p + nb_s) // V7X_SUBLANES) * V7X_SUBLANES
    c_all = jnp.pad(jnp.concatenate([c_prompt, c_sample], axis=0), ((0, rows - nb_p - nb_s), (0, 0)))
    mod = _modulation(c_all, w_mod, b_mod)
    y_p, (gla_p, k_p, v_p, conv_p, h_p) = _trunk(x_prompt, mod[:, :nb_p], None, p, w)
    y_s, (gla_s, k_s, v_s, conv_s, h_s) = _trunk(
        x_sample, mod[:, nb_p:nb_p + nb_s],
        (state_gla, cache_swa_k, cache_swa_v, state_lru_conv, state_lru_h), p, w)
    return (y_p, y_s, gla_p, gla_s, k_p, k_s, v_p, v_s, conv_p, conv_s, h_p, h_s)
```

```python
import functools
import math

import jax
import jax.numpy as jnp
from jax import lax
from jax.experimental import pallas as pl
from jax.experimental.pallas import tpu as pltpu

F32 = jnp.float32
BF16 = jnp.bfloat16

D_MODEL = 1024
DEPTH = 4
N_MIXERS = 3
RMS_EPS = 1e-6

GLA_HEADS = 4
GLA_QK = D_MODEL // 2
GLA_V = D_MODEL
GLA_DK = GLA_QK // GLA_HEADS
GLA_DV = GLA_V // GLA_HEADS
GLA_GATE_RANK = 16
GLA_TAU = 16.0
GLA_SUB = 16
GLA_CHUNK = 128
GLA_IN_PAD = 2 * GLA_QK + 2 * GLA_V + 128

SWA_HEAD_DIM = 64
SWA_Q_HEADS = D_MODEL // SWA_HEAD_DIM
SWA_KV_HEADS = 4
SWA_GROUP = SWA_Q_HEADS // SWA_KV_HEADS
SWA_Q = SWA_Q_HEADS * SWA_HEAD_DIM
SWA_KV = SWA_KV_HEADS * SWA_HEAD_DIM
WINDOW = 128

D_RNN = D_MODEL
LRU_BLOCKS = 4
LRU_BLOCK = D_RNN // LRU_BLOCKS
CONV_WIDTH = 4
LRU_C = 8.0

PEER_HEADS = 8
PEER_NKEYS = 128
PEER_EXPERTS = PEER_NKEYS * PEER_NKEYS
PEER_KEY_DIM = 256
PEER_HALF = PEER_KEY_DIM // 2
PEER_TOPK = 16
PEER_PAIRS = PEER_HEADS * PEER_TOPK

V7X_LANES = 128
V7X_SUBLANES = 8
V7X_VMEM_BYTES = 64 * 1024 * 1024

TOKEN_TILE = 512
NEG_BIG = -1e30


def _vmem_limit(nbytes):
    return int(min(max(nbytes * 3 // 2, 16 * 1024 * 1024), V7X_VMEM_BYTES - 8 * 1024 * 1024))


def _params(semantics, nbytes):
    return pltpu.CompilerParams(dimension_semantics=semantics, vmem_limit_bytes=_vmem_limit(nbytes))


def _rms(x, g):
    return x * lax.rsqrt(jnp.mean(x * x, axis=-1, keepdims=True) + RMS_EPS) * g


def _gelu_tanh(x):
    return 0.5 * x * (1.0 + jnp.tanh(math.sqrt(2.0 / math.pi) * (x + 0.044715 * (x * x * x))))


def _gelu_times_half_gate(x, half_gate):
    c1 = math.sqrt(2.0 / math.pi)
    inner = x * (c1 + (c1 * 0.044715) * (x * x))
    return (x * half_gate) * (1.0 + jnp.tanh(inner))


def _sigmoid(x):
    return 1.0 / (1.0 + jnp.exp(-x))


def _softplus(x):
    return jnp.maximum(x, 0.0) + jnp.log1p(jnp.exp(-jnp.abs(x)))


def _row_operand(vec, seq_len, tile):
    b, d = vec.shape
    if seq_len % tile == 0:
        per_seq = seq_len // tile
        return vec.reshape(b, 1, d), pl.BlockSpec((1, 1, d), lambda i: (i // per_seq, 0, 0))
    assert tile % seq_len == 0
    rep = jnp.repeat(vec, seq_len, axis=0).reshape(b * seq_len // tile, tile, d)
    return rep, pl.BlockSpec((1, tile, d), lambda i: (i, 0, 0))


def _mod_kernel(c_ref, w_ref, b_ref, o_ref):
    c = c_ref[...]
    sc = c * _sigmoid(c)
    o_ref[0] = jnp.dot(sc, w_ref[0], preferred_element_type=F32) + b_ref[0]


def _modulation(c, w_mod, b_mod):
    bp, d = c.shape
    tn = 1024
    nt = 6 * d // tn
    return pl.pallas_call(
        _mod_kernel,
        grid=(DEPTH, nt),
        in_specs=[pl.BlockSpec((bp, d), lambda l, j: (0, 0)),
                  pl.BlockSpec((1, d, tn), lambda l, j: (l, 0, j)),
                  pl.BlockSpec((1, 1, tn), lambda l, j: (l, 0, j))],
        out_specs=pl.BlockSpec((1, bp, tn), lambda l, j: (l, 0, j)),
        out_shape=jax.ShapeDtypeStruct((DEPTH, bp, 6 * d), F32),
        compiler_params=_params(("parallel", "parallel"), 2 * (d * tn * 4 + 2 * bp * tn * 4)),
        name="modulation",
    )(c, w_mod, b_mod.reshape(DEPTH, 1, 6 * d))


def _norm_proj_kernel(x_ref, g_ref, sc_ref, sh_ref, w_ref, o_ref, *h_ref):
    h = _rms(x_ref[...], g_ref[...]) * (1.0 + sc_ref[0]) + sh_ref[0]
    hb = h.astype(BF16)
    o_ref[...] = jnp.dot(hb, w_ref[...], preferred_element_type=F32).astype(o_ref.dtype)
    if h_ref:
        h_ref[0][...] = hb


def _norm_proj(x, g, scale, shift, w_bf16, seq_len, with_h=False):
    n, d = x.shape
    nout = w_bf16.shape[1]
    tm = TOKEN_TILE
    sc_arr, sc_spec = _row_operand(scale, seq_len, tm)
    sh_arr, sh_spec = _row_operand(shift, seq_len, tm)
    out_shape = [jax.ShapeDtypeStruct((n, nout), BF16 if with_h else F32)]
    out_specs = [pl.BlockSpec((tm, nout), lambda i: (i, 0))]
    if with_h:
        out_shape.append(jax.ShapeDtypeStruct((n, d), BF16))
        out_specs.append(pl.BlockSpec((tm, d), lambda i: (i, 0)))
    nbytes = 2 * (tm * d * 4 + d * nout * 2 + tm * nout * 4 + 3 * tm * d * 4)
    res = pl.pallas_call(
        _norm_proj_kernel,
        grid=(n // tm,),
        in_specs=[pl.BlockSpec((tm, d), lambda i: (i, 0)),
                  pl.BlockSpec((1, d), lambda i: (0, 0)),
                  sc_spec, sh_spec,
                  pl.BlockSpec((d, nout), lambda i: (0, 0))],
        out_specs=out_specs,
        out_shape=out_shape,
        compiler_params=_params(("parallel",), nbytes),
        name="norm_proj",
    )(x, g.reshape(1, d), sc_arr, sh_arr, w_bf16)
    return res if with_h else res[0]


def _proj_residual_kernel(a_ref, w_ref, x_ref, gt_ref, o_ref):
    y = jnp.dot(a_ref[...].astype(BF16), w_ref[...], preferred_element_type=F32)
    o_ref[...] = x_ref[...] + gt_ref[0] * y


def _proj_residual(a, w_bf16, x, gate, seq_len):
    n, k = a.shape
    d = x.shape[1]
    tm = TOKEN_TILE
    gt_arr, gt_spec = _row_operand(gate, seq_len, tm)
    nbytes = 2 * (tm * k * 4 + k * d * 2 + 3 * tm * d * 4)
    return pl.pallas_call(
        _proj_residual_kernel,
        grid=(n // tm,),
        in_specs=[pl.BlockSpec((tm, k), lambda i: (i, 0)),
                  pl.BlockSpec((k, d), lambda i: (0, 0)),
                  pl.BlockSpec((tm, d), lambda i: (i, 0)),
                  gt_spec],
        out_specs=pl.BlockSpec((tm, d), lambda i: (i, 0)),
        out_shape=jax.ShapeDtypeStruct((n, d), F32),
        compiler_params=_params(("parallel",), nbytes),
        name="proj_residual",
    )(a, w_bf16, x, gt_arr)


def _log_decay(lr, wa2, ba):
    z = jnp.dot(lr, wa2, preferred_element_type=F32) + ba
    return (jnp.minimum(z, 0.0) - jnp.log1p(jnp.exp(-jnp.abs(z)))) * (1.0 / GLA_TAU)


def _col_bcast(row):
    return jnp.transpose(jnp.broadcast_to(row, (V7X_LANES, V7X_LANES)))


def _head_out(o, gate, gn):
    return _rms(o, gn) * (gate * _sigmoid(gate))


def _gla_prompt_kernel(p_ref, wa2_ref, ba_ref, gn_ref, o_ref, s_ref):
    c = pl.program_id(0)
    C = GLA_CHUNK
    nsub = C // GLA_SUB

    @pl.when(c == 0)
    def _():
        s_ref[...] = jnp.zeros_like(s_ref)

    row = lax.broadcasted_iota(jnp.int32, (C, C), 0)
    col = lax.broadcasted_iota(jnp.int32, (C, C), 1)
    tri = (col <= row).astype(F32)
    later = ((col > row) & (col // GLA_SUB == row // GLA_SUB)).astype(F32)
    sums = jnp.concatenate([tri, later], axis=0)
    causal = col <= row
    rsub = lax.broadcasted_iota(jnp.int32, (C, GLA_DK), 0) // GLA_SUB

    for seq in range(p_ref.shape[0]):
        _gla_prompt_chunk(p_ref.at[seq], wa2_ref, ba_ref, gn_ref, o_ref.at[seq], s_ref.at[seq],
                          sums, causal, rsub)


def _gla_prompt_chunk(p_ref, wa2_ref, ba_ref, gn_ref, o_ref, s_ref, sums, causal, rsub):
    C = GLA_CHUNK
    nsub = C // GLA_SUB
    lr = p_ref[:, 2 * GLA_QK + 2 * GLA_V:]
    la_all = _log_decay(lr, wa2_ref[...], ba_ref[...])
    cs_all = jnp.dot(sums, la_all, preferred_element_type=F32, precision=lax.Precision.HIGHEST)
    for h in range(GLA_HEADS):
        q = p_ref[:, h * GLA_DK:(h + 1) * GLA_DK] * (GLA_DK ** -0.5)
        k = p_ref[:, GLA_QK + h * GLA_DK:GLA_QK + (h + 1) * GLA_DK]
        v = p_ref[:, 2 * GLA_QK + h * GLA_DV:2 * GLA_QK + (h + 1) * GLA_DV]
        gate = p_ref[:, 2 * GLA_QK + GLA_V + h * GLA_DV:2 * GLA_QK + GLA_V + (h + 1) * GLA_DV]
        cs = cs_all[:, h * GLA_DK:(h + 1) * GLA_DK]
        b = cs[:C]
        to_sub_end = cs[C:]
        b_last = b[C - 1:C]
        k_sub = k * jnp.exp(to_sub_end)
        q_parts, k_parts = [], []
        for m in range(nsub):
            ref_row = b[m * GLA_SUB + GLA_SUB - 1:m * GLA_SUB + GLA_SUB]
            e = jnp.where(rsub >= m, b - ref_row, NEG_BIG)
            q_parts.append((q * jnp.exp(e)).astype(BF16))
            k_parts.append(jnp.where(rsub == m, k_sub, 0.0).astype(BF16))
        qcat = jnp.concatenate(q_parts, axis=1)
        kcat = jnp.concatenate(k_parts, axis=1)
        att = lax.dot_general(qcat, kcat, (((1,), (1,)), ((), ())), preferred_element_type=F32)
        att = jnp.where(causal, att, 0.0)
        s_old = s_ref[h]
        o = jnp.dot(att.astype(BF16), v.astype(BF16), preferred_element_type=F32)
        o = o + jnp.dot((q * jnp.exp(b)).astype(BF16), s_old.astype(BF16), preferred_element_type=F32)
        k_end = (k * jnp.exp(b_last - b)).astype(BF16)
        upd = lax.dot_general(k_end, v.astype(BF16), (((0,), (0,)), ((), ())), preferred_element_type=F32)
        decay = _col_bcast(jnp.exp(b_last))
        s_ref[h] = jnp.concatenate([decay] * (GLA_DV // V7X_LANES), axis=1) * s_old + upd
        o_ref[:, h * GLA_DV:(h + 1) * GLA_DV] = _head_out(o, gate, gn_ref[...])


def _gla_prompt(proj, wa2_pad, ba, gn, batch, seq_len):
    C = GLA_CHUNK
    p3 = proj.reshape(batch, seq_len, GLA_IN_PAD)
    nbytes = batch * (2 * (C * GLA_IN_PAD * 4 + C * GLA_V * 4 + GLA_HEADS * GLA_DK * GLA_DV * 4) + 64 * C * C * 4)
    o, s = pl.pallas_call(
        _gla_prompt_kernel,
        grid=(seq_len // C,),
        in_specs=[pl.BlockSpec((batch, C, GLA_IN_PAD), lambda c: (0, c, 0)),
                  pl.BlockSpec((V7X_LANES, GLA_QK), lambda c: (0, 0)),
                  pl.BlockSpec((1, GLA_QK), lambda c: (0, 0)),
                  pl.BlockSpec((1, GLA_DV), lambda c: (0, 0))],
        out_specs=[pl.BlockSpec((batch, C, GLA_V), lambda c: (0, c, 0)),
                   pl.BlockSpec((batch, GLA_HEADS, GLA_DK, GLA_DV), lambda c: (0, 0, 0, 0))],
        out_shape=[jax.ShapeDtypeStruct((batch, seq_len, GLA_V), F32),
                   jax.ShapeDtypeStruct((batch, GLA_HEADS, GLA_DK, GLA_DV), F32)],
        compiler_params=_params(("arbitrary",), nbytes),
        name="gla_prompt",
    )(p3, wa2_pad, ba.reshape(1, GLA_QK), gn.reshape(1, GLA_DV))
    return o.reshape(batch * seq_len, GLA_V), s


GLA_SAMPLE_BATCH = 8


def _gla_sample_kernel(p_ref, s0_ref, wa2_ref, ba_ref, gn_ref, *refs):
    o_ref, s_ref = refs[-2:]
    nseq = s0_ref.shape[1]
    rows = p_ref.shape[0]
    T = rows // nseq
    row = lax.broadcasted_iota(jnp.int32, (rows, rows), 0)
    col = lax.broadcasted_iota(jnp.int32, (rows, rows), 1)
    same_seq = (row // T) == (col // T)
    causal = same_seq & (col <= row)
    sums = jnp.concatenate([causal.astype(F32), same_seq.astype(F32)], axis=0)
    lr = p_ref[:, 2 * GLA_QK + 2 * GLA_V:]
    la = _log_decay(lr, wa2_ref[...], ba_ref[...])
    cs = jnp.dot(sums, la, preferred_element_type=F32, precision=lax.Precision.HIGHEST)
    b_all, b_last_all = cs[:rows], cs[rows:]
    for h in range(GLA_HEADS):
        hk = slice(h * GLA_DK, (h + 1) * GLA_DK)
        q = p_ref[:, h * GLA_DK:(h + 1) * GLA_DK] * (GLA_DK ** -0.5)
        k = p_ref[:, GLA_QK + h * GLA_DK:GLA_QK + (h + 1) * GLA_DK]
        v = p_ref[:, 2 * GLA_QK + h * GLA_DV:2 * GLA_QK + (h + 1) * GLA_DV]
        gate = p_ref[:, 2 * GLA_QK + GLA_V + h * GLA_DV:2 * GLA_QK + GLA_V + (h + 1) * GLA_DV]
        b, b_last = b_all[:, hk], b_last_all[:, hk]
        k_end = k * jnp.exp(b_last - b)
        q_rel = (q * jnp.exp(b - b_last)).astype(BF16)
        q_dec = q * jnp.exp(b)
        att = lax.dot_general(q_rel, k_end.astype(BF16), (((1,), (1,)), ((), ())), preferred_element_type=F32)
        att = jnp.where(causal, att, 0.0)
        o_intra = jnp.dot(att.astype(BF16), v.astype(BF16), preferred_element_type=F32)
        o_inter = []
        for j in range(nseq):
            rj = slice(j * T, (j + 1) * T)
            s_old = s0_ref[0, j, h]
            o_inter.append(jnp.dot(q_dec[rj].astype(BF16), s_old.astype(BF16), preferred_element_type=F32))
            upd = lax.dot_general(k_end[rj].astype(BF16), v[rj].astype(BF16), (((0,), (0,)), ((), ())),
                                  preferred_element_type=F32)
            decay = _col_bcast(jnp.exp(b_last[j * T:j * T + 1]))
            s_ref[0, j, h] = jnp.concatenate([decay] * (GLA_DV // V7X_LANES), axis=1) * s_old + upd
        o = o_intra + jnp.concatenate(o_inter, axis=0)
        o_ref[:, h * GLA_DV:(h + 1) * GLA_DV] = _head_out(o, gate, gn_ref[...])


def _gla_sample(proj, states, layer, new_states, wa2_pad, ba, gn, batch, seq_len):
    assert seq_len <= GLA_SUB
    nb = GLA_SAMPLE_BATCH
    rows = nb * seq_len
    state_spec = pl.BlockSpec((1, nb, GLA_HEADS, GLA_DK, GLA_DV), lambda b: (layer, b, 0, 0, 0))
    state_block = nb * GLA_HEADS * GLA_DK * GLA_DV * 4
    nbytes = 2 * (rows * (GLA_IN_PAD + GLA_V) * 4 + 2 * state_block) + 16 * rows * GLA_IN_PAD * 4
    in_specs = [pl.BlockSpec((rows, GLA_IN_PAD), lambda b: (b, 0)),
                state_spec,
                pl.BlockSpec((V7X_LANES, GLA_QK), lambda b: (0, 0)),
                pl.BlockSpec((1, GLA_QK), lambda b: (0, 0)),
                pl.BlockSpec((1, GLA_DV), lambda b: (0, 0))]
    operands = [proj, states, wa2_pad, ba.reshape(1, GLA_QK), gn.reshape(1, GLA_DV)]
    aliases = {}
    if new_states is not None:
        in_specs.append(pl.BlockSpec(memory_space=pl.ANY))
        operands.append(new_states)
        aliases = {len(operands) - 1: 1}
    o, s = pl.pallas_call(
        _gla_sample_kernel,
        grid=(batch // nb,),
        in_specs=in_specs,
        out_specs=[pl.BlockSpec((rows, GLA_V), lambda b: (b, 0)), state_spec],
        out_shape=[jax.ShapeDtypeStruct((batch * seq_len, GLA_V), F32),
                   jax.ShapeDtypeStruct(states.shape, F32)],
        input_output_aliases=aliases,
        compiler_params=_params(("parallel",), nbytes),
        name="gla_sample",
    )(*operands)
    return o, s


def _head_group_norm(x, gain, gsum):
    sq = x * x
    hi = sq.astype(BF16)
    lo = (sq - hi.astype(F32)).astype(BF16)
    ms = (jnp.dot(hi, gsum, preferred_element_type=F32) + jnp.dot(lo, gsum, preferred_element_type=F32))
    return x * lax.rsqrt(ms * (1.0 / SWA_HEAD_DIM) + RMS_EPS) * gain


def _swa_attend(q_groups, k_all, v_all, mask, sink_ref):
    tq = q_groups[0].shape[0]
    lane_head = lax.broadcasted_iota(jnp.int32, (tq, SWA_KV), 1) // SWA_HEAD_DIM
    mask_rows = jnp.concatenate([mask] * SWA_GROUP, axis=0)
    out = [jnp.zeros((tq, SWA_KV), F32) for _ in range(SWA_GROUP)]
    for kv in range(SWA_KV_HEADS):
        in_head = lane_head == kv
        qs = jnp.concatenate([jnp.where(in_head, qg, 0.0) for qg in q_groups], axis=0).astype(BF16)
        s = lax.dot_general(qs, k_all, (((1,), (1,)), ((), ())), preferred_element_type=F32)
        s = jnp.where(mask_rows, s, -jnp.inf)
        sink = jnp.concatenate(
            [jnp.full((tq, 1), sink_ref[kv * SWA_GROUP + g], F32) for g in range(SWA_GROUP)], axis=0)
        m = jnp.maximum(jnp.max(s, axis=1, keepdims=True), sink)
        p = jnp.exp(s - m)
        denom = jnp.sum(p, axis=1, keepdims=True) + jnp.exp(sink - m)
        pv = jnp.dot(p.astype(BF16), v_all, preferred_element_type=F32) / denom
        for g in range(SWA_GROUP):
            out[g] = jnp.where(in_head, pv[g * tq:(g + 1) * tq], out[g])
    return out


SWA_STEP_WINDOWS = 8


def _swa_prompt_kernel(sink_ref, cur_ref, prev_ref, gq_ref, gk_ref, gsum_ref, o_ref, k_ref, v_ref):
    n = pl.program_id(1)
    W = WINDOW
    gsum = gsum_ref[...]
    k_cur = _head_group_norm(cur_ref[0, :, SWA_Q:SWA_Q + SWA_KV], gk_ref[...], gsum)
    k_prev = _head_group_norm(prev_ref[0, :, SWA_Q:SWA_Q + SWA_KV], gk_ref[...], gsum)
    k_rows = jnp.concatenate([k_prev, k_cur], axis=0).astype(BF16)
    v_rows = jnp.concatenate([prev_ref[0, :, SWA_Q + SWA_KV:], cur_ref[0, :, SWA_Q + SWA_KV:]], axis=0).astype(BF16)
    t = lax.broadcasted_iota(jnp.int32, (W, 2 * W), 0)
    s = lax.broadcasted_iota(jnp.int32, (W, 2 * W), 1)
    band = (s >= t) & (s <= t + W)
    for w in range(SWA_STEP_WINDOWS):
        rows = slice(w * W, (w + 1) * W)
        mask = band if w > 0 else band & ((s >= W) | (n > 0))
        q_groups = [_head_group_norm(cur_ref[0, rows, g * SWA_KV:(g + 1) * SWA_KV], gq_ref[...], gsum)
                    * (SWA_HEAD_DIM ** -0.5) for g in range(SWA_GROUP)]
        out = _swa_attend(q_groups, k_rows[w * W:(w + 2) * W], v_rows[w * W:(w + 2) * W], mask, sink_ref)
        o_ref[0, rows, :] = jnp.concatenate(out, axis=1)
    last = slice((SWA_STEP_WINDOWS - 1) * W, SWA_STEP_WINDOWS * W)
    k_ref[0] = k_cur[last]
    v_ref[0] = cur_ref[0, last, SWA_Q + SWA_KV:]


def _swa_gsum():
    head = jnp.arange(SWA_KV) // SWA_HEAD_DIM
    return (head[:, None] == head[None, :]).astype(BF16)


def _swa_prompt(proj, gq, gk, sinks, batch, seq_len):
    W = WINDOW
    nw = SWA_STEP_WINDOWS
    width = SWA_Q + 2 * SWA_KV
    p3 = proj.reshape(batch, seq_len, width)
    nbytes = 2 * ((nw + 1) * W * width * 4 + nw * W * SWA_Q * 4 + 2 * W * SWA_KV * 4) + nw * 48 * W * 2 * W * 4
    o, k, v = pl.pallas_call(
        _swa_prompt_kernel,
        grid=(batch, seq_len // (nw * W)),
        in_specs=[pl.BlockSpec(memory_space=pltpu.SMEM),
                  pl.BlockSpec((1, nw * W, width), lambda b, n: (b, n, 0)),
                  pl.BlockSpec((1, W, width), lambda b, n: (b, jnp.maximum(nw * n - 1, 0), 0)),
                  pl.BlockSpec((1, SWA_KV), lambda b, n: (0, 0)),
                  pl.BlockSpec((1, SWA_KV), lambda b, n: (0, 0)),
                  pl.BlockSpec((SWA_KV, SWA_KV), lambda b, n: (0, 0))],
        out_specs=[pl.BlockSpec((1, nw * W, SWA_Q), lambda b, n: (b, n, 0)),
                   pl.BlockSpec((1, W, SWA_KV), lambda b, n: (b, 0, 0)),
                   pl.BlockSpec((1, W, SWA_KV), lambda b, n: (b, 0, 0))],
        out_shape=[jax.ShapeDtypeStruct((batch, seq_len, SWA_Q), F32),
                   jax.ShapeDtypeStruct((batch, W, SWA_KV), F32),
                   jax.ShapeDtypeStruct((batch, W, SWA_KV), F32)],
        compiler_params=_params(("parallel", "arbitrary"), nbytes),
        name="swa_prompt",
    )(sinks, p3, p3, jnp.tile(gq, SWA_KV_HEADS).reshape(1, SWA_KV), jnp.tile(gk, SWA_KV_HEADS).reshape(1, SWA_KV),
      _swa_gsum())
    return o.reshape(batch * seq_len, SWA_Q), k, v


SWA_SAMPLE_BATCH = 8


def _swa_sample_kernel(sink_ref, p_ref, kc_ref, vc_ref, gq_ref, gk_ref, gsum_ref, o_ref, k_ref, v_ref):
    T = p_ref.shape[1]
    gsum = gsum_ref[...]
    t = lax.broadcasted_iota(jnp.int32, (T, WINDOW + T), 0)
    s = lax.broadcasted_iota(jnp.int32, (T, WINDOW + T), 1)
    mask = (s >= t) & (s <= t + WINDOW)
    for j in range(p_ref.shape[0]):
        k_new = _head_group_norm(p_ref[j, :, SWA_Q:SWA_Q + SWA_KV], gk_ref[...], gsum)
        v_new = p_ref[j, :, SWA_Q + SWA_KV:]
        k_all = jnp.concatenate([kc_ref[j], k_new], axis=0)
        v_all = jnp.concatenate([vc_ref[j], v_new], axis=0)
        q_groups = [_head_group_norm(p_ref[j, :, g * SWA_KV:(g + 1) * SWA_KV], gq_ref[...], gsum)
                    * (SWA_HEAD_DIM ** -0.5) for g in range(SWA_GROUP)]
        out = _swa_attend(q_groups, k_all.astype(BF16), v_all.astype(BF16), mask, sink_ref)
        o_ref[j] = jnp.concatenate(out, axis=1)
        k_ref[j] = k_all[T:]
        v_ref[j] = v_all[T:]


def _swa_sample(proj, k_cache, v_cache, gq, gk, sinks, batch, seq_len):
    nb = SWA_SAMPLE_BATCH
    W = WINDOW
    width = SWA_Q + 2 * SWA_KV
    p3 = proj.reshape(batch, seq_len, width)
    nbytes = 2 * (nb * seq_len * (width + SWA_Q) * 4 + 4 * nb * W * SWA_KV * 4)
    o, k, v = pl.pallas_call(
        _swa_sample_kernel,
        grid=(batch // nb,),
        in_specs=[pl.BlockSpec(memory_space=pltpu.SMEM),
                  pl.BlockSpec((nb, seq_len, width), lambda b: (b, 0, 0)),
                  pl.BlockSpec((nb, W, SWA_KV), lambda b: (b, 0, 0)),
                  pl.BlockSpec((nb, W, SWA_KV), lambda b: (b, 0, 0)),
                  pl.BlockSpec((1, SWA_KV), lambda b: (0, 0)),
                  pl.BlockSpec((1, SWA_KV), lambda b: (0, 0)),
                  pl.BlockSpec((SWA_KV, SWA_KV), lambda b: (0, 0))],
        out_specs=[pl.BlockSpec((nb, seq_len, SWA_Q), lambda b: (b, 0, 0)),
                   pl.BlockSpec((nb, W, SWA_KV), lambda b: (b, 0, 0)),
                   pl.BlockSpec((nb, W, SWA_KV), lambda b: (b, 0, 0))],
        out_shape=[jax.ShapeDtypeStruct((batch, seq_len, SWA_Q), F32),
                   jax.ShapeDtypeStruct((batch, W, SWA_KV), F32),
                   jax.ShapeDtypeStruct((batch, W, SWA_KV), F32)],
        compiler_params=_params(("parallel",), nbytes),
        name="swa_sample",
    )(sinks, p3, k_cache, v_cache, jnp.tile(gq, SWA_KV_HEADS).reshape(1, SWA_KV),
      jnp.tile(gk, SWA_KV_HEADS).reshape(1, SWA_KV), _swa_gsum())
    return o.reshape(batch * seq_len, SWA_Q), k, v


def _lru_conv(x, shifted, cw_ref, cb_ref):
    y = cb_ref[...] + cw_ref[CONV_WIDTH - 1:CONV_WIDTH] * x
    for s in range(1, CONV_WIDTH):
        y = y + cw_ref[CONV_WIDTH - 1 - s:CONV_WIDTH - s] * shifted[s - 1]
    return y


def _block_diag_dot(x, w_ref):
    xb = x.astype(BF16)
    return jnp.concatenate(
        [jnp.dot(xb[:, n * LRU_BLOCK:(n + 1) * LRU_BLOCK], w_ref[n], preferred_element_type=F32)
         for n in range(LRU_BLOCKS)], axis=1)


def _lru_terms(xc, wga_ref, bga_ref, wgx_ref, bgx_ref, lam_ref):
    r = _sigmoid(_block_diag_dot(xc, wga_ref) + bga_ref[...])
    i = _sigmoid(_block_diag_dot(xc, wgx_ref) + bgx_ref[...])
    log_a = (-LRU_C) * r * _softplus(-lam_ref[...])
    a = jnp.exp(log_a)
    y2 = 2.0 * log_a
    u = a * a
    em1 = jnp.where(u == 1.0, y2, jnp.where(u == 0.0, -1.0, (u - 1.0) * y2 / jnp.log(u)))
    mult = jnp.sqrt(-em1)
    return a, mult * i * xc


def _scan_rows(a, b, group):
    rows = a.shape[0]
    pos = lax.broadcasted_iota(jnp.int32, a.shape, 0) % group
    d = 1
    while d < group:
        keep = pos >= d
        b = jnp.where(keep, a * pltpu.roll(b, d, 0) + b, b)
        a = jnp.where(keep, a * pltpu.roll(a, d, 0), a)
        d *= 2
    return a, b


def _lru_prompt_kernel(p_ref, cw_ref, cb_ref, wga_ref, bga_ref, wgx_ref, bgx_ref, lam_ref,
                       o_ref, h_ref, tail_ref, hc_ref):
    n = pl.program_id(1)
    rows = p_ref.shape[1]

    @pl.when(n == 0)
    def _():
        tail_ref[...] = jnp.zeros_like(tail_ref)
        hc_ref[...] = jnp.zeros_like(hc_ref)

    y = p_ref[0, :, :D_RNN]
    x = p_ref[0, :, D_RNN:]
    tail = tail_ref[...]
    r8 = lax.broadcasted_iota(jnp.int32, (V7X_SUBLANES, D_RNN), 0)
    shifted = []
    for s in range(1, CONV_WIDTH):
        xs = pltpu.roll(x, s, 0)
        head = jnp.where(r8 < s, pltpu.roll(tail, s, 0), xs[:V7X_SUBLANES])
        shifted.append(jnp.concatenate([head, xs[V7X_SUBLANES:]], axis=0))
    xc = _lru_conv(x, shifted, cw_ref, cb_ref)
    a, bterm = _lru_terms(xc, wga_ref, bga_ref, wgx_ref, bgx_ref, lam_ref)
    acum, hzero = _scan_rows(a, bterm, rows)
    hs = acum * hc_ref[0:1] + hzero
    o_ref[0] = _gelu_tanh(y) * hs
    last = hs[rows - 1:rows]
    h_ref[0] = last
    hc_ref[...] = jnp.broadcast_to(last, hc_ref.shape)
    tail_ref[...] = x[rows - V7X_SUBLANES:]


LRU_TILE = 256


def _lru_weight_specs(imap):
    return [pl.BlockSpec((CONV_WIDTH, D_RNN), imap(2)),
            pl.BlockSpec((1, D_RNN), imap(2)),
            pl.BlockSpec((LRU_BLOCKS, LRU_BLOCK, LRU_BLOCK), imap(3)),
            pl.BlockSpec((1, D_RNN), imap(2)),
            pl.BlockSpec((LRU_BLOCKS, LRU_BLOCK, LRU_BLOCK), imap(3)),
            pl.BlockSpec((1, D_RNN), imap(2)),
            pl.BlockSpec((1, D_RNN), imap(2))]


def _lru_prompt(proj, cw, cb, wga, bga, wgx, bgx, lam, batch, seq_len):
    R = LRU_TILE
    p3 = proj.reshape(batch, seq_len, 2 * D_RNN)
    nbytes = 2 * (R * 3 * D_RNN * 4 + 2 * LRU_BLOCKS * LRU_BLOCK * LRU_BLOCK * 2) + 24 * R * D_RNN * 4
    o, h = pl.pallas_call(
        _lru_prompt_kernel,
        grid=(batch, seq_len // R),
        in_specs=[pl.BlockSpec((1, R, 2 * D_RNN), lambda b, n: (b, n, 0))]
        + _lru_weight_specs(lambda nd: (lambda b, n: (0,) * nd)),
        out_specs=[pl.BlockSpec((1, R, D_RNN), lambda b, n: (b, n, 0)),
                   pl.BlockSpec((1, 1, D_RNN), lambda b, n: (b, 0, 0))],
        out_shape=[jax.ShapeDtypeStruct((batch, seq_len, D_RNN), F32),
                   jax.ShapeDtypeStruct((batch, 1, D_RNN), F32)],
        scratch_shapes=[pltpu.VMEM((V7X_SUBLANES, D_RNN), F32), pltpu.VMEM((V7X_SUBLANES, D_RNN), F32)],
        compiler_params=_params(("parallel", "arbitrary"), nbytes),
        name="lru_prompt",
    )(p3, cw, cb.reshape(1, D_RNN), wga.astype(BF16), bga.reshape(1, D_RNN), wgx.astype(BF16),
      bgx.reshape(1, D_RNN), lam.reshape(1, D_RNN))
    return o.reshape(batch * seq_len, D_RNN), h.reshape(batch, D_RNN)


def _lru_sample_kernel(p_ref, prev_ref, h0_ref, cw_ref, cb_ref, wga_ref, bga_ref, wgx_ref, bgx_ref, lam_ref,
                       o_ref, hs_ref, *, seq_len):
    rows = p_ref.shape[0]
    y = p_ref[:, :D_RNN]
    x = p_ref[:, D_RNN:]
    prev = prev_ref[...]
    pos = lax.broadcasted_iota(jnp.int32, (rows, D_RNN), 0) % seq_len
    shifted = [jnp.where(pos < s, pltpu.roll(prev, rows - seq_len + s, 0), pltpu.roll(x, s, 0))
               for s in range(1, CONV_WIDTH)]
    xc = _lru_conv(x, shifted, cw_ref, cb_ref)
    a, bterm = _lru_terms(xc, wga_ref, bga_ref, wgx_ref, bgx_ref, lam_ref)
    acum, hzero = _scan_rows(a, bterm, seq_len)
    hs = acum * h0_ref[...] + hzero
    o_ref[...] = _gelu_tanh(y) * hs
    hs_ref[...] = hs


def _lru_sample(proj, conv_state, h0, cw, cb, wga, bga, wgx, bgx, lam, batch, seq_len):
    assert seq_len == V7X_SUBLANES
    n = batch * seq_len
    R = LRU_TILE
    prev = jnp.pad(conv_state, ((0, 0), (seq_len - (CONV_WIDTH - 1), 0), (0, 0))).reshape(n, D_RNN)
    h0_rows = jnp.repeat(h0, seq_len, axis=0)
    nbytes = 2 * (R * 6 * D_RNN * 4 + 2 * LRU_BLOCKS * LRU_BLOCK * LRU_BLOCK * 2) + 24 * R * D_RNN * 4
    return pl.pallas_call(
        functools.partial(_lru_sample_kernel, seq_len=seq_len),
        grid=(n // R,),
        in_specs=[pl.BlockSpec((R, 2 * D_RNN), lambda i: (i, 0)),
                  pl.BlockSpec((R, D_RNN), lambda i: (i, 0)),
                  pl.BlockSpec((R, D_RNN), lambda i: (i, 0))]
        + _lru_weight_specs(lambda nd: (lambda i: (0,) * nd)),
        out_specs=[pl.BlockSpec((R, D_RNN), lambda i: (i, 0)),
                   pl.BlockSpec((R, D_RNN), lambda i: (i, 0))],
        out_shape=[jax.ShapeDtypeStruct((n, D_RNN), F32), jax.ShapeDtypeStruct((n, D_RNN), F32)],
        compiler_params=_params(("parallel",), nbytes),
        name="lru_sample",
    )(proj, prev, h0_rows, cw, cb.reshape(1, D_RNN), wga.astype(BF16), bga.reshape(1, D_RNN), wgx.astype(BF16),
      bgx.reshape(1, D_RNN), lam.reshape(1, D_RNN))


def _run(gen):
    while True:
        try:
            next(gen)
        except StopIteration as stop:
            return stop.value


def _lockstep(gens):
    results = [None] * len(gens)
    live = list(range(len(gens)))
    anchor = None
    while live:
        for idx in list(live):
            try:
                gens[idx].send(anchor)
            except StopIteration as stop:
                results[idx] = stop.value
                live.remove(idx)
        if live:
            anchor = yield
    return results


def _topk_rows_steps(s, k):
    n = s.shape[0]
    rid = lax.broadcasted_iota(jnp.int32, s.shape, 0).astype(F32)
    vals, ids = [], []
    for _ in range(k):
        m = jnp.max(s, axis=0, keepdims=True)
        ix = jnp.min(jnp.where(s == m, rid, float(n)), axis=0, keepdims=True)
        vals.append(m)
        ids.append(ix)
        s = jnp.where(rid == ix, -jnp.inf, s)
        anchor = yield
        if anchor is not None:
            s = s + anchor
    return jnp.concatenate(vals, axis=0), jnp.concatenate(ids, axis=0).astype(jnp.int32)


def _zero_from(parts):
    acc = None
    for x in parts:
        bits = pltpu.bitcast(x, jnp.uint32)
        bits = bits.reshape(bits.shape[0] // V7X_SUBLANES, V7X_SUBLANES, bits.shape[1])
        folded = bits[0]
        for r in range(1, bits.shape[0]):
            folded = folded | bits[r]
        acc = folded if acc is None else acc | folded
    cols = [acc[:, t * V7X_LANES:(t + 1) * V7X_LANES] for t in range(acc.shape[1] // V7X_LANES)]
    one = cols[0]
    for t in cols[1:]:
        one = one | t
    zero = lax.shift_right_logical(lax.shift_right_logical(one, jnp.uint32(16)), jnp.uint32(16))
    return pltpu.bitcast(zero, F32)[0:1, 0:1]


def _staircase_candidates(s1, s2):
    K = s1.shape[0]
    sub = V7X_SUBLANES
    first_single = next(a for a in range(K) if K // (a + 1) == 1)
    assert (K - first_single) % sub == 0
    pieces, starts, at = [], [], 0
    for a in range(first_single):
        nb = K // (a + 1)
        rows = -(-nb // sub) * sub
        piece = s1[a:a + 1] + s2[:rows]
        if rows != nb:
            piece = jnp.where(lax.broadcasted_iota(jnp.int32, piece.shape, 0) < nb, piece, -jnp.inf)
        pieces.append(piece)
        starts.append(at)
        at += rows
    pieces.append(s1[first_single:] + s2[0:1])
    return jnp.concatenate(pieces, axis=0), starts, at


ROUTE_STEPS = 2 + 2 * PEER_TOPK


def _route_head_steps(q_ref, sk_ref, stage_ref):
    K = PEER_TOPK
    par = pl.program_id(1) % 2
    for p in range(2):
        qh = q_ref[:, p * PEER_HALF:(p + 1) * PEER_HALF].astype(BF16)
        stage_ref[par, p] = lax.dot_general(sk_ref[0, p].astype(BF16), qh, (((1,), (1,)), ((), ())),
                                            preferred_element_type=F32)
    anchor = yield
    scores = [stage_ref[par, 0], stage_ref[par, 1]]
    if anchor is not None:
        scores = [st + anchor for st in scores]
    (s1, i1), (s2, i2) = yield from _lockstep([_topk_rows_steps(st, K) for st in scores])
    anchor = yield
    cand, starts, single_start = _staircase_candidates(s1, s2)
    if anchor is not None:
        cand = cand + anchor
    top, ci = yield from _topk_rows_steps(cand, K)
    a_id = jnp.zeros_like(ci)
    group_start = jnp.zeros_like(ci)
    for a in range(1, len(starts)):
        a_id = jnp.where(ci >= starts[a], a, a_id)
        group_start = jnp.where(ci >= starts[a], starts[a], group_start)
    single = ci >= single_start
    a_id = jnp.where(single, len(starts) + ci - single_start, a_id)
    b_id = jnp.where(single, 0, ci - group_start)
    e1 = jnp.zeros_like(ci)
    e2 = jnp.zeros_like(ci)
    for a in range(K):
        e1 = jnp.where(a_id == a, i1[a:a + 1], e1)
        e2 = jnp.where(b_id == a, i2[a:a + 1], e2)
    e = jnp.exp(top - top[0:1])
    return e1, e2, e / jnp.sum(e, axis=0, keepdims=True)


def _route_kernel(q_ref, sk_ref, e1_ref, e2_ref, g_ref, stage_ref):
    e1_ref[0], e2_ref[0], g_ref[0] = _run(_route_head_steps(q_ref, sk_ref, stage_ref))


EXPERT_TILE = 256


def _peer_route(q, sub_keys):
    n = q.shape[0]
    tb = EXPERT_TILE
    spec = pl.BlockSpec((1, PEER_TOPK, tb), lambda i, h: (h, 0, i))
    nbytes = 2 * (tb * PEER_KEY_DIM * 4 + 2 * PEER_NKEYS * PEER_HALF * 4) + 16 * 2 * PEER_NKEYS * tb * 4
    return pl.pallas_call(
        _route_kernel,
        grid=(n // tb, PEER_HEADS),
        in_specs=[pl.BlockSpec((tb, PEER_KEY_DIM), lambda i, h: (i, h)),
                  pl.BlockSpec((1, 2, PEER_NKEYS, PEER_HALF), lambda i, h: (h, 0, 0, 0))],
        out_specs=[spec, spec, spec],
        out_shape=[jax.ShapeDtypeStruct((PEER_HEADS, PEER_TOPK, n), jnp.int32),
                   jax.ShapeDtypeStruct((PEER_HEADS, PEER_TOPK, n), jnp.int32),
                   jax.ShapeDtypeStruct((PEER_HEADS, PEER_TOPK, n), F32)],
        scratch_shapes=[pltpu.VMEM((2, 2, PEER_NKEYS, tb), F32)],
        compiler_params=_params(("parallel", "parallel"), nbytes),
        name="peer_route",
    )(q, sub_keys)


EXPERT_CHUNK = PEER_EXPERTS // PEER_HEADS
EXPERT_SUB = 256
WEIGHT_BLOCK = 512
GATE_ROW_PAD = V7X_SUBLANES
GATE_UNROLL = 64


def _expert_kernel(h_ref, q_ref, sk_ref, e1_ref, e2_ref, g_ref, ut_ref, v0_ref, v1_ref, x_ref, gt_ref, o_ref,
                   gate_ref, acc_ref, e1t_ref, e2t_ref, gtt_ref, re1_ref, re2_ref, rg_ref, stage_ref, w_ref):
    i = pl.program_id(0)
    c = pl.program_id(1)
    tb = h_ref.shape[0]
    NK = PEER_NKEYS
    stride = tb + GATE_ROW_PAD
    slot = i % 2

    @pl.when((i == 0) & (c == 0))
    def _():
        re1_ref[0] = e1_ref[...].reshape(PEER_PAIRS, tb)
        re2_ref[0] = e2_ref[...].reshape(PEER_PAIRS, tb)
        rg_ref[0] = g_ref[...].reshape(PEER_PAIRS, tb)

    @pl.when(c == 0)
    def _():
        acc_ref[...] = jnp.zeros_like(acc_ref)
        e1t_ref[...] = jnp.transpose(re1_ref[slot])
        e2t_ref[...] = jnp.transpose(re2_ref[slot])
        gtt_ref[...] = jnp.transpose(rg_ref[slot])
        kid = lax.broadcasted_iota(jnp.int32, (NK, PEER_PAIRS), 0)

        def per_token(n, carry):
            i1 = e1t_ref[pl.ds(n, 1), :]
            i2 = e2t_ref[pl.ds(n, 1), :]
            gg = gtt_ref[pl.ds(n, 1), :]
            a_t = jnp.where(kid == i1, 1.0, 0.0).astype(BF16)
            b_t = jnp.where(kid == i2, 0.5 * gg, 0.0).astype(BF16)
            gn = lax.dot_general(a_t, b_t, (((1,), (1,)), ((), ())), preferred_element_type=F32)
            gate_ref[pl.ds(n, NK, stride=stride), :] = gn
            return carry

        lax.fori_loop(0, tb, per_token, 0, unroll=GATE_UNROLL)

    route = _route_head_steps(q_ref, sk_ref, stage_ref)
    routed = []
    n_sub = EXPERT_CHUNK // EXPERT_SUB

    def advance_route(rounds, anchor=None):
        for _ in range(rounds):
            if routed:
                return
            try:
                route.send(anchor)
            except StopIteration as stop:
                routed.append(stop.value)
            anchor = None

    h = h_ref[...]
    keys_per_chunk = EXPERT_CHUNK // NK
    advance_route(1)
    WB = WEIGHT_BLOCK
    w_parts = []
    for j in range(n_sub):
        lo = j * EXPERT_SUB
        blk, off = divmod(lo, WB)
        s = jnp.dot(h, ut_ref[0, blk, :, off:off + EXPERT_SUB], preferred_element_type=F32)
        gsel = jnp.concatenate(
            [gate_ref[pl.ds(pl.multiple_of((c * keys_per_chunk + lo // NK + t) * stride, V7X_SUBLANES), tb), :]
             for t in range(EXPERT_SUB // NK)], axis=1)
        wb = _gelu_times_half_gate(s, gsel).astype(BF16)
        w_parts.append(wb)
        w_ref[blk, :, off:off + EXPERT_SUB] = wb
    anchor = _zero_from(w_parts)
    for n, vn_ref in enumerate((v0_ref, v1_ref)):
        part = None
        for k in range(EXPERT_CHUNK // WB):
            d = jnp.dot(w_ref[k], vn_ref[0, k * WB:(k + 1) * WB, :], preferred_element_type=F32)
            part = d if part is None else part + d
        acc_ref[:, n * WB:(n + 1) * WB] += part
    advance_route(ROUTE_STEPS + 1, anchor)
    n1, n2, ng = routed[0]
    rows = pl.ds(pl.multiple_of(c * PEER_TOPK, PEER_TOPK), PEER_TOPK)
    re1_ref[1 - slot, rows, :] = n1
    re2_ref[1 - slot, rows, :] = n2
    rg_ref[1 - slot, rows, :] = ng

    @pl.when(c == pl.num_programs(1) - 1)
    def _():
        o_ref[...] = x_ref[...] + gt_ref[0] * acc_ref[...]


def _peer_u_blocks(u):
    nl, e, d = u.shape
    return u.reshape(nl, e // WEIGHT_BLOCK, WEIGHT_BLOCK, d).transpose(0, 1, 3, 2).astype(BF16)


def _peer_experts(h_bf16, q, sub_keys, u_blocks, v_bf16, layer, x, gate, seq_len):
    n, d = x.shape
    tb = EXPERT_TILE
    ec = EXPERT_CHUNK
    wb = WEIGHT_BLOCK
    assert d == 2 * wb
    nt = n // tb
    e1, e2, g = _peer_route(q[:tb], sub_keys)
    gt_arr, gt_spec0 = _row_operand(gate, seq_len, tb)
    gt_spec = pl.BlockSpec(gt_spec0.block_shape, lambda i, c: gt_spec0.index_map(i))
    rspec = pl.BlockSpec((PEER_HEADS, PEER_TOPK, tb), lambda i, c: (0, 0, 0))
    nbytes = (2 * (tb * d * 2 + 3 * PEER_PAIRS * tb * 4 + 2 * d * ec * 2 + 3 * tb * d * 4 + tb * PEER_KEY_DIM * 4)
              + (tb + GATE_ROW_PAD) * PEER_NKEYS * PEER_NKEYS * 4 + tb * d * 4 + 9 * tb * PEER_PAIRS * 4
              + 8 * tb * EXPERT_SUB * 4 + 8 * PEER_NKEYS * tb * 4)
    return pl.pallas_call(
        _expert_kernel,
        grid=(nt, PEER_HEADS),
        in_specs=[pl.BlockSpec((tb, d), lambda i, c: (i, 0)),
                  pl.BlockSpec((tb, PEER_KEY_DIM), lambda i, c: (jnp.minimum(i + 1, nt - 1), c)),
                  pl.BlockSpec((1, 2, PEER_NKEYS, PEER_HALF), lambda i, c: (c, 0, 0, 0)),
                  rspec, rspec, rspec,
                  pl.BlockSpec((1, ec // wb, d, wb), lambda i, c: (layer, c, 0, 0)),
                  pl.BlockSpec((1, ec, wb), lambda i, c: (layer, c, 0)),
                  pl.BlockSpec((1, ec, wb), lambda i, c: (layer, c, 1)),
                  pl.BlockSpec((tb, d), lambda i, c: (i, 0)),
                  gt_spec],
        out_specs=pl.BlockSpec((tb, d), lambda i, c: (i, 0)),
        out_shape=jax.ShapeDtypeStruct((n, d), F32),
        scratch_shapes=[pltpu.VMEM(((tb + GATE_ROW_PAD) * PEER_NKEYS, PEER_NKEYS), F32),
                        pltpu.VMEM((tb, d), F32),
                        pltpu.VMEM((tb, PEER_PAIRS), jnp.int32),
                        pltpu.VMEM((tb, PEER_PAIRS), jnp.int32),
                        pltpu.VMEM((tb, PEER_PAIRS), F32),
                        pltpu.VMEM((2, PEER_PAIRS, tb), jnp.int32),
                        pltpu.VMEM((2, PEER_PAIRS, tb), jnp.int32),
                        pltpu.VMEM((2, PEER_PAIRS, tb), F32),
                        pltpu.VMEM((2, 2, PEER_NKEYS, tb), F32),
                        pltpu.VMEM((ec // wb, tb, wb), BF16)],
        compiler_params=_params(("arbitrary", "arbitrary"), nbytes),
        name="peer_experts",
    )(h_bf16, q, sub_keys, e1, e2, g, u_blocks, v_bf16, v_bf16, x, gt_arr)


def _swa_permute_in(w_in):
    d = w_in.shape[0]
    wq = w_in[:, :SWA_Q].reshape(d, SWA_KV_HEADS, SWA_GROUP, SWA_HEAD_DIM).transpose(0, 2, 1, 3).reshape(d, SWA_Q)
    return jnp.concatenate([wq, w_in[:, SWA_Q:]], axis=1)


def _swa_permute_out(w_out):
    d = w_out.shape[1]
    return w_out.reshape(SWA_KV_HEADS, SWA_GROUP, SWA_HEAD_DIM, d).transpose(1, 0, 2, 3).reshape(SWA_Q, d)


def _prepare_weights(p):
    w = {}
    w['gla_in'] = [jnp.pad(p['w_gla_in'][j], ((0, 0), (0, GLA_IN_PAD - p['w_gla_in'].shape[2]))).astype(BF16)
                   for j in range(p['w_gla_in'].shape[0])]
    w['gla_a2'] = [jnp.pad(p['w_gla_a2'][j], ((0, V7X_LANES - GLA_GATE_RANK), (0, 0)))
                   for j in range(p['w_gla_a2'].shape[0])]
    w['gla_out'] = [m.astype(BF16) for m in p['w_gla_out']]
    w['swa_in'] = [_swa_permute_in(m).astype(BF16) for m in p['w_swa_in']]
    w['swa_out'] = [_swa_permute_out(m).astype(BF16) for m in p['w_swa_out']]
    w['lru_in'] = [m.astype(BF16) for m in p['w_lru_in']]
    w['lru_out'] = [m.astype(BF16) for m in p['w_lru_out']]
    w['peer_q'] = [m.astype(BF16) for m in p['w_peer_q']]
    w['peer_ut'] = _peer_u_blocks(p['peer_u'])
    w['peer_v'] = p['peer_v'].astype(BF16)
    return w


def _trunk(x3, mod, states, p, w):
    batch, seq_len, d = x3.shape
    x = x3.reshape(batch * seq_len, d)
    new_gla, new_k, new_v, new_conv, new_h = [], [], [], [], []
    gla_stack = None
    for i in range(DEPTH):
        kind, j = i % N_MIXERS, i // N_MIXERS
        sh_m, sc_m, gt_m, sh_f, sc_f, gt_f = [mod[i][:, k * d:(k + 1) * d] for k in range(6)]
        if kind == 0:
            proj = _norm_proj(x, p['g_ln_mix'][i], sc_m, sh_m, w['gla_in'][j], seq_len)
            if states is None:
                mix, s_new = _gla_prompt(proj, w['gla_a2'][j], p['b_gla_a'][j], p['g_gla_norm'][j], batch, seq_len)
                new_gla.append(s_new)
            else:
                mix, gla_stack = _gla_sample(proj, states[0], j, gla_stack, w['gla_a2'][j], p['b_gla_a'][j],
                                             p['g_gla_norm'][j], batch, seq_len)
            w_out = w['gla_out'][j]
        elif kind == 1:
            proj = _norm_proj(x, p['g_ln_mix'][i], sc_m, sh_m, w['swa_in'][j], seq_len)
            if states is None:
                mix, k_n, v_n = _swa_prompt(proj, p['g_swa_q'][j], p['g_swa_k'][j], p['swa_sinks'][j], batch, seq_len)
            else:
                kc = states[1][j].reshape(batch, WINDOW, SWA_KV)
                vc = states[2][j].reshape(batch, WINDOW, SWA_KV)
                mix, k_n, v_n = _swa_sample(proj, kc, vc, p['g_swa_q'][j], p['g_swa_k'][j], p['swa_sinks'][j],
                                            batch, seq_len)
            new_k.append(k_n.reshape(batch, WINDOW, SWA_KV_HEADS, SWA_HEAD_DIM))
            new_v.append(v_n.reshape(batch, WINDOW, SWA_KV_HEADS, SWA_HEAD_DIM))
            w_out = w['swa_out'][j]
        else:
            proj = _norm_proj(x, p['g_ln_mix'][i], sc_m, sh_m, w['lru_in'][j], seq_len)
            lru_args = (p['lru_conv_w'][j], p['lru_conv_b'][j], p['w_lru_ga'][j], p['b_lru_ga'][j],
                        p['w_lru_gx'][j], p['b_lru_gx'][j], p['lru_lam'][j], batch, seq_len)
            assert seq_len >= CONV_WIDTH - 1
            if states is None:
                mix, h_n = _lru_prompt(proj, *lru_args)
            else:
                mix, hs = _lru_sample(proj, states[3][j], states[4][j], *lru_args)
                h_n = hs.reshape(batch, seq_len, D_RNN)[:, -1]
            new_conv.append(proj[:, D_RNN:].reshape(batch, seq_len, D_RNN)[:, seq_len - (CONV_WIDTH - 1):])
            new_h.append(h_n)
            w_out = w['lru_out'][j]
        x = _proj_residual(mix, w_out, x, gt_m, seq_len)
        q, hb = _norm_proj(x, p['g_ln_ffn'][i], sc_f, sh_f, w['peer_q'][i], seq_len, with_h=True)
        x = _peer_experts(hb, q, p['peer_sub_keys'][i], w['peer_ut'], w['peer_v'], i, x, gt_f, seq_len)
    y = x.reshape(batch, seq_len, d)
    gla_out = jnp.stack(new_gla) if states is None else gla_stack
    return y, (gla_out, jnp.stack(new_k), jnp.stack(new_v), jnp.stack(new_conv), jnp.stack(new_h))


def kernel(x_prompt, x_sample, state_gla, cache_swa_k, cache_swa_v, state_lru_conv, state_lru_h,
           c_prompt, c_sample, g_ln_mix, g_ln_ffn, w_mod, b_mod,
           w_gla_in, w_gla_a2, b_gla_a, g_gla_norm, w_gla_out,
           w_swa_in, g_swa_q, g_swa_k, swa_sinks, w_swa_out,
           w_lru_in, lru_conv_w, lru_conv_b, w_lru_ga, b_lru_ga, w_lru_gx, b_lru_gx, lru_lam, w_lru_out,
           w_peer_q, peer_sub_keys, peer_u, peer_v):
    p = {'g_ln_mix': g_ln_mix, 'g_ln_ffn': g_ln_ffn,
         'w_gla_in': w_gla_in, 'w_gla_a2': w_gla_a2, 'b_gla_a': b_gla_a, 'g_gla_norm': g_gla_norm,
         'w_gla_out': w_gla_out,
         'w_swa_in': w_swa_in, 'g_swa_q': g_swa_q, 'g_swa_k': g_swa_k, 'swa_sinks': swa_sinks,
         'w_swa_out': w_swa_out,
         'w_lru_in': w_lru_in, 'lru_conv_w': lru_conv_w, 'lru_conv_b': lru_conv_b,
         'w_lru_ga': w_lru_ga, 'b_lru_ga': b_lru_ga, 'w_lru_gx': w_lru_gx, 'b_lru_gx': b_lru_gx,
         'lru_lam': lru_lam, 'w_lru_out': w_lru_out,
         'w_peer_q': w_peer_q, 'peer_sub_keys': peer_sub_keys, 'peer_u': peer_u, 'peer_v': peer_v}
    w = _prepare_weights(p)
    nb_p, nb_s = c_prompt.shape[0], c_sample.shape[0]
    rows = -(-(nb_p + nb_s) // V7X_SUBLANES) * V7X_SUBLANES
    c_all = jnp.pad(jnp.concatenate([c_prompt, c_sample], axis=0), ((0, rows - nb_p - nb_s), (0, 0)))
    mod = _modulation(c_all, w_mod, b_mod)
    y_p, (gla_p, k_p, v_p, conv_p, h_p) = _trunk(x_prompt, mod[:, :nb_p], None, p, w)
    y_s, (gla_s, k_s, v_s, conv_s, h_s) = _trunk(
        x_sample, mod[:, nb_p:nb_p + nb_s],
        (state_gla, cache_swa_k, cache_swa_v, state_lru_conv, state_lru_h), p, w)
    return (y_p, y_s, gla_p, gla_s, k_p, k_s, v_p, v_s, conv_p, conv_s, h_p, h_s)
```

```python
import functools
import math

import jax
import jax.numpy as jnp
from jax import lax
from jax.experimental import pallas as pl
from jax.experimental.pallas import tpu as pltpu

F32 = jnp.float32
BF16 = jnp.bfloat16

D_MODEL = 1024
DEPTH = 4
N_MIXERS = 3
RMS_EPS = 1e-6

GLA_HEADS = 4
GLA_QK = D_MODEL // 2
GLA_V = D_MODEL
GLA_DK = GLA_QK // GLA_HEADS
GLA_DV = GLA_V // GLA_HEADS
GLA_GATE_RANK = 16
GLA_TAU = 16.0
GLA_SUB = 16
GLA_CHUNK = 128
GLA_IN_PAD = 2 * GLA_QK + 2 * GLA_V + 128

SWA_HEAD_DIM = 64
SWA_Q_HEADS = D_MODEL // SWA_HEAD_DIM
SWA_KV_HEADS = 4
SWA_GROUP = SWA_Q_HEADS // SWA_KV_HEADS
SWA_Q = SWA_Q_HEADS * SWA_HEAD_DIM
SWA_KV = SWA_KV_HEADS * SWA_HEAD_DIM
WINDOW = 128

D_RNN = D_MODEL
LRU_BLOCKS = 4
LRU_BLOCK = D_RNN // LRU_BLOCKS
CONV_WIDTH = 4
LRU_C = 8.0

PEER_HEADS = 8
PEER_NKEYS = 128
PEER_EXPERTS = PEER_NKEYS * PEER_NKEYS
PEER_KEY_DIM = 256
PEER_HALF = PEER_KEY_DIM // 2
PEER_TOPK = 16
PEER_PAIRS = PEER_HEADS * PEER_TOPK

V7X_LANES = 128
V7X_SUBLANES = 8
V7X_VMEM_BYTES = 64 * 1024 * 1024

TOKEN_TILE = 512
NEG_BIG = -1e30


def _vmem_limit(nbytes):
    return int(min(max(nbytes * 3 // 2, 16 * 1024 * 1024), V7X_VMEM_BYTES - 8 * 1024 * 1024))


def _params(semantics, nbytes):
    return pltpu.CompilerParams(dimension_semantics=semantics, vmem_limit_bytes=_vmem_limit(nbytes))


def _rms(x, g):
    return x * lax.rsqrt(jnp.mean(x * x, axis=-1, keepdims=True) + RMS_EPS) * g


def _gelu_tanh(x):
    return 0.5 * x * (1.0 + jnp.tanh(math.sqrt(2.0 / math.pi) * (x + 0.044715 * (x * x * x))))


def _gelu_times_half_gate(x, half_gate):
    c1 = math.sqrt(2.0 / math.pi)
    inner = x * (c1 + (c1 * 0.044715) * (x * x))
    return (x * half_gate) * (1.0 + jnp.tanh(inner))


def _sigmoid(x):
    return 1.0 / (1.0 + jnp.exp(-x))


def _softplus(x):
    return jnp.maximum(x, 0.0) + jnp.log1p(jnp.exp(-jnp.abs(x)))


def _row_operand(vec, seq_len, tile):
    b, d = vec.shape
    if seq_len % tile == 0:
        per_seq = seq_len // tile
        return vec.reshape(b, 1, d), pl.BlockSpec((1, 1, d), lambda i: (i // per_seq, 0, 0))
    assert tile % seq_len == 0
    rep = jnp.repeat(vec, seq_len, axis=0).reshape(b * seq_len // tile, tile, d)
    return rep, pl.BlockSpec((1, tile, d), lambda i: (i, 0, 0))


def _mod_kernel(c_ref, w_ref, b_ref, o_ref):
    c = c_ref[...]
    sc = c * _sigmoid(c)
    o_ref[0] = jnp.dot(sc, w_ref[0], preferred_element_type=F32) + b_ref[0]


def _modulation(c, w_mod, b_mod):
    bp, d = c.shape
    tn = 1024
    nt = 6 * d // tn
    return pl.pallas_call(
        _mod_kernel,
        grid=(DEPTH, nt),
        in_specs=[pl.BlockSpec((bp, d), lambda l, j: (0, 0)),
                  pl.BlockSpec((1, d, tn), lambda l, j: (l, 0, j)),
                  pl.BlockSpec((1, 1, tn), lambda l, j: (l, 0, j))],
        out_specs=pl.BlockSpec((1, bp, tn), lambda l, j: (l, 0, j)),
        out_shape=jax.ShapeDtypeStruct((DEPTH, bp, 6 * d), F32),
        compiler_params=_params(("parallel", "parallel"), 2 * (d * tn * 4 + 2 * bp * tn * 4)),
        name="modulation",
    )(c, w_mod, b_mod.reshape(DEPTH, 1, 6 * d))


def _norm_proj_kernel(x_ref, g_ref, sc_ref, sh_ref, w_ref, o_ref, *h_ref):
    h = _rms(x_ref[...], g_ref[...]) * (1.0 + sc_ref[0]) + sh_ref[0]
    hb = h.astype(BF16)
    o_ref[...] = jnp.dot(hb, w_ref[...], preferred_element_type=F32).astype(o_ref.dtype)
    if h_ref:
        h_ref[0][...] = hb


def _norm_proj(x, g, scale, shift, w_bf16, seq_len, with_h=False):
    n, d = x.shape
    nout = w_bf16.shape[1]
    tm = TOKEN_TILE
    sc_arr, sc_spec = _row_operand(scale, seq_len, tm)
    sh_arr, sh_spec = _row_operand(shift, seq_len, tm)
    out_shape = [jax.ShapeDtypeStruct((n, nout), BF16 if with_h else F32)]
    out_specs = [pl.BlockSpec((tm, nout), lambda i: (i, 0))]
    if with_h:
        out_shape.append(jax.ShapeDtypeStruct((n, d), BF16))
        out_specs.append(pl.BlockSpec((tm, d), lambda i: (i, 0)))
    nbytes = 2 * (tm * d * 4 + d * nout * 2 + tm * nout * 4 + 3 * tm * d * 4)
    res = pl.pallas_call(
        _norm_proj_kernel,
        grid=(n // tm,),
        in_specs=[pl.BlockSpec((tm, d), lambda i: (i, 0)),
                  pl.BlockSpec((1, d), lambda i: (0, 0)),
                  sc_spec, sh_spec,
                  pl.BlockSpec((d, nout), lambda i: (0, 0))],
        out_specs=out_specs,
        out_shape=out_shape,
        compiler_params=_params(("parallel",), nbytes),
        name="norm_proj",
    )(x, g.reshape(1, d), sc_arr, sh_arr, w_bf16)
    return res if with_h else res[0]


def _residual_then_proj_kernel(a_ref, wo_ref, x_ref, gt_ref, g_ref, sc_ref, sh_ref, wq_ref, xo_ref, q_ref, h_ref):
    y = jnp.dot(a_ref[...].astype(BF16), wo_ref[...], preferred_element_type=F32)
    x_new = x_ref[...] + gt_ref[0] * y
    xo_ref[...] = x_new
    hb = (_rms(x_new, g_ref[...]) * (1.0 + sc_ref[0]) + sh_ref[0]).astype(BF16)
    h_ref[...] = hb
    q_ref[...] = jnp.dot(hb, wq_ref[...], preferred_element_type=F32).astype(q_ref.dtype)


def _residual_then_proj(a, wo_bf16, x, gate, g, scale, shift, wq_bf16, seq_len):
    n, k = a.shape
    d = x.shape[1]
    nq = wq_bf16.shape[1]
    tm = TOKEN_TILE
    gt_arr, gt_spec = _row_operand(gate, seq_len, tm)
    sc_arr, sc_spec = _row_operand(scale, seq_len, tm)
    sh_arr, sh_spec = _row_operand(shift, seq_len, tm)
    nbytes = 2 * (tm * k * 4 + k * d * 2 + d * nq * 2 + 2 * tm * d * 4 + tm * nq * 2 + tm * d * 2 + 3 * tm * d * 4)
    row = lambda i: (i, 0)
    fixed = lambda i: (0, 0)
    return pl.pallas_call(
        _residual_then_proj_kernel,
        grid=(n // tm,),
        in_specs=[pl.BlockSpec((tm, k), row), pl.BlockSpec((k, d), fixed), pl.BlockSpec((tm, d), row), gt_spec,
                  pl.BlockSpec((1, d), fixed), sc_spec, sh_spec, pl.BlockSpec((d, nq), fixed)],
        out_specs=[pl.BlockSpec((tm, d), row), pl.BlockSpec((tm, nq), row), pl.BlockSpec((tm, d), row)],
        out_shape=[jax.ShapeDtypeStruct((n, d), F32), jax.ShapeDtypeStruct((n, nq), BF16),
                   jax.ShapeDtypeStruct((n, d), BF16)],
        compiler_params=_params(("parallel",), nbytes),
        name="residual_then_proj",
    )(a, wo_bf16, x, gt_arr, g.reshape(1, d), sc_arr, sh_arr, wq_bf16)


def _log_decay(lr, wa2, ba):
    z = jnp.dot(lr, wa2, preferred_element_type=F32) + ba
    return (jnp.minimum(z, 0.0) - jnp.log1p(jnp.exp(-jnp.abs(z)))) * (1.0 / GLA_TAU)


def _col_bcast(row):
    return jnp.transpose(jnp.broadcast_to(row, (V7X_LANES, V7X_LANES)))


def _head_out(o, gate, gn):
    return _rms(o, gn) * (gate * _sigmoid(gate))


def _gla_prompt_kernel(p_ref, wa2_ref, ba_ref, gn_ref, o_ref, s_ref):
    c = pl.program_id(0)
    C = GLA_CHUNK
    nsub = C // GLA_SUB

    @pl.when(c == 0)
    def _():
        s_ref[...] = jnp.zeros_like(s_ref)

    row = lax.broadcasted_iota(jnp.int32, (C, C), 0)
    col = lax.broadcasted_iota(jnp.int32, (C, C), 1)
    tri = (col <= row).astype(F32)
    later = ((col > row) & (col // GLA_SUB == row // GLA_SUB)).astype(F32)
    sums = jnp.concatenate([tri, later], axis=0)
    causal = col <= row
    rsub = lax.broadcasted_iota(jnp.int32, (C, GLA_DK), 0) // GLA_SUB

    for seq in range(p_ref.shape[0]):
        _gla_prompt_chunk(p_ref.at[seq], wa2_ref, ba_ref, gn_ref, o_ref.at[seq], s_ref.at[seq],
                          sums, causal, rsub)


def _gla_prompt_chunk(p_ref, wa2_ref, ba_ref, gn_ref, o_ref, s_ref, sums, causal, rsub):
    C = GLA_CHUNK
    nsub = C // GLA_SUB
    lr = p_ref[:, 2 * GLA_QK + 2 * GLA_V:]
    la_all = _log_decay(lr, wa2_ref[...], ba_ref[...])
    cs_all = jnp.dot(sums, la_all, preferred_element_type=F32, precision=lax.Precision.HIGHEST)
    for h in range(GLA_HEADS):
        q = p_ref[:, h * GLA_DK:(h + 1) * GLA_DK] * (GLA_DK ** -0.5)
        k = p_ref[:, GLA_QK + h * GLA_DK:GLA_QK + (h + 1) * GLA_DK]
        v = p_ref[:, 2 * GLA_QK + h * GLA_DV:2 * GLA_QK + (h + 1) * GLA_DV]
        gate = p_ref[:, 2 * GLA_QK + GLA_V + h * GLA_DV:2 * GLA_QK + GLA_V + (h + 1) * GLA_DV]
        cs = cs_all[:, h * GLA_DK:(h + 1) * GLA_DK]
        b = cs[:C]
        to_sub_end = cs[C:]
        b_last = b[C - 1:C]
        k_sub = k * jnp.exp(to_sub_end)
        q_parts, k_parts = [], []
        for m in range(nsub):
            ref_row = b[m * GLA_SUB + GLA_SUB - 1:m * GLA_SUB + GLA_SUB]
            e = jnp.where(rsub >= m, b - ref_row, NEG_BIG)
            q_parts.append((q * jnp.exp(e)).astype(BF16))
            k_parts.append(jnp.where(rsub == m, k_sub, 0.0).astype(BF16))
        qcat = jnp.concatenate(q_parts, axis=1)
        kcat = jnp.concatenate(k_parts, axis=1)
        att = lax.dot_general(qcat, kcat, (((1,), (1,)), ((), ())), preferred_element_type=F32)
        att = jnp.where(causal, att, 0.0)
        s_old = s_ref[h]
        o = jnp.dot(att.astype(BF16), v.astype(BF16), preferred_element_type=F32)
        o = o + jnp.dot((q * jnp.exp(b)).astype(BF16), s_old.astype(BF16), preferred_element_type=F32)
        k_end = (k * jnp.exp(b_last - b)).astype(BF16)
        upd = lax.dot_general(k_end, v.astype(BF16), (((0,), (0,)), ((), ())), preferred_element_type=F32)
        decay = _col_bcast(jnp.exp(b_last))
        s_ref[h] = jnp.concatenate([decay] * (GLA_DV // V7X_LANES), axis=1) * s_old + upd
        o_ref[:, h * GLA_DV:(h + 1) * GLA_DV] = _head_out(o, gate, gn_ref[...])


def _gla_prompt(proj, wa2_pad, ba, gn, batch, seq_len):
    C = GLA_CHUNK
    p3 = proj.reshape(batch, seq_len, GLA_IN_PAD)
    nbytes = batch * (2 * (C * GLA_IN_PAD * 4 + C * GLA_V * 4 + GLA_HEADS * GLA_DK * GLA_DV * 4) + 64 * C * C * 4)
    o, s = pl.pallas_call(
        _gla_prompt_kernel,
        grid=(seq_len // C,),
        in_specs=[pl.BlockSpec((batch, C, GLA_IN_PAD), lambda c: (0, c, 0)),
                  pl.BlockSpec((V7X_LANES, GLA_QK), lambda c: (0, 0)),
                  pl.BlockSpec((1, GLA_QK), lambda c: (0, 0)),
                  pl.BlockSpec((1, GLA_DV), lambda c: (0, 0))],
        out_specs=[pl.BlockSpec((batch, C, GLA_V), lambda c: (0, c, 0)),
                   pl.BlockSpec((batch, GLA_HEADS, GLA_DK, GLA_DV), lambda c: (0, 0, 0, 0))],
        out_shape=[jax.ShapeDtypeStruct((batch, seq_len, GLA_V), F32),
                   jax.ShapeDtypeStruct((batch, GLA_HEADS, GLA_DK, GLA_DV), F32)],
        compiler_params=_params(("arbitrary",), nbytes),
        name="gla_prompt",
    )(p3, wa2_pad, ba.reshape(1, GLA_QK), gn.reshape(1, GLA_DV))
    return o.reshape(batch * seq_len, GLA_V), s


GLA_SAMPLE_BATCH = 8


def _gla_sample_kernel(p_ref, s0_ref, wa2_ref, ba_ref, gn_ref, *refs):
    o_ref, s_ref = refs[-2:]
    nseq = s0_ref.shape[1]
    rows = p_ref.shape[0]
    T = rows // nseq
    row = lax.broadcasted_iota(jnp.int32, (rows, rows), 0)
    col = lax.broadcasted_iota(jnp.int32, (rows, rows), 1)
    same_seq = (row // T) == (col // T)
    causal = same_seq & (col <= row)
    sums = jnp.concatenate([causal.astype(F32), same_seq.astype(F32)], axis=0)
    lr = p_ref[:, 2 * GLA_QK + 2 * GLA_V:]
    la = _log_decay(lr, wa2_ref[...], ba_ref[...])
    cs = jnp.dot(sums, la, preferred_element_type=F32, precision=lax.Precision.HIGHEST)
    b_all, b_last_all = cs[:rows], cs[rows:]
    for h in range(GLA_HEADS):
        hk = slice(h * GLA_DK, (h + 1) * GLA_DK)
        q = p_ref[:, h * GLA_DK:(h + 1) * GLA_DK] * (GLA_DK ** -0.5)
        k = p_ref[:, GLA_QK + h * GLA_DK:GLA_QK + (h + 1) * GLA_DK]
        v = p_ref[:, 2 * GLA_QK + h * GLA_DV:2 * GLA_QK + (h + 1) * GLA_DV]
        gate = p_ref[:, 2 * GLA_QK + GLA_V + h * GLA_DV:2 * GLA_QK + GLA_V + (h + 1) * GLA_DV]
        b, b_last = b_all[:, hk], b_last_all[:, hk]
        k_end = k * jnp.exp(b_last - b)
        q_rel = (q * jnp.exp(b - b_last)).astype(BF16)
        q_dec = q * jnp.exp(b)
        att = lax.dot_general(q_rel, k_end.astype(BF16), (((1,), (1,)), ((), ())), preferred_element_type=F32)
        att = jnp.where(causal, att, 0.0)
        o_intra = jnp.dot(att.astype(BF16), v.astype(BF16), preferred_element_type=F32)
        o_inter = []
        for j in range(nseq):
            rj = slice(j * T, (j + 1) * T)
            s_old = s0_ref[0, j, h]
            o_inter.append(jnp.dot(q_dec[rj].astype(BF16), s_old.astype(BF16), preferred_element_type=F32))
            upd = lax.dot_general(k_end[rj].astype(BF16), v[rj].astype(BF16), (((0,), (0,)), ((), ())),
                                  preferred_element_type=F32)
            decay = _col_bcast(jnp.exp(b_last[j * T:j * T + 1]))
            s_ref[0, j, h] = jnp.concatenate([decay] * (GLA_DV // V7X_LANES), axis=1) * s_old + upd
        o = o_intra + jnp.concatenate(o_inter, axis=0)
        o_ref[:, h * GLA_DV:(h + 1) * GLA_DV] = _head_out(o, gate, gn_ref[...])


def _gla_sample(proj, states, layer, new_states, wa2_pad, ba, gn, batch, seq_len):
    assert seq_len <= GLA_SUB
    nb = GLA_SAMPLE_BATCH
    rows = nb * seq_len
    state_spec = pl.BlockSpec((1, nb, GLA_HEADS, GLA_DK, GLA_DV), lambda b: (layer, b, 0, 0, 0))
    state_block = nb * GLA_HEADS * GLA_DK * GLA_DV * 4
    nbytes = 2 * (rows * (GLA_IN_PAD + GLA_V) * 4 + 2 * state_block) + 16 * rows * GLA_IN_PAD * 4
    in_specs = [pl.BlockSpec((rows, GLA_IN_PAD), lambda b: (b, 0)),
                state_spec,
                pl.BlockSpec((V7X_LANES, GLA_QK), lambda b: (0, 0)),
                pl.BlockSpec((1, GLA_QK), lambda b: (0, 0)),
                pl.BlockSpec((1, GLA_DV), lambda b: (0, 0))]
    operands = [proj, states, wa2_pad, ba.reshape(1, GLA_QK), gn.reshape(1, GLA_DV)]
    aliases = {}
    if new_states is not None:
        in_specs.append(pl.BlockSpec(memory_space=pl.ANY))
        operands.append(new_states)
        aliases = {len(operands) - 1: 1}
    o, s = pl.pallas_call(
        _gla_sample_kernel,
        grid=(batch // nb,),
        in_specs=in_specs,
        out_specs=[pl.BlockSpec((rows, GLA_V), lambda b: (b, 0)), state_spec],
        out_shape=[jax.ShapeDtypeStruct((batch * seq_len, GLA_V), F32),
                   jax.ShapeDtypeStruct(states.shape, F32)],
        input_output_aliases=aliases,
        compiler_params=_params(("parallel",), nbytes),
        name="gla_sample",
    )(*operands)
    return o, s


def _head_group_norm(x, gain, gsum):
    sq = x * x
    hi = sq.astype(BF16)
    lo = (sq - hi.astype(F32)).astype(BF16)
    ms = (jnp.dot(hi, gsum, preferred_element_type=F32) + jnp.dot(lo, gsum, preferred_element_type=F32))
    return x * lax.rsqrt(ms * (1.0 / SWA_HEAD_DIM) + RMS_EPS) * gain


def _swa_attend(q_groups, k_all, v_all, mask, sink_ref):
    tq = q_groups[0].shape[0]
    lane_head = lax.broadcasted_iota(jnp.int32, (tq, SWA_KV), 1) // SWA_HEAD_DIM
    mask_rows = jnp.concatenate([mask] * SWA_GROUP, axis=0)
    out = [jnp.zeros((tq, SWA_KV), F32) for _ in range(SWA_GROUP)]
    for kv in range(SWA_KV_HEADS):
        in_head = lane_head == kv
        qs = jnp.concatenate([jnp.where(in_head, qg, 0.0) for qg in q_groups], axis=0).astype(BF16)
        s = lax.dot_general(qs, k_all, (((1,), (1,)), ((), ())), preferred_element_type=F32)
        s = jnp.where(mask_rows, s, -jnp.inf)
        sink = jnp.concatenate(
            [jnp.full((tq, 1), sink_ref[kv * SWA_GROUP + g], F32) for g in range(SWA_GROUP)], axis=0)
        m = jnp.maximum(jnp.max(s, axis=1, keepdims=True), sink)
        p = jnp.exp(s - m)
        denom = jnp.sum(p, axis=1, keepdims=True) + jnp.exp(sink - m)
        pv = jnp.dot(p.astype(BF16), v_all, preferred_element_type=F32) / denom
        for g in range(SWA_GROUP):
            out[g] = jnp.where(in_head, pv[g * tq:(g + 1) * tq], out[g])
    return out


SWA_STEP_WINDOWS = 8


def _swa_prompt_kernel(sink_ref, cur_ref, prev_ref, gq_ref, gk_ref, gsum_ref, o_ref, k_ref, v_ref):
    n = pl.program_id(1)
    W = WINDOW
    gsum = gsum_ref[...]
    k_cur = _head_group_norm(cur_ref[0, :, SWA_Q:SWA_Q + SWA_KV], gk_ref[...], gsum)
    k_prev = _head_group_norm(prev_ref[0, :, SWA_Q:SWA_Q + SWA_KV], gk_ref[...], gsum)
    k_rows = jnp.concatenate([k_prev, k_cur], axis=0).astype(BF16)
    v_rows = jnp.concatenate([prev_ref[0, :, SWA_Q + SWA_KV:], cur_ref[0, :, SWA_Q + SWA_KV:]], axis=0).astype(BF16)
    t = lax.broadcasted_iota(jnp.int32, (W, 2 * W), 0)
    s = lax.broadcasted_iota(jnp.int32, (W, 2 * W), 1)
    band = (s >= t) & (s <= t + W)
    for w in range(SWA_STEP_WINDOWS):
        rows = slice(w * W, (w + 1) * W)
        mask = band if w > 0 else band & ((s >= W) | (n > 0))
        q_groups = [_head_group_norm(cur_ref[0, rows, g * SWA_KV:(g + 1) * SWA_KV], gq_ref[...], gsum)
                    * (SWA_HEAD_DIM ** -0.5) for g in range(SWA_GROUP)]
        out = _swa_attend(q_groups, k_rows[w * W:(w + 2) * W], v_rows[w * W:(w + 2) * W], mask, sink_ref)
        o_ref[0, rows, :] = jnp.concatenate(out, axis=1)
    last = slice((SWA_STEP_WINDOWS - 1) * W, SWA_STEP_WINDOWS * W)
    k_ref[0] = k_cur[last]
    v_ref[0] = cur_ref[0, last, SWA_Q + SWA_KV:]


def _swa_gsum():
    head = jnp.arange(SWA_KV) // SWA_HEAD_DIM
    return (head[:, None] == head[None, :]).astype(BF16)


def _swa_prompt(proj, gq, gk, sinks, batch, seq_len):
    W = WINDOW
    nw = SWA_STEP_WINDOWS
    width = SWA_Q + 2 * SWA_KV
    p3 = proj.reshape(batch, seq_len, width)
    nbytes = 2 * ((nw + 1) * W * width * 4 + nw * W * SWA_Q * 4 + 2 * W * SWA_KV * 4) + nw * 48 * W * 2 * W * 4
    o, k, v = pl.pallas_call(
        _swa_prompt_kernel,
        grid=(batch, seq_len // (nw * W)),
        in_specs=[pl.BlockSpec(memory_space=pltpu.SMEM),
                  pl.BlockSpec((1, nw * W, width), lambda b, n: (b, n, 0)),
                  pl.BlockSpec((1, W, width), lambda b, n: (b, jnp.maximum(nw * n - 1, 0), 0)),
                  pl.BlockSpec((1, SWA_KV), lambda b, n: (0, 0)),
                  pl.BlockSpec((1, SWA_KV), lambda b, n: (0, 0)),
                  pl.BlockSpec((SWA_KV, SWA_KV), lambda b, n: (0, 0))],
        out_specs=[pl.BlockSpec((1, nw * W, SWA_Q), lambda b, n: (b, n, 0)),
                   pl.BlockSpec((1, W, SWA_KV), lambda b, n: (b, 0, 0)),
                   pl.BlockSpec((1, W, SWA_KV), lambda b, n: (b, 0, 0))],
        out_shape=[jax.ShapeDtypeStruct((batch, seq_len, SWA_Q), F32),
                   jax.ShapeDtypeStruct((batch, W, SWA_KV), F32),
                   jax.ShapeDtypeStruct((batch, W, SWA_KV), F32)],
        compiler_params=_params(("parallel", "arbitrary"), nbytes),
        name="swa_prompt",
    )(sinks, p3, p3, jnp.tile(gq, SWA_KV_HEADS).reshape(1, SWA_KV), jnp.tile(gk, SWA_KV_HEADS).reshape(1, SWA_KV),
      _swa_gsum())
    return o.reshape(batch * seq_len, SWA_Q), k, v


SWA_SAMPLE_BATCH = 8


def _swa_sample_kernel(sink_ref, p_ref, kc_ref, vc_ref, gq_ref, gk_ref, gsum_ref, o_ref, k_ref, v_ref):
    T = p_ref.shape[1]
    gsum = gsum_ref[...]
    t = lax.broadcasted_iota(jnp.int32, (T, WINDOW + T), 0)
    s = lax.broadcasted_iota(jnp.int32, (T, WINDOW + T), 1)
    mask = (s >= t) & (s <= t + WINDOW)
    for j in range(p_ref.shape[0]):
        k_new = _head_group_norm(p_ref[j, :, SWA_Q:SWA_Q + SWA_KV], gk_ref[...], gsum)
        v_new = p_ref[j, :, SWA_Q + SWA_KV:]
        k_all = jnp.concatenate([kc_ref[j], k_new], axis=0)
        v_all = jnp.concatenate([vc_ref[j], v_new], axis=0)
        q_groups = [_head_group_norm(p_ref[j, :, g * SWA_KV:(g + 1) * SWA_KV], gq_ref[...], gsum)
                    * (SWA_HEAD_DIM ** -0.5) for g in range(SWA_GROUP)]
        out = _swa_attend(q_groups, k_all.astype(BF16), v_all.astype(BF16), mask, sink_ref)
        o_ref[j] = jnp.concatenate(out, axis=1)
        k_ref[j] = k_all[T:]
        v_ref[j] = v_all[T:]


def _swa_sample(proj, k_cache, v_cache, gq, gk, sinks, batch, seq_len):
    nb = SWA_SAMPLE_BATCH
    W = WINDOW
    width = SWA_Q + 2 * SWA_KV
    p3 = proj.reshape(batch, seq_len, width)
    nbytes = 2 * (nb * seq_len * (width + SWA_Q) * 4 + 4 * nb * W * SWA_KV * 4)
    o, k, v = pl.pallas_call(
        _swa_sample_kernel,
        grid=(batch // nb,),
        in_specs=[pl.BlockSpec(memory_space=pltpu.SMEM),
                  pl.BlockSpec((nb, seq_len, width), lambda b: (b, 0, 0)),
                  pl.BlockSpec((nb, W, SWA_KV), lambda b: (b, 0, 0)),
                  pl.BlockSpec((nb, W, SWA_KV), lambda b: (b, 0, 0)),
                  pl.BlockSpec((1, SWA_KV), lambda b: (0, 0)),
                  pl.BlockSpec((1, SWA_KV), lambda b: (0, 0)),
                  pl.BlockSpec((SWA_KV, SWA_KV), lambda b: (0, 0))],
        out_specs=[pl.BlockSpec((nb, seq_len, SWA_Q), lambda b: (b, 0, 0)),
                   pl.BlockSpec((nb, W, SWA_KV), lambda b: (b, 0, 0)),
                   pl.BlockSpec((nb, W, SWA_KV), lambda b: (b, 0, 0))],
        out_shape=[jax.ShapeDtypeStruct((batch, seq_len, SWA_Q), F32),
                   jax.ShapeDtypeStruct((batch, W, SWA_KV), F32),
                   jax.ShapeDtypeStruct((batch, W, SWA_KV), F32)],
        compiler_params=_params(("parallel",), nbytes),
        name="swa_sample",
    )(sinks, p3, k_cache, v_cache, jnp.tile(gq, SWA_KV_HEADS).reshape(1, SWA_KV),
      jnp.tile(gk, SWA_KV_HEADS).reshape(1, SWA_KV), _swa_gsum())
    return o.reshape(batch * seq_len, SWA_Q), k, v


def _lru_conv(x, shifted, cw_ref, cb_ref):
    y = cb_ref[...] + cw_ref[CONV_WIDTH - 1:CONV_WIDTH] * x
    for s in range(1, CONV_WIDTH):
        y = y + cw_ref[CONV_WIDTH - 1 - s:CONV_WIDTH - s] * shifted[s - 1]
    return y


def _block_diag_dot(x, w_ref):
    xb = x.astype(BF16)
    return jnp.concatenate(
        [jnp.dot(xb[:, n * LRU_BLOCK:(n + 1) * LRU_BLOCK], w_ref[n], preferred_element_type=F32)
         for n in range(LRU_BLOCKS)], axis=1)


def _lru_terms(xc, wga_ref, bga_ref, wgx_ref, bgx_ref, lam_ref):
    r = _sigmoid(_block_diag_dot(xc, wga_ref) + bga_ref[...])
    i = _sigmoid(_block_diag_dot(xc, wgx_ref) + bgx_ref[...])
    log_a = (-LRU_C) * r * _softplus(-lam_ref[...])
    a = jnp.exp(log_a)
    y2 = 2.0 * log_a
    u = a * a
    em1 = jnp.where(u == 1.0, y2, jnp.where(u == 0.0, -1.0, (u - 1.0) * y2 / jnp.log(u)))
    mult = jnp.sqrt(-em1)
    return a, mult * i * xc


def _scan_rows(a, b, group):
    rows = a.shape[0]
    pos = lax.broadcasted_iota(jnp.int32, a.shape, 0) % group
    d = 1
    while d < group:
        keep = pos >= d
        b = jnp.where(keep, a * pltpu.roll(b, d, 0) + b, b)
        a = jnp.where(keep, a * pltpu.roll(a, d, 0), a)
        d *= 2
    return a, b


def _lru_prompt_kernel(p_ref, cw_ref, cb_ref, wga_ref, bga_ref, wgx_ref, bgx_ref, lam_ref,
                       o_ref, h_ref, tail_ref, hc_ref):
    n = pl.program_id(1)
    rows = p_ref.shape[1]

    @pl.when(n == 0)
    def _():
        tail_ref[...] = jnp.zeros_like(tail_ref)
        hc_ref[...] = jnp.zeros_like(hc_ref)

    y = p_ref[0, :, :D_RNN]
    x = p_ref[0, :, D_RNN:]
    tail = tail_ref[...]
    r8 = lax.broadcasted_iota(jnp.int32, (V7X_SUBLANES, D_RNN), 0)
    shifted = []
    for s in range(1, CONV_WIDTH):
        xs = pltpu.roll(x, s, 0)
        head = jnp.where(r8 < s, pltpu.roll(tail, s, 0), xs[:V7X_SUBLANES])
        shifted.append(jnp.concatenate([head, xs[V7X_SUBLANES:]], axis=0))
    xc = _lru_conv(x, shifted, cw_ref, cb_ref)
    a, bterm = _lru_terms(xc, wga_ref, bga_ref, wgx_ref, bgx_ref, lam_ref)
    acum, hzero = _scan_rows(a, bterm, rows)
    hs = acum * hc_ref[0:1] + hzero
    o_ref[0] = _gelu_tanh(y) * hs
    last = hs[rows - 1:rows]
    h_ref[0] = last
    hc_ref[...] = jnp.broadcast_to(last, hc_ref.shape)
    tail_ref[...] = x[rows - V7X_SUBLANES:]


LRU_TILE = 256


def _lru_weight_specs(imap):
    return [pl.BlockSpec((CONV_WIDTH, D_RNN), imap(2)),
            pl.BlockSpec((1, D_RNN), imap(2)),
            pl.BlockSpec((LRU_BLOCKS, LRU_BLOCK, LRU_BLOCK), imap(3)),
            pl.BlockSpec((1, D_RNN), imap(2)),
            pl.BlockSpec((LRU_BLOCKS, LRU_BLOCK, LRU_BLOCK), imap(3)),
            pl.BlockSpec((1, D_RNN), imap(2)),
            pl.BlockSpec((1, D_RNN), imap(2))]


def _lru_prompt(proj, cw, cb, wga, bga, wgx, bgx, lam, batch, seq_len):
    R = LRU_TILE
    p3 = proj.reshape(batch, seq_len, 2 * D_RNN)
    nbytes = 2 * (R * 3 * D_RNN * 4 + 2 * LRU_BLOCKS * LRU_BLOCK * LRU_BLOCK * 2) + 24 * R * D_RNN * 4
    o, h = pl.pallas_call(
        _lru_prompt_kernel,
        grid=(batch, seq_len // R),
        in_specs=[pl.BlockSpec((1, R, 2 * D_RNN), lambda b, n: (b, n, 0))]
        + _lru_weight_specs(lambda nd: (lambda b, n: (0,) * nd)),
        out_specs=[pl.BlockSpec((1, R, D_RNN), lambda b, n: (b, n, 0)),
                   pl.BlockSpec((1, 1, D_RNN), lambda b, n: (b, 0, 0))],
        out_shape=[jax.ShapeDtypeStruct((batch, seq_len, D_RNN), F32),
                   jax.ShapeDtypeStruct((batch, 1, D_RNN), F32)],
        scratch_shapes=[pltpu.VMEM((V7X_SUBLANES, D_RNN), F32), pltpu.VMEM((V7X_SUBLANES, D_RNN), F32)],
        compiler_params=_params(("parallel", "arbitrary"), nbytes),
        name="lru_prompt",
    )(p3, cw, cb.reshape(1, D_RNN), wga.astype(BF16), bga.reshape(1, D_RNN), wgx.astype(BF16),
      bgx.reshape(1, D_RNN), lam.reshape(1, D_RNN))
    return o.reshape(batch * seq_len, D_RNN), h.reshape(batch, D_RNN)


def _lru_sample_kernel(p_ref, prev_ref, h0_ref, cw_ref, cb_ref, wga_ref, bga_ref, wgx_ref, bgx_ref, lam_ref,
                       o_ref, hs_ref, *, seq_len):
    rows = p_ref.shape[0]
    y = p_ref[:, :D_RNN]
    x = p_ref[:, D_RNN:]
    prev = prev_ref[...]
    pos = lax.broadcasted_iota(jnp.int32, (rows, D_RNN), 0) % seq_len
    shifted = [jnp.where(pos < s, pltpu.roll(prev, rows - seq_len + s, 0), pltpu.roll(x, s, 0))
               for s in range(1, CONV_WIDTH)]
    xc = _lru_conv(x, shifted, cw_ref, cb_ref)
    a, bterm = _lru_terms(xc, wga_ref, bga_ref, wgx_ref, bgx_ref, lam_ref)
    acum, hzero = _scan_rows(a, bterm, seq_len)
    hs = acum * h0_ref[...] + hzero
    o_ref[...] = _gelu_tanh(y) * hs
    hs_ref[...] = hs


def _lru_sample(proj, conv_state, h0, cw, cb, wga, bga, wgx, bgx, lam, batch, seq_len):
    assert seq_len == V7X_SUBLANES
    n = batch * seq_len
    R = LRU_TILE
    prev = jnp.pad(conv_state, ((0, 0), (seq_len - (CONV_WIDTH - 1), 0), (0, 0))).reshape(n, D_RNN)
    h0_rows = jnp.repeat(h0, seq_len, axis=0)
    nbytes = 2 * (R * 6 * D_RNN * 4 + 2 * LRU_BLOCKS * LRU_BLOCK * LRU_BLOCK * 2) + 24 * R * D_RNN * 4
    return pl.pallas_call(
        functools.partial(_lru_sample_kernel, seq_len=seq_len),
        grid=(n // R,),
        in_specs=[pl.BlockSpec((R, 2 * D_RNN), lambda i: (i, 0)),
                  pl.BlockSpec((R, D_RNN), lambda i: (i, 0)),
                  pl.BlockSpec((R, D_RNN), lambda i: (i, 0))]
        + _lru_weight_specs(lambda nd: (lambda i: (0,) * nd)),
        out_specs=[pl.BlockSpec((R, D_RNN), lambda i: (i, 0)),
                   pl.BlockSpec((R, D_RNN), lambda i: (i, 0))],
        out_shape=[jax.ShapeDtypeStruct((n, D_RNN), F32), jax.ShapeDtypeStruct((n, D_RNN), F32)],
        compiler_params=_params(("parallel",), nbytes),
        name="lru_sample",
    )(proj, prev, h0_rows, cw, cb.reshape(1, D_RNN), wga.astype(BF16), bga.reshape(1, D_RNN), wgx.astype(BF16),
      bgx.reshape(1, D_RNN), lam.reshape(1, D_RNN))


def _run(gen):
    while True:
        try:
            next(gen)
        except StopIteration as stop:
            return stop.value


def _lockstep(gens):
    results = [None] * len(gens)
    live = list(range(len(gens)))
    anchor = None
    while live:
        for idx in list(live):
            try:
                gens[idx].send(anchor)
            except StopIteration as stop:
                results[idx] = stop.value
                live.remove(idx)
        if live:
            anchor = yield
    return results


def _topk_rows_steps(s, k):
    n = s.shape[0]
    rid = lax.broadcasted_iota(jnp.int32, s.shape, 0).astype(F32)
    vals, ids = [], []
    for _ in range(k):
        m = jnp.max(s, axis=0, keepdims=True)
        ix = jnp.min(jnp.where(s == m, rid, float(n)), axis=0, keepdims=True)
        vals.append(m)
        ids.append(ix)
        s = jnp.where(rid == ix, -jnp.inf, s)
        anchor = yield
        if anchor is not None:
            s = s + anchor
    return jnp.concatenate(vals, axis=0), jnp.concatenate(ids, axis=0).astype(jnp.int32)


def _zero_from(parts):
    acc = None
    for x in parts:
        bits = pltpu.bitcast(x, jnp.uint32)
        bits = bits.reshape(bits.shape[0] // V7X_SUBLANES, V7X_SUBLANES, bits.shape[1])
        folded = bits[0]
        for r in range(1, bits.shape[0]):
            folded = folded | bits[r]
        acc = folded if acc is None else acc | folded
    cols = [acc[:, t * V7X_LANES:(t + 1) * V7X_LANES] for t in range(acc.shape[1] // V7X_LANES)]
    one = cols[0]
    for t in cols[1:]:
        one = one | t
    zero = lax.shift_right_logical(lax.shift_right_logical(one, jnp.uint32(16)), jnp.uint32(16))
    return pltpu.bitcast(zero, F32)[0:1, 0:1]


def _staircase_candidates(s1, s2):
    K = s1.shape[0]
    sub = V7X_SUBLANES
    first_single = next(a for a in range(K) if K // (a + 1) == 1)
    assert (K - first_single) % sub == 0
    pieces, starts, at = [], [], 0
    for a in range(first_single):
        nb = K // (a + 1)
        rows = -(-nb // sub) * sub
        piece = s1[a:a + 1] + s2[:rows]
        if rows != nb:
            piece = jnp.where(lax.broadcasted_iota(jnp.int32, piece.shape, 0) < nb, piece, -jnp.inf)
        pieces.append(piece)
        starts.append(at)
        at += rows
    pieces.append(s1[first_single:] + s2[0:1])
    return jnp.concatenate(pieces, axis=0), starts, at


ROUTE_STEPS = 2 + 2 * PEER_TOPK


def _route_head_steps(q_ref, sk_ref, stage_ref):
    K = PEER_TOPK
    par = pl.program_id(1) % 2
    for p in range(2):
        qh = q_ref[:, p * PEER_HALF:(p + 1) * PEER_HALF].astype(BF16)
        stage_ref[par, p] = lax.dot_general(sk_ref[0, p].astype(BF16), qh, (((1,), (1,)), ((), ())),
                                            preferred_element_type=F32)
    anchor = yield
    scores = [stage_ref[par, 0], stage_ref[par, 1]]
    if anchor is not None:
        scores = [st + anchor for st in scores]
    (s1, i1), (s2, i2) = yield from _lockstep([_topk_rows_steps(st, K) for st in scores])
    anchor = yield
    cand, starts, single_start = _staircase_candidates(s1, s2)
    if anchor is not None:
        cand = cand + anchor
    top, ci = yield from _topk_rows_steps(cand, K)
    a_id = jnp.zeros_like(ci)
    group_start = jnp.zeros_like(ci)
    for a in range(1, len(starts)):
        a_id = jnp.where(ci >= starts[a], a, a_id)
        group_start = jnp.where(ci >= starts[a], starts[a], group_start)
    single = ci >= single_start
    a_id = jnp.where(single, len(starts) + ci - single_start, a_id)
    b_id = jnp.where(single, 0, ci - group_start)
    e1 = jnp.zeros_like(ci)
    e2 = jnp.zeros_like(ci)
    for a in range(K):
        e1 = jnp.where(a_id == a, i1[a:a + 1], e1)
        e2 = jnp.where(b_id == a, i2[a:a + 1], e2)
    e = jnp.exp(top - top[0:1])
    return e1, e2, e / jnp.sum(e, axis=0, keepdims=True)


def _route_kernel(q_ref, sk_ref, e1_ref, e2_ref, g_ref, stage_ref):
    e1_ref[0], e2_ref[0], g_ref[0] = _run(_route_head_steps(q_ref, sk_ref, stage_ref))


EXPERT_TILE = 256


def _peer_route(q, sub_keys):
    n = q.shape[0]
    tb = EXPERT_TILE
    spec = pl.BlockSpec((1, PEER_TOPK, tb), lambda i, h: (h, 0, i))
    nbytes = 2 * (tb * PEER_KEY_DIM * 4 + 2 * PEER_NKEYS * PEER_HALF * 4) + 16 * 2 * PEER_NKEYS * tb * 4
    return pl.pallas_call(
        _route_kernel,
        grid=(n // tb, PEER_HEADS),
        in_specs=[pl.BlockSpec((tb, PEER_KEY_DIM), lambda i, h: (i, h)),
                  pl.BlockSpec((1, 2, PEER_NKEYS, PEER_HALF), lambda i, h: (h, 0, 0, 0))],
        out_specs=[spec, spec, spec],
        out_shape=[jax.ShapeDtypeStruct((PEER_HEADS, PEER_TOPK, n), jnp.int32),
                   jax.ShapeDtypeStruct((PEER_HEADS, PEER_TOPK, n), jnp.int32),
                   jax.ShapeDtypeStruct((PEER_HEADS, PEER_TOPK, n), F32)],
        scratch_shapes=[pltpu.VMEM((2, 2, PEER_NKEYS, tb), F32)],
        compiler_params=_params(("parallel", "parallel"), nbytes),
        name="peer_route",
    )(q, sub_keys)


EXPERT_CHUNK = PEER_EXPERTS // PEER_HEADS
EXPERT_SUB = 256
WEIGHT_BLOCK = 512
GATE_ROW_PAD = V7X_SUBLANES
GATE_UNROLL = 64


def _expert_kernel(h_ref, q_ref, sk_ref, e1_ref, e2_ref, g_ref, ut_ref, v0_ref, v1_ref, x_ref, gt_ref, o_ref,
                   gate_ref, acc_ref, e1t_ref, e2t_ref, gtt_ref, re1_ref, re2_ref, rg_ref, stage_ref, w_ref):
    i = pl.program_id(0)
    c = pl.program_id(1)
    tb = h_ref.shape[0]
    NK = PEER_NKEYS
    stride = tb + GATE_ROW_PAD
    slot = i % 2

    @pl.when((i == 0) & (c == 0))
    def _():
        re1_ref[0] = e1_ref[...].reshape(PEER_PAIRS, tb)
        re2_ref[0] = e2_ref[...].reshape(PEER_PAIRS, tb)
        rg_ref[0] = g_ref[...].reshape(PEER_PAIRS, tb)

    @pl.when(c == 0)
    def _():
        acc_ref[...] = jnp.zeros_like(acc_ref)
        e1t_ref[...] = jnp.transpose(re1_ref[slot])
        e2t_ref[...] = jnp.transpose(re2_ref[slot])
        gtt_ref[...] = jnp.transpose(rg_ref[slot])
        kid = lax.broadcasted_iota(jnp.int32, (NK, PEER_PAIRS), 0)

        def per_token(n, carry):
            i1 = e1t_ref[pl.ds(n, 1), :]
            i2 = e2t_ref[pl.ds(n, 1), :]
            gg = gtt_ref[pl.ds(n, 1), :]
            a_t = jnp.where(kid == i1, 1.0, 0.0).astype(BF16)
            b_t = jnp.where(kid == i2, 0.5 * gg, 0.0).astype(BF16)
            gn = lax.dot_general(a_t, b_t, (((1,), (1,)), ((), ())), preferred_element_type=F32)
            gate_ref[pl.ds(n, NK, stride=stride), :] = gn
            return carry

        lax.fori_loop(0, tb, per_token, 0, unroll=GATE_UNROLL)

    route = _route_head_steps(q_ref, sk_ref, stage_ref)
    routed = []
    n_sub = EXPERT_CHUNK // EXPERT_SUB

    def advance_route(rounds, anchor=None):
        for _ in range(rounds):
            if routed:
                return
            try:
                route.send(anchor)
            except StopIteration as stop:
                routed.append(stop.value)
            anchor = None

    h = h_ref[...]
    keys_per_chunk = EXPERT_CHUNK // NK
    advance_route(1)
    WB = WEIGHT_BLOCK
    w_parts = []
    for j in range(n_sub):
        lo = j * EXPERT_SUB
        blk, off = divmod(lo, WB)
        s = jnp.dot(h, ut_ref[0, blk, :, off:off + EXPERT_SUB], preferred_element_type=F32)
        gsel = jnp.concatenate(
            [gate_ref[pl.ds(pl.multiple_of((c * keys_per_chunk + lo // NK + t) * stride, V7X_SUBLANES), tb), :]
             for t in range(EXPERT_SUB // NK)], axis=1)
        wb = _gelu_times_half_gate(s, gsel).astype(BF16)
        w_parts.append(wb)
        w_ref[blk, :, off:off + EXPERT_SUB] = wb
    anchor = _zero_from(w_parts)
    for n, vn_ref in enumerate((v0_ref, v1_ref)):
        part = None
        for k in range(EXPERT_CHUNK // WB):
            d = jnp.dot(w_ref[k], vn_ref[0, k * WB:(k + 1) * WB, :], preferred_element_type=F32)
            part = d if part is None else part + d
        acc_ref[:, n * WB:(n + 1) * WB] += part
    advance_route(ROUTE_STEPS + 1, anchor)
    n1, n2, ng = routed[0]
    rows = pl.ds(pl.multiple_of(c * PEER_TOPK, PEER_TOPK), PEER_TOPK)
    re1_ref[1 - slot, rows, :] = n1
    re2_ref[1 - slot, rows, :] = n2
    rg_ref[1 - slot, rows, :] = ng

    @pl.when(c == pl.num_programs(1) - 1)
    def _():
        o_ref[...] = x_ref[...] + gt_ref[0] * acc_ref[...]


def _peer_u_blocks(u):
    nl, e, d = u.shape
    return u.reshape(nl, e // WEIGHT_BLOCK, WEIGHT_BLOCK, d).transpose(0, 1, 3, 2).astype(BF16)


def _peer_experts(h_bf16, q, sub_keys, u_blocks, v_bf16, layer, x, gate, seq_len):
    n, d = x.shape
    tb = EXPERT_TILE
    ec = EXPERT_CHUNK
    wb = WEIGHT_BLOCK
    assert d == 2 * wb
    nt = n // tb
    e1, e2, g = _peer_route(q[:tb], sub_keys)
    gt_arr, gt_spec0 = _row_operand(gate, seq_len, tb)
    gt_spec = pl.BlockSpec(gt_spec0.block_shape, lambda i, c: gt_spec0.index_map(i))
    rspec = pl.BlockSpec((PEER_HEADS, PEER_TOPK, tb), lambda i, c: (0, 0, 0))
    nbytes = (2 * (tb * d * 2 + 3 * PEER_PAIRS * tb * 4 + 2 * d * ec * 2 + 3 * tb * d * 4 + tb * PEER_KEY_DIM * 4)
              + (tb + GATE_ROW_PAD) * PEER_NKEYS * PEER_NKEYS * 4 + tb * d * 4 + 9 * tb * PEER_PAIRS * 4
              + 8 * tb * EXPERT_SUB * 4 + 8 * PEER_NKEYS * tb * 4)
    return pl.pallas_call(
        _expert_kernel,
        grid=(nt, PEER_HEADS),
        in_specs=[pl.BlockSpec((tb, d), lambda i, c: (i, 0)),
                  pl.BlockSpec((tb, PEER_KEY_DIM), lambda i, c: (jnp.minimum(i + 1, nt - 1), c)),
                  pl.BlockSpec((1, 2, PEER_NKEYS, PEER_HALF), lambda i, c: (c, 0, 0, 0)),
                  rspec, rspec, rspec,
                  pl.BlockSpec((1, ec // wb, d, wb), lambda i, c: (layer, c, 0, 0)),
                  pl.BlockSpec((1, ec, wb), lambda i, c: (layer, c, 0)),
                  pl.BlockSpec((1, ec, wb), lambda i, c: (layer, c, 1)),
                  pl.BlockSpec((tb, d), lambda i, c: (i, 0)),
                  gt_spec],
        out_specs=pl.BlockSpec((tb, d), lambda i, c: (i, 0)),
        out_shape=jax.ShapeDtypeStruct((n, d), F32),
        scratch_shapes=[pltpu.VMEM(((tb + GATE_ROW_PAD) * PEER_NKEYS, PEER_NKEYS), F32),
                        pltpu.VMEM((tb, d), F32),
                        pltpu.VMEM((tb, PEER_PAIRS), jnp.int32),
                        pltpu.VMEM((tb, PEER_PAIRS), jnp.int32),
                        pltpu.VMEM((tb, PEER_PAIRS), F32),
                        pltpu.VMEM((2, PEER_PAIRS, tb), jnp.int32),
                        pltpu.VMEM((2, PEER_PAIRS, tb), jnp.int32),
                        pltpu.VMEM((2, PEER_PAIRS, tb), F32),
                        pltpu.VMEM((2, 2, PEER_NKEYS, tb), F32),
                        pltpu.VMEM((ec // wb, tb, wb), BF16)],
        compiler_params=_params(("arbitrary", "arbitrary"), nbytes),
        name="peer_experts",
    )(h_bf16, q, sub_keys, e1, e2, g, u_blocks, v_bf16, v_bf16, x, gt_arr)


def _swa_permute_in(w_in):
    d = w_in.shape[0]
    wq = w_in[:, :SWA_Q].reshape(d, SWA_KV_HEADS, SWA_GROUP, SWA_HEAD_DIM).transpose(0, 2, 1, 3).reshape(d, SWA_Q)
    return jnp.concatenate([wq, w_in[:, SWA_Q:]], axis=1)


def _swa_permute_out(w_out):
    d = w_out.shape[1]
    return w_out.reshape(SWA_KV_HEADS, SWA_GROUP, SWA_HEAD_DIM, d).transpose(1, 0, 2, 3).reshape(SWA_Q, d)


def _prepare_weights(p):
    w = {}
    w['gla_in'] = [jnp.pad(p['w_gla_in'][j], ((0, 0), (0, GLA_IN_PAD - p['w_gla_in'].shape[2]))).astype(BF16)
                   for j in range(p['w_gla_in'].shape[0])]
    w['gla_a2'] = [jnp.pad(p['w_gla_a2'][j], ((0, V7X_LANES - GLA_GATE_RANK), (0, 0)))
                   for j in range(p['w_gla_a2'].shape[0])]
    w['gla_out'] = [m.astype(BF16) for m in p['w_gla_out']]
    w['swa_in'] = [_swa_permute_in(m).astype(BF16) for m in p['w_swa_in']]
    w['swa_out'] = [_swa_permute_out(m).astype(BF16) for m in p['w_swa_out']]
    w['lru_in'] = [m.astype(BF16) for m in p['w_lru_in']]
    w['lru_out'] = [m.astype(BF16) for m in p['w_lru_out']]
    w['peer_q'] = [m.astype(BF16) for m in p['w_peer_q']]
    w['peer_ut'] = _peer_u_blocks(p['peer_u'])
    w['peer_v'] = p['peer_v'].astype(BF16)
    return w


def _trunk(x3, mod, states, p, w):
    batch, seq_len, d = x3.shape
    x = x3.reshape(batch * seq_len, d)
    new_gla, new_k, new_v, new_conv, new_h = [], [], [], [], []
    gla_stack = None
    for i in range(DEPTH):
        kind, j = i % N_MIXERS, i // N_MIXERS
        sh_m, sc_m, gt_m, sh_f, sc_f, gt_f = [mod[i][:, k * d:(k + 1) * d] for k in range(6)]
        if kind == 0:
            proj = _norm_proj(x, p['g_ln_mix'][i], sc_m, sh_m, w['gla_in'][j], seq_len)
            if states is None:
                mix, s_new = _gla_prompt(proj, w['gla_a2'][j], p['b_gla_a'][j], p['g_gla_norm'][j], batch, seq_len)
                new_gla.append(s_new)
            else:
                mix, gla_stack = _gla_sample(proj, states[0], j, gla_stack, w['gla_a2'][j], p['b_gla_a'][j],
                                             p['g_gla_norm'][j], batch, seq_len)
            w_out = w['gla_out'][j]
        elif kind == 1:
            proj = _norm_proj(x, p['g_ln_mix'][i], sc_m, sh_m, w['swa_in'][j], seq_len)
            if states is None:
                mix, k_n, v_n = _swa_prompt(proj, p['g_swa_q'][j], p['g_swa_k'][j], p['swa_sinks'][j], batch, seq_len)
            else:
                kc = states[1][j].reshape(batch, WINDOW, SWA_KV)
                vc = states[2][j].reshape(batch, WINDOW, SWA_KV)
                mix, k_n, v_n = _swa_sample(proj, kc, vc, p['g_swa_q'][j], p['g_swa_k'][j], p['swa_sinks'][j],
                                            batch, seq_len)
            new_k.append(k_n.reshape(batch, WINDOW, SWA_KV_HEADS, SWA_HEAD_DIM))
            new_v.append(v_n.reshape(batch, WINDOW, SWA_KV_HEADS, SWA_HEAD_DIM))
            w_out = w['swa_out'][j]
        else:
            proj = _norm_proj(x, p['g_ln_mix'][i], sc_m, sh_m, w['lru_in'][j], seq_len)
            lru_args = (p['lru_conv_w'][j], p['lru_conv_b'][j], p['w_lru_ga'][j], p['b_lru_ga'][j],
                        p['w_lru_gx'][j], p['b_lru_gx'][j], p['lru_lam'][j], batch, seq_len)
            assert seq_len >= CONV_WIDTH - 1
            if states is None:
                mix, h_n = _lru_prompt(proj, *lru_args)
            else:
                mix, hs = _lru_sample(proj, states[3][j], states[4][j], *lru_args)
                h_n = hs.reshape(batch, seq_len, D_RNN)[:, -1]
            new_conv.append(proj[:, D_RNN:].reshape(batch, seq_len, D_RNN)[:, seq_len - (CONV_WIDTH - 1):])
            new_h.append(h_n)
            w_out = w['lru_out'][j]
        x, q, hb = _residual_then_proj(mix, w_out, x, gt_m, p['g_ln_ffn'][i], sc_f, sh_f, w['peer_q'][i], seq_len)
        x = _peer_experts(hb, q, p['peer_sub_keys'][i], w['peer_ut'], w['peer_v'], i, x, gt_f, seq_len)
    y = x.reshape(batch, seq_len, d)
    gla_out = jnp.stack(new_gla) if states is None else gla_stack
    return y, (gla_out, jnp.stack(new_k), jnp.stack(new_v), jnp.stack(new_conv), jnp.stack(new_h))


def kernel(x_prompt, x_sample, state_gla, cache_swa_k, cache_swa_v, state_lru_conv, state_lru_h,
           c_prompt, c_sample, g_ln_mix, g_ln_ffn, w_mod, b_mod,
           w_gla_in, w_gla_a2, b_gla_a, g_gla_norm, w_gla_out,
           w_swa_in, g_swa_q, g_swa_k, swa_sinks, w_swa_out,
           w_lru_in, lru_conv_w, lru_conv_b, w_lru_ga, b_lru_ga, w_lru_gx, b_lru_gx, lru_lam, w_lru_out,
           w_peer_q, peer_sub_keys, peer_u, peer_v):
    p = {'g_ln_mix': g_ln_mix, 'g_ln_ffn': g_ln_ffn,
         'w_gla_in': w_gla_in, 'w_gla_a2': w_gla_a2, 'b_gla_a': b_gla_a, 'g_gla_norm': g_gla_norm,
         'w_gla_out': w_gla_out,
         'w_swa_in': w_swa_in, 'g_swa_q': g_swa_q, 'g_swa_k': g_swa_k, 'swa_sinks': swa_sinks,
         'w_swa_out': w_swa_out,
         'w_lru_in': w_lru_in, 'lru_conv_w': lru_conv_w, 'lru_conv_b': lru_conv_b,
         'w_lru_ga': w_lru_ga, 'b_lru_ga': b_lru_ga, 'w_lru_gx': w_lru_gx, 'b_lru_gx': b_lru_gx,
         'lru_lam': lru_lam, 'w_lru_out': w_lru_out,
         'w_peer_q': w_peer_q, 'peer_sub_keys': peer_sub_keys, 'peer_u': peer_u, 'peer_v': peer_v}
    w = _prepare_weights(p)
    nb_p, nb_s = c_prompt.shape[0], c_sample.shape[0]
    rows = -(-(nb_p + nb_s) // V7X_SUBLANES) * V7X_SUBLANES
    c_all = jnp.pad(jnp.concatenate([c_prompt, c_sample], axis=0), ((0, rows - nb_p - nb_s), (0, 0)))
    mod = _modulation(c_all, w_mod, b_mod)
    y_p, (gla_p, k_p, v_p, conv_p, h_p) = _trunk(x_prompt, mod[:, :nb_p], None, p, w)
    y_s, (gla_s, k_s, v_s, conv_s, h_s) = _trunk(
        x_sample, mod[:, nb_p:nb_p + nb_s],
        (state_gla, cache_swa_k, cache_swa_v, state_lru_conv, state_lru_h), p, w)
    return (y_p, y_s, gla_p, gla_s, k_p, k_s, v_p, v_s, conv_p, conv_s, h_p, h_s)
```
